```python
import jax, jax.numpy as jnp
from jax import lax
import numpy as np

D_MODEL = 2048
BATCH = 16
SEQ = 256
DEPTH = 4
DEC_BATCH = 8
DEC_SEQ = 1024
PAST_LEN = 256

GRID_W = 64
N_AB = (DEPTH + 1) // 2
N_POOL = DEPTH // 2
C_A = D_MODEL // 2
HEAD_A = 64
H_A = C_A // HEAD_A
DECAY_LORA = 64
ICLR_LORA = 64
GATE_LORA = 160
C_B = D_MODEL // 2
HEAD_B = 128
H_B = C_B // HEAD_B
CONV_W = 3
CHUNK = 64
C_PA = 3 * C_A + 2 * DECAY_LORA + 2 * ICLR_LORA + GATE_LORA
C_PB = 4 * C_B + 4 * H_B
C_IN = C_PA + C_PB
POOL_WINDOWS = (2, 4, 8, 16)
N_POOL_GROUPS = 4
C_G = D_MODEL // N_POOL_GROUPS
D_FF = ((8 * D_MODEL + 3 * 256 - 1) // (3 * 256)) * 256
RMS_EPS = 1e-6
GN_EPS = 64e-5

kernel_name = 'hybrid_rwkv7_gdn_pool_dit_step'


def _rmsnorm(x, w):
    xf = x.astype(jnp.float32)
    y = xf * lax.rsqrt(jnp.mean(xf * xf, -1, keepdims=True) + RMS_EPS)
    return (y * w.astype(jnp.float32)).astype(x.dtype)


def _l2norm(x, eps):
    return x * lax.rsqrt(jnp.sum(x * x, -1, keepdims=True) + eps)


def _from_prev(x, axis):
    pad = [(0, 0)] * x.ndim
    pad[axis] = (1, 0)
    return lax.slice_in_dim(jnp.pad(x, pad), 0, x.shape[axis], axis=axis)


def _from_next(x, axis):
    pad = [(0, 0)] * x.ndim
    pad[axis] = (0, 1)
    return lax.slice_in_dim(jnp.pad(x, pad), 1, x.shape[axis] + 1, axis=axis)


def _token_shift(p, grid):
    b, t, ch = p.shape
    if grid:
        rows = t // GRID_W
        p4 = p.reshape(b, rows, GRID_W, ch // 4, 4)
        s = jnp.stack([_from_prev(p4[..., 0], 2), _from_next(p4[..., 1], 2),
                       _from_prev(p4[..., 2], 1), _from_next(p4[..., 3], 1)], -1)
    else:
        p2 = p.reshape(b, t, ch // 2, 2)
        s = jnp.stack([_from_prev(p2[..., 0], 1), _from_next(p2[..., 1], 1)], -1)
    return s.reshape(b, t, ch)


def _centred_dwconv(x, w):
    t = x.shape[1]
    half = CONV_W // 2
    xp = jnp.pad(x, ((0, 0), (half, CONV_W - 1 - half), (0, 0)))
    return sum(xp[:, j:j + t] * w[j] for j in range(CONV_W))


def _groupnorm_heads(y, w, b):
    mu = jnp.mean(y, -1, keepdims=True)
    var = jnp.mean(jnp.square(y - mu), -1, keepdims=True)
    return (y - mu) * lax.rsqrt(var + GN_EPS) * w + b


def _rwkv7_scan(r, w, k, v, a, bb, s0, reverse):
    def step(S, inp):
        r_t, w_t, k_t, v_t, a_t, b_t = inp
        sa = jnp.einsum('bhvk,bhk->bhv', S, a_t)
        S = S * w_t[:, :, None, :] + sa[..., None] * b_t[:, :, None, :] + v_t[..., None] * k_t[:, :, None, :]
        return S, jnp.einsum('bhvk,bhk->bhv', S, r_t)
    xs = tuple(jnp.moveaxis(z, 1, 0) for z in (r, w, k, v, a, bb))
    S, ys = lax.scan(step, s0, xs, reverse=reverse)
    return jnp.moveaxis(ys, 0, 1), S


def _gdn_chunked(q, k, v, g, beta, s0):
    bn, t, h, dk = q.shape
    dv = v.shape[-1]
    n = t // CHUNK

    def blk(z):
        return jnp.moveaxis(z.reshape(bn, n, CHUNK, *z.shape[2:]), 3, 2)
    q = blk(q) * (dk ** -0.5)
    k = blk(k)
    v = blk(v)
    beta = blk(beta)
    g = jnp.cumsum(blk(g), axis=-1)
    idx = jnp.arange(CHUNK)
    incl = idx[:, None] >= idx[None, :]
    strict = idx[:, None] > idx[None, :]
    decay = jnp.exp(jnp.where(incl, g[..., :, None] - g[..., None, :], -jnp.inf))
    kb = k * beta[..., None]
    lmat = jnp.einsum('bnhik,bnhjk->bnhij', kb, k) * decay * strict
    amat = lmat + jnp.eye(CHUNK, dtype=lmat.dtype)
    rhs = jnp.concatenate([v * beta[..., None], kb * jnp.exp(g)[..., None]], -1)
    sol = lax.linalg.triangular_solve(amat, rhs, left_side=True, lower=True, unit_diagonal=True)
    u, wy = sol[..., :dv], sol[..., dv:]
    attn = jnp.einsum('bnhik,bnhjk->bnhij', q, k) * decay
    qg = q * jnp.exp(g)[..., None]
    g_last = g[..., -1]
    k_dec = k * jnp.exp(g_last[..., None] - g)[..., None]

    def step(S, inp):
        qg_c, w_c, u_c, attn_c, kd_c, gl_c = inp
        v_new = u_c - jnp.einsum('bhck,bhkv->bhcv', w_c, S)
        o = jnp.einsum('bhck,bhkv->bhcv', qg_c, S) + jnp.einsum('bhij,bhjv->bhiv', attn_c, v_new)
        S = S * jnp.exp(gl_c)[..., None, None] + jnp.einsum('bhck,bhcv->bhkv', kd_c, v_new)
        return S, o
    xs = tuple(jnp.moveaxis(z, 1, 0) for z in (qg, wy, u, attn, k_dec, g_last))
    S, o = lax.scan(step, s0, xs)
    o = jnp.transpose(o, (1, 0, 3, 2, 4)).reshape(bn, t, h, dv)
    return o, S


def _mixer_ab(h, grid, s_rwkv0, s_delta0, w_in, w_out, mu, w0, w2, a0, a2, g2, k_k, k_a, r_k,
              ln_w, ln_b, conv_w, a_log, dt_bias, gn_w):
    f32 = jnp.float32
    b, t, _ = h.shape
    p = h @ w_in
    pa = p[..., :C_PA]
    pa = (pa + (_token_shift(pa, grid) - pa) * mu).astype(f32)
    r, k, v, w_lo, a_lo, g_lo = jnp.split(
        pa, [C_A, 2 * C_A, 3 * C_A, 3 * C_A + 2 * DECAY_LORA, 3 * C_A + 2 * DECAY_LORA + 2 * ICLR_LORA], -1)
    w_lo = jnp.tanh(w_lo.reshape(b, t, 2, DECAY_LORA))
    a_lo = a_lo.reshape(b, t, 2, ICLR_LORA)
    w_log = -jax.nn.softplus(-(w0 + jnp.einsum('btdr,drc->btdc', w_lo, w2))) - 0.5
    decay = jnp.exp(-jnp.exp(w_log))
    iclr = jax.nn.sigmoid(a0 + jnp.einsum('btdr,drc->btdc', a_lo, a2))
    gate = jax.nn.sigmoid(g_lo) @ g2

    def heads(z):
        return z.reshape(b, t, H_A, HEAD_A)
    kk = _l2norm(heads(k * k_k), 1e-12)
    rh, vh = heads(r), heads(v)
    lnw, lnb = ln_w.reshape(H_A, HEAD_A), ln_b.reshape(H_A, HEAD_A)
    ys, fin_a = [], []
    for d in range(2):
        k_d = heads(k * (1 + (iclr[:, :, d] - 1) * k_a))
        y_d, s_d = _rwkv7_scan(rh, heads(decay[:, :, d]), k_d, vh, -kk, kk * heads(iclr[:, :, d]),
                               s_rwkv0[:, d].astype(f32), d == 1)
        ys.append(_groupnorm_heads(y_d, lnw, lnb) + jnp.sum(rh * k_d * r_k, -1, keepdims=True) * vh)
        fin_a.append(s_d)
    o_a = (ys[0] + ys[1]).reshape(b, t, C_A) * gate
    pb = p[..., C_PA:]
    qkv = jax.nn.silu(_centred_dwconv(pb[..., :3 * C_B], conv_w)).astype(f32)
    qd, kd, vd = jnp.split(qkv, 3, -1)
    qd = _l2norm(qd.reshape(b, t, H_B, HEAD_B), 1e-6)
    kd = _l2norm(kd.reshape(b, t, H_B, HEAD_B), 1e-6)
    vd = vd.reshape(b, t, H_B, HEAD_B)
    z = pb[..., 3 * C_B:4 * C_B].astype(f32).reshape(b, t, H_B, HEAD_B)
    beta = jax.nn.sigmoid(pb[..., 4 * C_B:4 * C_B + 2 * H_B].astype(f32)).reshape(b, t, 2, H_B)
    alpha = pb[..., 4 * C_B + 2 * H_B:].astype(f32).reshape(b, t, 2, H_B)
    g_log = -jnp.exp(a_log) * jax.nn.softplus(alpha + dt_bias)
    o_f, s_f = _gdn_chunked(qd, kd, vd, g_log[:, :, 0], beta[:, :, 0], s_delta0[:, 0].astype(f32))
    fl = lambda u_: jnp.flip(u_, 1)
    o_r, s_r = _gdn_chunked(fl(qd), fl(kd), fl(vd), fl(g_log[:, :, 1]), fl(beta[:, :, 1]),
                            s_delta0[:, 1].astype(f32))
    o_bd = o_f + fl(o_r)
    o_bd = o_bd * lax.rsqrt(jnp.mean(o_bd * o_bd, -1, keepdims=True) + RMS_EPS) * gn_w * jax.nn.silu(z)
    o_b = o_bd.reshape(b, t, C_B)
    out = jnp.concatenate([o_a, o_b], -1).astype(h.dtype) @ w_out
    return out, jnp.stack(fin_a, 1).astype(h.dtype), jnp.stack([s_f, s_r], 1).astype(h.dtype)


def _multiscale_pool(h, w_pool, scale):
    f32 = jnp.float32
    b, t, _ = h.shape
    hf = h.astype(f32)
    cs = jnp.pad(jnp.cumsum(hf, 1), ((0, 0), (1, 0), (0, 0)))
    pos = jnp.arange(t)
    means = []
    for gi, win in enumerate(POOL_WINDOWS):
        lo = jnp.clip(pos - win // 2, 0, t)
        hi = jnp.clip(pos - win // 2 + win, 0, t)
        cg = cs[..., gi * C_G:(gi + 1) * C_G]
        s = jnp.take(cg, hi, axis=1) - jnp.take(cg, lo, axis=1)
        means.append(s / (hi - lo).astype(f32)[:, None])
    p = jnp.stack(means, 2) - hf.reshape(b, t, N_POOL_GROUPS, C_G)
    y = jnp.einsum('btgc,gce->btge', p, w_pool).reshape(b, t, D_MODEL)
    return (y * scale).astype(h.dtype)


def _swiglu(h, wg, wu, wd):
    return (jax.nn.silu(h @ wg) * (h @ wu)) @ wd


def _trunk(x, cond, grid, s_rwkv0, s_delta0, shared, ab, pool, keep_state):
    mod_w, mod_b, norm_mix_w, norm_ffn_w, norm_final_w, w_gate, w_up, w_down = shared
    pool_w, pool_scale = pool
    cond_act = jax.nn.silu(cond)
    st_r, st_d = [], []
    for l in range(DEPTH):
        mod = (cond_act @ mod_w[l] + mod_b[l]).reshape(cond.shape[0], 1, 6, D_MODEL)
        shift_m, scale_m, gate_m, shift_f, scale_f, gate_f = (mod[:, :, j] for j in range(6))
        h = _rmsnorm(x, norm_mix_w[l]) * (1 + scale_m) + shift_m
        i = l // 2
        if l % 2 == 0:
            out, s_r, s_d = _mixer_ab(h, grid, s_rwkv0[:, i], s_delta0[:, i], *(prm[i] for prm in ab))
            if keep_state:
                st_r.append(s_r)
                st_d.append(s_d)
        else:
            out = _multiscale_pool(h, pool_w[i], pool_scale[i])
        x = x + gate_m * out
        h = _rmsnorm(x, norm_ffn_w[l]) * (1 + scale_f) + shift_f
        x = x + gate_f * _swiglu(h, w_gate[l], w_up[l], w_down[l])
    y = _rmsnorm(x, norm_final_w)
    if keep_state:
        return y, jnp.stack(st_r, 1), jnp.stack(st_d, 1)
    return y, None, None


def setup_inputs(seed: int = 0) -> dict:
    key = jax.random.key(seed)
    ks = iter(jax.random.split(key, 48))
    f32 = jnp.float32
    D = D_MODEL

    def nrm(shape, s):
        return jax.random.normal(next(ks), shape, f32) * s

    def uni(shape, lo, hi):
        return jax.random.uniform(next(ks), shape, f32, minval=lo, maxval=hi)

    dt = jnp.exp(uni((N_AB, 2, H_B), float(np.log(1e-3)), float(np.log(1e-1))))
    return {
        'x_prompt': nrm((BATCH, SEQ, D), 1.0),
        'x_sample': nrm((DEC_BATCH, DEC_SEQ, D), 1.0),
        'state_rwkv': nrm((DEC_BATCH, N_AB, 2, H_A, HEAD_A, HEAD_A), 1.0),
        'state_delta': nrm((DEC_BATCH, N_AB, 2, H_B, HEAD_B, HEAD_B), 0.3),
        'c': nrm((DEC_BATCH, D), 1.0),
        'c_ctx': nrm((D,), 1.0),
        'mod_w': nrm((DEPTH, D, 6 * D), 0.5 * D ** -0.5),
        'mod_b': nrm((DEPTH, 6 * D), 0.02),
        'norm_mix_w': 1.0 + nrm((DEPTH, D), 0.05),
        'norm_ffn_w': 1.0 + nrm((DEPTH, D), 0.05),
        'norm_final_w': 1.0 + nrm((D,), 0.05),
        'ab_w_in': nrm((N_AB, D, C_IN), D ** -0.5),
        'ab_w_out': nrm((N_AB, C_A + C_B, D), (C_A + C_B) ** -0.5),
        'rwkv_mu': uni((N_AB, C_PA), 0.0, 1.0),
        'rwkv_w0': uni((N_AB, 2, C_A), -6.0, -1.0),
        'rwkv_w2': nrm((N_AB, 2, DECAY_LORA, C_A), 0.1 * DECAY_LORA ** -0.5),
        'rwkv_a0': nrm((N_AB, 2, C_A), 0.1),
        'rwkv_a2': nrm((N_AB, 2, ICLR_LORA, C_A), 0.5 * ICLR_LORA ** -0.5),
        'rwkv_g2': nrm((N_AB, GATE_LORA, C_A), GATE_LORA ** -0.5),
        'rwkv_k_k': 0.85 + nrm((N_AB, C_A), 0.05),
        'rwkv_k_a': 1.0 + nrm((N_AB, C_A), 0.05),
        'rwkv_r_k': nrm((N_AB, H_A, HEAD_A), 0.1),
        'rwkv_ln_w': 1.0 + nrm((N_AB, C_A), 0.05),
        'rwkv_ln_b': nrm((N_AB, C_A), 0.01),
        'gdn_conv_w': nrm((N_AB, CONV_W, 3 * C_B), CONV_W ** -0.5),
        'gdn_a_log': jnp.log(uni((N_AB, 2, H_B), 1.0, 16.0)),
        'gdn_dt_bias': dt + jnp.log(-jnp.expm1(-dt)),
        'gdn_norm_w': 1.0 + nrm((N_AB, HEAD_B), 0.05),
        'pool_w': nrm((N_POOL, N_POOL_GROUPS, C_G, C_G), C_G ** -0.5),
        'pool_scale': 1.0 + nrm((N_POOL, D), 0.1),
        'ffn_w_gate': nrm((DEPTH, D, D_FF), D ** -0.5),
        'ffn_w_up': nrm((DEPTH, D, D_FF), D ** -0.5),
        'ffn_w_down': nrm((DEPTH, D_FF, D), D_FF ** -0.5),
    }


def reference(x_prompt, x_sample, state_rwkv, state_delta, c, c_ctx, mod_w, mod_b, norm_mix_w,
              norm_ffn_w, norm_final_w, ab_w_in, ab_w_out, rwkv_mu, rwkv_w0, rwkv_w2, rwkv_a0, rwkv_a2,
              rwkv_g2, rwkv_k_k, rwkv_k_a, rwkv_r_k, rwkv_ln_w, rwkv_ln_b, gdn_conv_w, gdn_a_log,
              gdn_dt_bias, gdn_norm_w, pool_w, pool_scale, ffn_w_gate, ffn_w_up, ffn_w_down):
    shared = (mod_w, mod_b, norm_mix_w, norm_ffn_w, norm_final_w, ffn_w_gate, ffn_w_up, ffn_w_down)
    ab = (ab_w_in, ab_w_out, rwkv_mu, rwkv_w0, rwkv_w2, rwkv_a0, rwkv_a2, rwkv_g2, rwkv_k_k, rwkv_k_a,
          rwkv_r_k, rwkv_ln_w, rwkv_ln_b, gdn_conv_w, gdn_a_log, gdn_dt_bias, gdn_norm_w)
    pool = (pool_w, pool_scale)
    bp = x_prompt.shape[0]
    zero_r = jnp.zeros((bp, N_AB, 2, H_A, HEAD_A, HEAD_A), x_prompt.dtype)
    zero_d = jnp.zeros((bp, N_AB, 2, H_B, HEAD_B, HEAD_B), x_prompt.dtype)
    y_prompt, new_state_rwkv, new_state_delta = _trunk(x_prompt, c_ctx[None, :], False, zero_r, zero_d,
                                                       shared, ab, pool, True)
    y_sample, _, _ = _trunk(x_sample, c, True, state_rwkv, state_delta, shared, ab, pool, False)
    return (y_prompt, y_sample, new_state_rwkv, new_state_delta)
```

```python
import functools

import jax
import jax.numpy as jnp
from jax import lax
from jax.experimental import pallas as pl
from jax.experimental.pallas import tpu as pltpu

F32 = jnp.float32
BF16 = jnp.bfloat16

D_MODEL = 2048
DEPTH = 4
GRID_W = 64
N_AB = 2
C_A = 1024
HEAD_A = 64
H_A = 16
LORA = 64
GATE_LORA = 160
C_B = 1024
HEAD_B = 128
H_B = 8
C_PA = 3 * C_A + 4 * LORA + GATE_LORA
C_PA_PAD = 3584
C_LO = C_PA_PAD - 3 * C_A
POOL_GROUPS = 4
C_G = D_MODEL // POOL_GROUPS
D_FF = 5632
RMS_EPS = 1e-6
GN_EPS = 64e-5
CHUNK = 64
ROW_BLK = 256
LANES = 128
VMEM_LIMIT = 48 * 1024 * 1024

NT = (((1,), (1,)), ((), ()))
TN = (((0,), (0,)), ((), ()))
NN = (((1,), (0,)), ((), ()))


def _cparams(*sem):
    return pltpu.CompilerParams(dimension_semantics=sem, vmem_limit_bytes=VMEM_LIMIT)


def _bdot(a, b, dims=NN):
    return lax.dot_general(a.astype(BF16), b.astype(BF16), dims, preferred_element_type=F32)


def _fdot(a, b, dims=NN):
    return lax.dot_general(a, b, dims, precision=lax.Precision.HIGHEST, preferred_element_type=F32)


def _sigmoid(x):
    return 1.0 / (1.0 + jnp.exp(-x))


def _silu(x):
    return x * _sigmoid(x)


def _softplus(x):
    return jnp.maximum(x, 0.0) + jnp.log(1.0 + jnp.exp(-jnp.abs(x)))


def _mod_kernel(c_ref, w_ref, b_ref, o_ref):
    act = _silu(c_ref[...])
    o_ref[...] = _fdot(act, w_ref[...]) + b_ref[...]


def _mod_all(cond, mod_w, mod_b):
    n_c, d = cond.shape
    depth, _, n = mod_w.shape
    tn = 1024
    return pl.pallas_call(
        _mod_kernel,
        grid=(depth, n // tn),
        in_specs=[
            pl.BlockSpec((n_c, d), lambda l, j: (0, 0)),
            pl.BlockSpec((None, d, tn), lambda l, j: (l, 0, j)),
            pl.BlockSpec((None, 1, tn), lambda l, j: (l, 0, j)),
        ],
        out_specs=pl.BlockSpec((None, n_c, tn), lambda l, j: (l, 0, j)),
        out_shape=jax.ShapeDtypeStruct((depth, n_c, n), F32),
        compiler_params=_cparams("parallel", "parallel"),
        name="mod_all",
    )(cond, mod_w, mod_b.reshape(depth, 1, n))


def _norm_mod_kernel(x_ref, nw_ref, sc_ref, sh_ref, o_ref):
    x = x_ref[...]
    ms = jnp.mean(x * x, -1, keepdims=True)
    y = x * lax.rsqrt(ms + RMS_EPS) * nw_ref[...]
    o_ref[...] = (y * (1.0 + sc_ref[...]) + sh_ref[...]).astype(o_ref.dtype)


def _norm_mod(x, nw, scale, shift, out_dtype):
    m, d = x.shape
    nb = m // ROW_BLK
    return pl.pallas_call(
        _norm_mod_kernel,
        grid=(nb,),
        in_specs=[
            pl.BlockSpec((ROW_BLK, d), lambda i: (i, 0)),
            pl.BlockSpec((1, d), lambda i: (0, 0)),
            pl.BlockSpec((None, 1, d), lambda i: (i, 0, 0)),
            pl.BlockSpec((None, 1, d), lambda i: (i, 0, 0)),
        ],
        out_specs=pl.BlockSpec((ROW_BLK, d), lambda i: (i, 0)),
        out_shape=jax.ShapeDtypeStruct((m, d), out_dtype),
        compiler_params=_cparams("parallel"),
        name="norm_mod",
    )(x, nw.reshape(1, d), scale, shift)


def _mm_kernel(a_ref, b_ref, o_ref):
    o_ref[...] = jnp.dot(a_ref[...], b_ref[...], preferred_element_type=F32).astype(o_ref.dtype)


def _matmul(a, b, out_dtype, tm=512, tn=512):
    m, k = a.shape
    _, n = b.shape
    tn = min(tn, n)
    return pl.pallas_call(
        _mm_kernel,
        grid=(n // tn, m // tm),
        in_specs=[
            pl.BlockSpec((tm, k), lambda j, i: (i, 0)),
            pl.BlockSpec((k, tn), lambda j, i: (0, j)),
        ],
        out_specs=pl.BlockSpec((tm, tn), lambda j, i: (i, j)),
        out_shape=jax.ShapeDtypeStruct((m, n), out_dtype),
        compiler_params=_cparams("parallel", "parallel"),
        name="matmul",
    )(a, b)


def _gated_residual(x_ref, g_ref, acc, o_ref, tm):
    for s in range(tm // ROW_BLK):
        rows = slice(s * ROW_BLK, (s + 1) * ROW_BLK)
        o_ref[rows, :] = x_ref[rows, :] + g_ref[s] * acc[rows, :]


def _mm_res_kernel(a_ref, b_ref, x_ref, g_ref, o_ref, *, tm):
    acc = jnp.dot(a_ref[...], b_ref[...], preferred_element_type=F32)
    _gated_residual(x_ref, g_ref, acc, o_ref, tm)


def _mm2_res_kernel(a1_ref, a2_ref, b1_ref, b2_ref, x_ref, g_ref, o_ref, *, tm):
    acc = jnp.dot(a1_ref[...], b1_ref[...], preferred_element_type=F32)
    acc = acc + jnp.dot(a2_ref[...], b2_ref[...], preferred_element_type=F32)
    _gated_residual(x_ref, g_ref, acc, o_ref, tm)


def _matmul_residual(a_list, b_list, x, gate, tm=512, tn=512):
    m, n = x.shape
    sub = tm // ROW_BLK
    kern = _mm_res_kernel if len(a_list) == 1 else _mm2_res_kernel
    in_specs = [pl.BlockSpec((tm, a.shape[1]), lambda j, i: (i, 0)) for a in a_list]
    in_specs += [pl.BlockSpec((b.shape[0], tn), lambda j, i: (0, j)) for b in b_list]
    in_specs += [
        pl.BlockSpec((tm, tn), lambda j, i: (i, j)),
        pl.BlockSpec((sub, 1, tn), lambda j, i: (i, 0, j)),
    ]
    return pl.pallas_call(
        functools.partial(kern, tm=tm),
        grid=(n // tn, m // tm),
        in_specs=in_specs,
        out_specs=pl.BlockSpec((tm, tn), lambda j, i: (i, j)),
        out_shape=jax.ShapeDtypeStruct((m, n), F32),
        compiler_params=_cparams("parallel", "parallel"),
        name="matmul_residual",
    )(*a_list, *b_list, x, gate)


def _ffn_up_kernel(h_ref, wg_ref, wu_ref, o_ref):
    h = h_ref[...]
    g = jnp.dot(h, wg_ref[...], preferred_element_type=F32)
    u = jnp.dot(h, wu_ref[...], preferred_element_type=F32)
    o_ref[...] = (_silu(g) * u).astype(o_ref.dtype)


def _ffn_up(h, wg, wu, tm=512, tn=512):
    m, k = h.shape
    n = wg.shape[1]
    return pl.pallas_call(
        _ffn_up_kernel,
        grid=(n // tn, m // tm),
        in_specs=[
            pl.BlockSpec((tm, k), lambda j, i: (i, 0)),
            pl.BlockSpec((k, tn), lambda j, i: (0, j)),
            pl.BlockSpec((k, tn), lambda j, i: (0, j)),
        ],
        out_specs=pl.BlockSpec((tm, tn), lambda j, i: (i, j)),
        out_shape=jax.ShapeDtypeStruct((m, n), BF16),
        compiler_params=_cparams("parallel", "parallel"),
        name="ffn_up",
    )(h, wg, wu)


def _shift_rows(x, shift, valid):
    t = x.shape[0]
    return jnp.where(valid, pltpu.roll(x, shift % t, 0), 0.0)


def _token_shift_kernel(p_ref, mu_ref, o_ref, *, grid):
    x = p_ref[...]
    t, c = x.shape
    row = lax.broadcasted_iota(jnp.int32, (t, c), 0)
    lane = lax.broadcasted_iota(jnp.int32, (t, c), 1)
    if grid:
        col = row % GRID_W
        sel = lane % 4
        s = jnp.where(sel == 0, _shift_rows(x, 1, col != 0),
            jnp.where(sel == 1, _shift_rows(x, -1, col != GRID_W - 1),
            jnp.where(sel == 2, _shift_rows(x, GRID_W, row >= GRID_W),
                      _shift_rows(x, -GRID_W, row < t - GRID_W))))
    else:
        s = jnp.where(lane % 2 == 0, _shift_rows(x, 1, row != 0), _shift_rows(x, -1, row != t - 1))
    o_ref[...] = x + (s - x) * mu_ref[...]


def _token_shift(pa, mu, n_seq, t, row_blk0, grid):
    tc = 512
    return pl.pallas_call(
        functools.partial(_token_shift_kernel, grid=grid),
        grid=(n_seq, C_PA_PAD // tc),
        in_specs=[
            pl.BlockSpec((t, tc), lambda b, j: (row_blk0 + b, j)),
            pl.BlockSpec((1, tc), lambda b, j: (0, j)),
        ],
        out_specs=pl.BlockSpec((t, tc), lambda b, j: (b, j)),
        out_shape=jax.ShapeDtypeStruct((n_seq * t, C_PA_PAD), F32),
        compiler_params=_cparams("parallel", "parallel"),
        name="token_shift",
    )(pa, mu)


def _rwkv_lora_kernel(lo_ref, w2_ref, a2_ref, g2_ref, w0_ref, a0_ref, lw_ref, ic_ref, gt_ref):
    lo = lo_ref[...]
    w_in = jnp.tanh(lo[:, 0:2 * LORA])
    a_in = lo[:, 2 * LORA:4 * LORA]
    g_in = _sigmoid(lo[:, 4 * LORA:])
    for d in range(2):
        w_log = -_softplus(-(w0_ref[d] + _bdot(w_in, w2_ref[d]))) - 0.5
        lw_ref[d] = -jnp.exp(w_log)
        ic_ref[d] = _sigmoid(a0_ref[d] + _bdot(a_in, a2_ref[d]))
    gt_ref[...] = _bdot(g_in, g2_ref[...])


def _rwkv_lora(pam, w2bd, a2bd, g2pad, w0, a0, tm=512):
    m = pam.shape[0]
    lo_blk = 3 * C_A // C_LO
    return pl.pallas_call(
        _rwkv_lora_kernel,
        grid=(m // tm,),
        in_specs=[
            pl.BlockSpec((tm, C_LO), lambda i: (i, lo_blk)),
            pl.BlockSpec((2, 2 * LORA, C_A), lambda i: (0, 0, 0)),
            pl.BlockSpec((2, 2 * LORA, C_A), lambda i: (0, 0, 0)),
            pl.BlockSpec((C_LO - 4 * LORA, C_A), lambda i: (0, 0)),
            pl.BlockSpec((2, 1, C_A), lambda i: (0, 0, 0)),
            pl.BlockSpec((2, 1, C_A), lambda i: (0, 0, 0)),
        ],
        out_specs=[
            pl.BlockSpec((2, tm, C_A), lambda i: (0, i, 0)),
            pl.BlockSpec((2, tm, C_A), lambda i: (0, i, 0)),
            pl.BlockSpec((tm, C_A), lambda i: (i, 0)),
        ],
        out_shape=[
            jax.ShapeDtypeStruct((2, m, C_A), F32),
            jax.ShapeDtypeStruct((2, m, C_A), F32),
            jax.ShapeDtypeStruct((m, C_A), F32),
        ],
        compiler_params=_cparams("parallel"),
        name="rwkv_lora",
    )(pam, w2bd, a2bd, g2pad, w0, a0)


def _order_masks(n_rows, n_cols, reverse):
    rr = lax.broadcasted_iota(jnp.int32, (n_rows, n_cols), 0) % CHUNK
    cc = lax.broadcasted_iota(jnp.int32, (n_rows, n_cols), 1) % CHUNK
    if reverse:
        return cc >= rr, cc > rr
    return cc <= rr, cc < rr


def _triangular_inverse(l, order):
    n = l.shape[0]
    ri = lax.broadcasted_iota(jnp.int32, (n, n), 0)
    ci = lax.broadcasted_iota(jnp.int32, (n, n), 1)

    def joins(k):
        return (ri // (2 * k) == ci // (2 * k)) & ((ri // k) != (ci // k))

    x = (ri == ci).astype(F32) + jnp.where(joins(1), l, 0.0)
    k = 2
    while k < order:
        x = x + _bdot(_bdot(x, jnp.where(joins(k), l, 0.0)), x)
        k *= 2
    return x


def _split_heads(x, first):
    return jnp.concatenate([jnp.where(first, x, 0.0), jnp.where(first, 0.0, x)], 0)


def _rwkv_chunk(s_bd, r, lw, k, v, a, b, reverse):
    c = CHUNK
    incl, strict = _order_masks(c, c, reverse)
    cum = _fdot(incl.astype(F32), lw)
    cum_ex = cum - lw
    tot = cum[0:1] if reverse else cum[c - 1:c]
    e_neg = jnp.exp(-cum)
    e_hat = jnp.exp(tot - cum)
    at = a * jnp.exp(cum_ex)
    rt = r * jnp.exp(cum)
    first = lax.broadcasted_iota(jnp.int32, (c, LANES), 1) < HEAD_A
    lhs = jnp.concatenate([at, rt], 0)
    rbk = jnp.concatenate([_split_heads(b * e_neg, first), _split_heads(k * e_neg, first)], 0)
    g = _bdot(lhs, rbk, NT)
    ri_g = lax.broadcasted_iota(jnp.int32, (2 * c, 4 * c), 0)
    pos_c = lax.broadcasted_iota(jnp.int32, (2 * c, 4 * c), 1) % c
    pos_r = ri_g % c
    own = jnp.where(ri_g < c, 0, 1)
    keep = (pos_c > pos_r - own) if reverse else (pos_c < pos_r + own)
    g = jnp.where(keep, g, 0.0)
    l_ab = g[0:c, 0:2 * c]
    l_ak = g[0:c, 2 * c:4 * c]
    m_r = g[c:2 * c, :]
    v_bd = _split_heads(v, first)
    sa = _bdot(lhs, s_bd, NT)
    rhs = sa[0:c] + _bdot(l_ak, v_bd)
    l_bd = _split_heads(l_ab, first)
    t_inv = _triangular_inverse(l_bd, c)
    u_bd = _bdot(t_inv, _split_heads(rhs, first))
    y = sa[c:2 * c] + _bdot(m_r, jnp.concatenate([u_bd, v_bd], 0))
    u = u_bd[0:c] + u_bd[c:2 * c]
    upd = _bdot(jnp.concatenate([u, v], 0), jnp.concatenate([b * e_hat, k * e_hat], 0), TN)
    ri = lax.broadcasted_iota(jnp.int32, (LANES, LANES), 0) < HEAD_A
    ci = lax.broadcasted_iota(jnp.int32, (LANES, LANES), 1) < HEAD_A
    s_new = s_bd * jnp.exp(tot) + jnp.where(ri == ci, upd, 0.0)
    return y, s_new


def _head_sum(x, first):
    s0 = jnp.sum(jnp.where(first, x, 0.0), -1, keepdims=True)
    s1 = jnp.sum(jnp.where(first, 0.0, x), -1, keepdims=True)
    return jnp.where(first, s0, s1)


def _rwkv_scan_kernel(r_ref, k_ref, v_ref, lw_ref, ic_ref, gt_ref, kk_ref, ka_ref, rk_ref, lnw_ref, lnb_ref,
                      s0_ref, o_ref, sf_ref, acc_ref, *, t):
    n = t // CHUNK
    first = lax.broadcasted_iota(jnp.int32, (CHUNK, LANES), 1) < HEAD_A
    for d in range(2):
        reverse = d == 1

        def body(i, s_bd):
            ci = (n - 1 - i) if reverse else i
            rows = pl.ds(pl.multiple_of(ci * CHUNK, CHUNK), CHUNK)
            r = r_ref[rows, :]
            k = k_ref[rows, :]
            v = v_ref[rows, :]
            lw = lw_ref[d, rows, :]
            iclr = ic_ref[d, rows, :]
            kk = k * kk_ref[...]
            kk = kk * lax.rsqrt(_head_sum(kk * kk, first) + 1e-12)
            k_d = k * (1.0 + (iclr - 1.0) * ka_ref[...])
            y, s_new = _rwkv_chunk(s_bd, r, lw, k_d, v, -kk, kk * iclr, reverse)
            mu = _head_sum(y, first) * (1.0 / HEAD_A)
            yc = y - mu
            var = _head_sum(yc * yc, first) * (1.0 / HEAD_A)
            out = yc * lax.rsqrt(var + GN_EPS) * lnw_ref[...] + lnb_ref[...]
            out = out + _head_sum(r * k_d * rk_ref[...], first) * v
            if d == 0:
                acc_ref[rows, :] = out
            else:
                o_ref[rows, :] = ((acc_ref[rows, :] + out) * gt_ref[rows, :]).astype(o_ref.dtype)
            return s_new

        sf_ref[d] = lax.fori_loop(0, n, body, s0_ref[d])


def _rwkv_scan(pam, lw, ic, gate, k_k, k_a, r_k, ln_w, ln_b, s0_bd, n_seq, t, row_blk0):
    hp = C_A // LANES
    vec = lambda: pl.BlockSpec((1, LANES), lambda b, h: (0, h))
    rows = lambda off: pl.BlockSpec((t, LANES), lambda b, h: (row_blk0 + b, off + h))
    rows2 = lambda: pl.BlockSpec((2, t, LANES), lambda b, h: (0, row_blk0 + b, h))
    return pl.pallas_call(
        functools.partial(_rwkv_scan_kernel, t=t),
        grid=(n_seq, hp),
        in_specs=[
            rows(0), rows(hp), rows(2 * hp), rows2(), rows2(), rows(0),
            vec(), vec(), vec(), vec(), vec(),
            pl.BlockSpec((None, 2, None, LANES, LANES), lambda b, h: (b, 0, h, 0, 0)),
        ],
        out_specs=[
            pl.BlockSpec((t, LANES), lambda b, h: (b, h)),
            pl.BlockSpec((None, 2, None, LANES, LANES), lambda b, h: (b, 0, h, 0, 0)),
        ],
        out_shape=[
            jax.ShapeDtypeStruct((n_seq * t, C_A), BF16),
            jax.ShapeDtypeStruct((n_seq, 2, hp, LANES, LANES), F32),
        ],
        scratch_shapes=[pltpu.VMEM((t, LANES), F32)],
        compiler_params=_cparams("parallel", "parallel"),
        name="rwkv_scan",
    )(pam, pam, pam, lw, ic, gate, k_k, k_a, r_k, ln_w, ln_b, s0_bd)


def _gdn_chunk(s, q, k, v, g, beta, reverse):
    c = CHUNK
    incl, strict = _order_masks(c, c, reverse)
    gcb = _fdot(incl.astype(F32), jnp.broadcast_to(g, (c, LANES)))
    gc = gcb[:, 0:1]
    pick = (lax.broadcasted_iota(jnp.int32, (8, LANES), 1) == 0).astype(F32)
    gr = _fdot(pick, gcb, NT)[0:1, :]
    decay = jnp.where(incl, jnp.exp(jnp.where(incl, gc - gr, 0.0)), 0.0)
    g_last = gc[0:1] if reverse else gc[c - 1:c]
    kb = k * beta
    l = jnp.where(strict, _bdot(kb, k, NT) * decay, 0.0)
    a_inv = _triangular_inverse(-l, c)
    eg = jnp.exp(gc)
    sol = _bdot(a_inv, jnp.concatenate([v * beta, kb * eg], -1))
    u, w = sol[:, 0:HEAD_B], sol[:, HEAD_B:]
    attn = _bdot(q, k, NT) * decay
    ws = _bdot(jnp.concatenate([w, q * eg], 0), s)
    v_new = u - ws[0:c]
    o = ws[c:2 * c] + _bdot(attn, v_new)
    s_new = s * jnp.exp(g_last) + _bdot(k * jnp.exp(g_last - gc), v_new, TN)
    return o, s_new


def _gdn_scan_kernel(alog_ref, dtb_ref, q_ref, k_ref, v_ref, z_ref, ba_ref, cq_ref, ck_ref, cv_ref, gnw_ref,
                     s0_ref, o_ref, sf_ref, qs_ref, ks_ref, vs_ref, acc_ref, *, t):
    n = t // CHUNK
    h = pl.program_id(1)
    row = lax.broadcasted_iota(jnp.int32, (t, LANES), 0)

    def conv_silu(x_ref, w_ref):
        x = x_ref[...]
        w = w_ref[...]
        y = w[0:1] * _shift_rows(x, 1, row != 0) + w[1:2] * x + w[2:3] * _shift_rows(x, -1, row != t - 1)
        return _silu(y)

    def l2norm(x):
        return x * lax.rsqrt(jnp.sum(x * x, -1, keepdims=True) + 1e-6)

    qs_ref[...] = l2norm(conv_silu(q_ref, cq_ref)) * (HEAD_B ** -0.5)
    ks_ref[...] = l2norm(conv_silu(k_ref, ck_ref))
    vs_ref[...] = conv_silu(v_ref, cv_ref)
    lane = lax.broadcasted_iota(jnp.int32, (CHUNK, LANES), 1)

    for d in range(2):
        reverse = d == 1
        a_neg = -jnp.exp(jnp.zeros((1, 1), F32) + alog_ref[d, h])
        dtb = dtb_ref[d, h]

        def body(i, s):
            ci = (n - 1 - i) if reverse else i
            rows = pl.ds(pl.multiple_of(ci * CHUNK, CHUNK), CHUNK)
            ba = ba_ref[rows, :]
            beta = _sigmoid(jnp.sum(jnp.where(lane == d * H_B + h, ba, 0.0), -1, keepdims=True))
            alpha = jnp.sum(jnp.where(lane == 2 * H_B + d * H_B + h, ba, 0.0), -1, keepdims=True)
            g = a_neg * _softplus(alpha + dtb)
            o, s_new = _gdn_chunk(s, qs_ref[rows, :], ks_ref[rows, :], vs_ref[rows, :], g, beta, reverse)
            if d == 0:
                acc_ref[rows, :] = o
            else:
                o = acc_ref[rows, :] + o
                o = o * lax.rsqrt(jnp.mean(o * o, -1, keepdims=True) + RMS_EPS) * gnw_ref[...]
                o_ref[rows, :] = (o * _silu(z_ref[rows, :])).astype(o_ref.dtype)
            return s_new

        sf_ref[d] = lax.fori_loop(0, n, body, s0_ref[d])


def _gdn_scan(pbm, pbs, conv_w, a_log, dt_bias, gn_w, s0, n_seq, t, row_blk0):
    rows = lambda off: pl.BlockSpec((t, LANES), lambda b, h: (row_blk0 + b, off + h))
    cw = lambda off: pl.BlockSpec((3, LANES), lambda b, h: (0, off + h))
    smem = pl.BlockSpec(memory_space=pltpu.SMEM)
    state = pl.BlockSpec((None, 2, None, HEAD_B, HEAD_B), lambda b, h: (b, 0, h, 0, 0))
    return pl.pallas_call(
        functools.partial(_gdn_scan_kernel, t=t),
        grid=(n_seq, H_B),
        in_specs=[
            smem, smem,
            rows(0), rows(H_B), rows(2 * H_B), rows(3 * H_B),
            pl.BlockSpec((t, LANES), lambda b, h: (row_blk0 + b, 0)),
            cw(0), cw(H_B), cw(2 * H_B),
            pl.BlockSpec((1, LANES), lambda b, h: (0, 0)),
            state,
        ],
        out_specs=[pl.BlockSpec((t, LANES), lambda b, h: (b, h)), state],
        out_shape=[
            jax.ShapeDtypeStruct((n_seq * t, C_B), BF16),
            jax.ShapeDtypeStruct((n_seq, 2, H_B, HEAD_B, HEAD_B), F32),
        ],
        scratch_shapes=[pltpu.VMEM((t, LANES), F32)] * 4,
        compiler_params=_cparams("parallel", "parallel"),
        name="gdn_scan",
    )(a_log, dt_bias, pbm, pbm, pbm, pbm, pbs, conv_w, conv_w, conv_w, gn_w, s0)


def _pool_kernel(h_ref, x_ref, w_ref, sc_ref, g_ref, o_ref, *, t):
    gi = pl.program_id(1)
    h = h_ref[...]
    row = lax.broadcasted_iota(jnp.int32, (t, C_G), 0)

    def back(x, n):
        return _shift_rows(x, n, row >= n)

    def fwd(x, n):
        return _shift_rows(x, -n, row < t - n)

    p2 = back(h, 1) + h
    q2 = h + fwd(h, 1)
    p4 = back(p2, 2) + p2
    q4 = q2 + fwd(q2, 2)
    p8 = back(p4, 4) + p4
    q8 = q4 + fwd(q4, 4)
    p_h = jnp.where(gi == 0, h, jnp.where(gi == 1, p2, jnp.where(gi == 2, p4, p8)))
    q_h = jnp.where(gi == 0, h, jnp.where(gi == 1, q2, jnp.where(gi == 2, q4, q8)))
    s = back(p_h, 1) + q_h
    half = jnp.left_shift(1, gi)
    lo = jnp.clip(row - half, 0, t)
    hi = jnp.clip(row + half, 0, t)
    p = s / (hi - lo).astype(F32) - h
    y = _bdot(p, w_ref[...]) * sc_ref[...]
    o_ref[...] = x_ref[...] + g_ref[...] * y


def _pool_mixer(h, x, w_pool, scale, gate_seq, n_seq, t, row_blk0):
    return pl.pallas_call(
        functools.partial(_pool_kernel, t=t),
        grid=(n_seq, POOL_GROUPS),
        in_specs=[
            pl.BlockSpec((t, C_G), lambda b, g: (row_blk0 + b, g)),
            pl.BlockSpec((t, C_G), lambda b, g: (row_blk0 + b, g)),
            pl.BlockSpec((None, C_G, C_G), lambda b, g: (g, 0, 0)),
            pl.BlockSpec((1, C_G), lambda b, g: (0, g)),
            pl.BlockSpec((None, 1, C_G), lambda b, g: (b, 0, g)),
        ],
        out_specs=pl.BlockSpec((t, C_G), lambda b, g: (b, g)),
        out_shape=jax.ShapeDtypeStruct((n_seq * t, D_MODEL), F32),
        compiler_params=_cparams("parallel", "parallel"),
        name="pool_mixer",
    )(h, x, w_pool, scale, gate_seq)


def _pad_cols(w, n):
    return jnp.pad(w, ((0, 0), (0, n - w.shape[1])))


def _block_diag_pairs(s):
    b, two, h, n, _ = s.shape
    sp = s.reshape(b, two, h // 2, 2, n, n)
    z = jnp.zeros_like(sp[:, :, :, 0])
    top = jnp.concatenate([sp[:, :, :, 0], z], -1)
    bot = jnp.concatenate([z, sp[:, :, :, 1]], -1)
    return jnp.concatenate([top, bot], -2)


def _diag_pairs(s_bd):
    b, two, hp, _, _ = s_bd.shape
    n = HEAD_A
    return jnp.stack([s_bd[..., :n, :n], s_bd[..., n:, n:]], 3).reshape(b, two, 2 * hp, n, n)


def kernel(x_prompt, x_sample, state_rwkv, state_delta, c, c_ctx, mod_w, mod_b, norm_mix_w, norm_ffn_w, norm_final_w, ab_w_in, ab_w_out, rwkv_mu, rwkv_w0, rwkv_w2, rwkv_a0, rwkv_a2, rwkv_g2, rwkv_k_k, rwkv_k_a, rwkv_r_k, rwkv_ln_w, rwkv_ln_b, gdn_conv_w, gdn_a_log, gdn_dt_bias, gdn_norm_w, pool_w, pool_scale, ffn_w_gate, ffn_w_up, ffn_w_down):
    bp, tp, d = x_prompt.shape
    bs, ts, _ = x_sample.shape
    mp, ms = bp * tp, bs * ts
    m = mp + ms
    assert tp == ROW_BLK and ts % ROW_BLK == 0 and mp % ts == 0
    x = jnp.concatenate([x_prompt.reshape(mp, d), x_sample.reshape(ms, d)], 0)

    n_cond = 16
    cond = jnp.zeros((n_cond, d), F32).at[0].set(c_ctx).at[1:1 + bs].set(c)
    mod = _mod_all(cond, mod_w, mod_b).reshape(DEPTH, n_cond, 6, d)
    mod_blk = jnp.concatenate([
        jnp.broadcast_to(mod[:, 0:1], (DEPTH, mp // ROW_BLK, 6, d)),
        jnp.repeat(mod[:, 1:1 + bs], ts // ROW_BLK, axis=1)], 1)[:, :, :, None, :]
    zeros_blk = jnp.zeros((m // ROW_BLK, 1, d), F32)

    zero_r = jnp.zeros((bp, 2, H_A // 2, LANES, LANES), F32)
    zero_d = jnp.zeros((bp, 2, H_B, HEAD_B, HEAD_B), F32)
    new_r, new_d = [], []

    for l in range(DEPTH):
        shift_m, scale_m, gate_m, shift_f, scale_f, gate_f = (mod_blk[l, :, j] for j in range(6))
        i = l // 2
        if l % 2 == 0:
            h = _norm_mod(x, norm_mix_w[l], scale_m, shift_m, BF16)
            w_in = ab_w_in[i]
            w_pa = _pad_cols(w_in[:, :C_PA], C_PA_PAD).astype(BF16)
            w_pbm = w_in[:, C_PA:C_PA + 4 * C_B].astype(BF16)
            w_pbs = _pad_cols(w_in[:, C_PA + 4 * C_B:], LANES).astype(BF16)
            pa = _matmul(h, w_pa, F32)
            pbm = _matmul(h, w_pbm, F32)
            pbs = _matmul(h, w_pbs, F32)
            mu = _pad_cols(rwkv_mu[i][None, :], C_PA_PAD)
            pam = jnp.concatenate([
                _token_shift(pa, mu, bp, tp, 0, False),
                _token_shift(pa, mu, bs, ts, mp // ts, True)], 0)
            w2bd = jnp.zeros((2, 2 * LORA, C_A), F32).at[0, :LORA].set(rwkv_w2[i, 0]).at[1, LORA:].set(rwkv_w2[i, 1])
            a2bd = jnp.zeros((2, 2 * LORA, C_A), F32).at[0, :LORA].set(rwkv_a2[i, 0]).at[1, LORA:].set(rwkv_a2[i, 1])
            g2pad = jnp.pad(rwkv_g2[i], ((0, C_LO - 4 * LORA - GATE_LORA), (0, 0)))
            lw, ic, gate = _rwkv_lora(pam, w2bd.astype(BF16), a2bd.astype(BF16), g2pad.astype(BF16),
                                      rwkv_w0[i][:, None, :], rwkv_a0[i][:, None, :])
            vecs = [v_[i].reshape(1, C_A) for v_ in (rwkv_k_k, rwkv_k_a, rwkv_r_k, rwkv_ln_w, rwkv_ln_b)]
            oa_p, sr_p = _rwkv_scan(pam, lw, ic, gate, *vecs, zero_r, bp, tp, 0)
            oa_s, _ = _rwkv_scan(pam, lw, ic, gate, *vecs, _block_diag_pairs(state_rwkv[:, i]), bs, ts, mp // ts)
            gnw = gdn_norm_w[i].reshape(1, HEAD_B)
            ob_p, sd_p = _gdn_scan(pbm, pbs, gdn_conv_w[i], gdn_a_log[i], gdn_dt_bias[i], gnw, zero_d, bp, tp, 0)
            ob_s, _ = _gdn_scan(pbm, pbs, gdn_conv_w[i], gdn_a_log[i], gdn_dt_bias[i], gnw, state_delta[:, i],
                                bs, ts, mp // ts)
            new_r.append(_diag_pairs(sr_p))
            new_d.append(sd_p)
            o_a = jnp.concatenate([oa_p, oa_s], 0)
            o_b = jnp.concatenate([ob_p, ob_s], 0)
            w_out = ab_w_out[i].astype(BF16)
            x = _matmul_residual([o_a, o_b], [w_out[:C_A], w_out[C_A:]], x, gate_m)
        else:
            h = _norm_mod(x, norm_mix_w[l], scale_m, shift_m, F32)
            gate_p = jnp.broadcast_to(mod[l, 0:1, 2][:, None, :], (bp, 1, d))
            gate_s = mod[l, 1:1 + bs, 2][:, None, :]
            wp = pool_w[i].astype(BF16)
            sc = pool_scale[i].reshape(1, d)
            x = jnp.concatenate([
                _pool_mixer(h, x, wp, sc, gate_p, bp, tp, 0),
                _pool_mixer(h, x, wp, sc, gate_s, bs, ts, mp // ts)], 0)
        h = _norm_mod(x, norm_ffn_w[l], scale_f, shift_f, BF16)
        act = _ffn_up(h, ffn_w_gate[l].astype(BF16), ffn_w_up[l].astype(BF16))
        x = _matmul_residual([act], [ffn_w_down[l].astype(BF16)], x, gate_f)

    y = _norm_mod(x, norm_final_w, zeros_blk, zeros_blk, F32)
    y_prompt = y[:mp].reshape(bp, tp, d)
    y_sample = y[mp:].reshape(bs, ts, d)
    return (y_prompt, y_sample, jnp.stack(new_r, 1), jnp.stack(new_d, 1))
```

```python
import functools

import jax
import jax.numpy as jnp
from jax import lax
from jax.experimental import pallas as pl
from jax.experimental.pallas import tpu as pltpu

F32 = jnp.float32
BF16 = jnp.bfloat16

D_MODEL = 2048
DEPTH = 4
GRID_W = 64
N_AB = 2
C_A = 1024
HEAD_A = 64
H_A = 16
LORA = 64
GATE_LORA = 160
C_B = 1024
HEAD_B = 128
H_B = 8
C_PA = 3 * C_A + 4 * LORA + GATE_LORA
C_PA_PAD = 3584
C_LO = C_PA_PAD - 3 * C_A
POOL_GROUPS = 4
C_G = D_MODEL // POOL_GROUPS
D_FF = 5632
RMS_EPS = 1e-6
GN_EPS = 64e-5
CHUNK = 64
ROW_BLK = 256
LANES = 128
VMEM_LIMIT = 48 * 1024 * 1024

NT = (((1,), (1,)), ((), ()))
TN = (((0,), (0,)), ((), ()))
NN = (((1,), (0,)), ((), ()))


def _cparams(*sem):
    return pltpu.CompilerParams(dimension_semantics=sem, vmem_limit_bytes=VMEM_LIMIT)


def _bdot(a, b, dims=NN):
    return lax.dot_general(a.astype(BF16), b.astype(BF16), dims, preferred_element_type=F32)


def _fdot(a, b, dims=NN):
    return lax.dot_general(a, b, dims, precision=lax.Precision.HIGHEST, preferred_element_type=F32)


def _sigmoid(x):
    return 1.0 / (1.0 + jnp.exp(-x))


def _silu(x):
    return x * _sigmoid(x)


def _softplus(x):
    return jnp.maximum(x, 0.0) + jnp.log(1.0 + jnp.exp(-jnp.abs(x)))


def _mod_kernel(c_ref, w_ref, b_ref, o_ref):
    act = _silu(c_ref[...])
    o_ref[...] = _fdot(act, w_ref[...]) + b_ref[...]


def _mod_all(cond, mod_w, mod_b):
    n_c, d = cond.shape
    depth, _, n = mod_w.shape
    tn = 1024
    return pl.pallas_call(
        _mod_kernel,
        grid=(depth, n // tn),
        in_specs=[
            pl.BlockSpec((n_c, d), lambda l, j: (0, 0)),
            pl.BlockSpec((None, d, tn), lambda l, j: (l, 0, j)),
            pl.BlockSpec((None, 1, tn), lambda l, j: (l, 0, j)),
        ],
        out_specs=pl.BlockSpec((None, n_c, tn), lambda l, j: (l, 0, j)),
        out_shape=jax.ShapeDtypeStruct((depth, n_c, n), F32),
        compiler_params=_cparams("parallel", "parallel"),
        name="mod_all",
    )(cond, mod_w, mod_b.reshape(depth, 1, n))


def _norm_mod_kernel(x_ref, nw_ref, sc_ref, sh_ref, o_ref):
    x = x_ref[...]
    ms = jnp.mean(x * x, -1, keepdims=True)
    y = x * lax.rsqrt(ms + RMS_EPS) * nw_ref[...]
    o_ref[...] = (y * (1.0 + sc_ref[...]) + sh_ref[...]).astype(o_ref.dtype)


def _norm_mod(x, nw, scale, shift, out_dtype):
    m, d = x.shape
    nb = m // ROW_BLK
    return pl.pallas_call(
        _norm_mod_kernel,
        grid=(nb,),
        in_specs=[
            pl.BlockSpec((ROW_BLK, d), lambda i: (i, 0)),
            pl.BlockSpec((1, d), lambda i: (0, 0)),
            pl.BlockSpec((None, 1, d), lambda i: (i, 0, 0)),
            pl.BlockSpec((None, 1, d), lambda i: (i, 0, 0)),
        ],
        out_specs=pl.BlockSpec((ROW_BLK, d), lambda i: (i, 0)),
        out_shape=jax.ShapeDtypeStruct((m, d), out_dtype),
        compiler_params=_cparams("parallel"),
        name="norm_mod",
    )(x, nw.reshape(1, d), scale, shift)


def _mm_kernel(a_ref, b_ref, o_ref):
    o_ref[...] = jnp.dot(a_ref[...], b_ref[...], preferred_element_type=F32).astype(o_ref.dtype)


def _matmul(a, b, out_dtype, tm=512, tn=512):
    m, k = a.shape
    _, n = b.shape
    tn = min(tn, n)
    return pl.pallas_call(
        _mm_kernel,
        grid=(n // tn, m // tm),
        in_specs=[
            pl.BlockSpec((tm, k), lambda j, i: (i, 0)),
            pl.BlockSpec((k, tn), lambda j, i: (0, j)),
        ],
        out_specs=pl.BlockSpec((tm, tn), lambda j, i: (i, j)),
        out_shape=jax.ShapeDtypeStruct((m, n), out_dtype),
        compiler_params=_cparams("parallel", "parallel"),
        name="matmul",
    )(a, b)


def _gated_residual(x_ref, g_ref, acc, o_ref, tm):
    for s in range(tm // ROW_BLK):
        rows = slice(s * ROW_BLK, (s + 1) * ROW_BLK)
        o_ref[rows, :] = x_ref[rows, :] + g_ref[s] * acc[rows, :]


def _mm_res_kernel(a_ref, b_ref, x_ref, g_ref, o_ref, *, tm):
    acc = jnp.dot(a_ref[...], b_ref[...], preferred_element_type=F32)
    _gated_residual(x_ref, g_ref, acc, o_ref, tm)


def _mm2_res_kernel(a1_ref, a2_ref, b1_ref, b2_ref, x_ref, g_ref, o_ref, *, tm):
    acc = jnp.dot(a1_ref[...], b1_ref[...], preferred_element_type=F32)
    acc = acc + jnp.dot(a2_ref[...], b2_ref[...], preferred_element_type=F32)
    _gated_residual(x_ref, g_ref, acc, o_ref, tm)


def _matmul_residual(a_list, b_list, x, gate, tm=512, tn=512):
    m, n = x.shape
    sub = tm // ROW_BLK
    kern = _mm_res_kernel if len(a_list) == 1 else _mm2_res_kernel
    in_specs = [pl.BlockSpec((tm, a.shape[1]), lambda j, i: (i, 0)) for a in a_list]
    in_specs += [pl.BlockSpec((b.shape[0], tn), lambda j, i: (0, j)) for b in b_list]
    in_specs += [
        pl.BlockSpec((tm, tn), lambda j, i: (i, j)),
        pl.BlockSpec((sub, 1, tn), lambda j, i: (i, 0, j)),
    ]
    return pl.pallas_call(
        functools.partial(kern, tm=tm),
        grid=(n // tn, m // tm),
        in_specs=in_specs,
        out_specs=pl.BlockSpec((tm, tn), lambda j, i: (i, j)),
        out_shape=jax.ShapeDtypeStruct((m, n), F32),
        compiler_params=_cparams("parallel", "parallel"),
        name="matmul_residual",
    )(*a_list, *b_list, x, gate)


def _ffn_up_kernel(h_ref, wg_ref, wu_ref, o_ref):
    h = h_ref[...]
    g = jnp.dot(h, wg_ref[...], preferred_element_type=F32)
    u = jnp.dot(h, wu_ref[...], preferred_element_type=F32)
    o_ref[...] = (_silu(g) * u).astype(o_ref.dtype)


def _ffn_up(h, wg, wu, tm=512, tn=512):
    m, k = h.shape
    n = wg.shape[1]
    return pl.pallas_call(
        _ffn_up_kernel,
        grid=(n // tn, m // tm),
        in_specs=[
            pl.BlockSpec((tm, k), lambda j, i: (i, 0)),
            pl.BlockSpec((k, tn), lambda j, i: (0, j)),
            pl.BlockSpec((k, tn), lambda j, i: (0, j)),
        ],
        out_specs=pl.BlockSpec((tm, tn), lambda j, i: (i, j)),
        out_shape=jax.ShapeDtypeStruct((m, n), BF16),
        compiler_params=_cparams("parallel", "parallel"),
        name="ffn_up",
    )(h, wg, wu)


def _shift_rows(x, shift, valid):
    t = x.shape[0]
    return jnp.where(valid, pltpu.roll(x, shift % t, 0), 0.0)


def _token_shift_kernel(p_ref, mu_ref, o_ref, *, grid):
    x = p_ref[...]
    t, c = x.shape
    row = lax.broadcasted_iota(jnp.int32, (t, c), 0)
    lane = lax.broadcasted_iota(jnp.int32, (t, c), 1)
    if grid:
        col = row % GRID_W
        sel = lane % 4
        s = jnp.where(sel == 0, _shift_rows(x, 1, col != 0),
            jnp.where(sel == 1, _shift_rows(x, -1, col != GRID_W - 1),
            jnp.where(sel == 2, _shift_rows(x, GRID_W, row >= GRID_W),
                      _shift_rows(x, -GRID_W, row < t - GRID_W))))
    else:
        s = jnp.where(lane % 2 == 0, _shift_rows(x, 1, row != 0), _shift_rows(x, -1, row != t - 1))
    o_ref[...] = x + (s - x) * mu_ref[...]


def _token_shift(pa, mu, n_seq, t, row_blk0, grid):
    tc = 512
    return pl.pallas_call(
        functools.partial(_token_shift_kernel, grid=grid),
        grid=(n_seq, C_PA_PAD // tc),
        in_specs=[
            pl.BlockSpec((t, tc), lambda b, j: (row_blk0 + b, j)),
            pl.BlockSpec((1, tc), lambda b, j: (0, j)),
        ],
        out_specs=pl.BlockSpec((t, tc), lambda b, j: (b, j)),
        out_shape=jax.ShapeDtypeStruct((n_seq * t, C_PA_PAD), F32),
        compiler_params=_cparams("parallel", "parallel"),
        name="token_shift",
    )(pa, mu)


def _rwkv_lora_kernel(lo_ref, w2_ref, a2_ref, g2_ref, w0_ref, a0_ref, lw_ref, ic_ref, gt_ref):
    lo = lo_ref[...]
    w_in = jnp.tanh(lo[:, 0:2 * LORA])
    a_in = lo[:, 2 * LORA:4 * LORA]
    g_in = _sigmoid(lo[:, 4 * LORA:])
    for d in range(2):
        w_log = -_softplus(-(w0_ref[d] + _bdot(w_in, w2_ref[d]))) - 0.5
        lw_ref[d] = -jnp.exp(w_log)
        ic_ref[d] = _sigmoid(a0_ref[d] + _bdot(a_in, a2_ref[d]))
    gt_ref[...] = _bdot(g_in, g2_ref[...])


def _rwkv_lora(pam, w2bd, a2bd, g2pad, w0, a0, tm=512):
    m = pam.shape[0]
    lo_blk = 3 * C_A // C_LO
    return pl.pallas_call(
        _rwkv_lora_kernel,
        grid=(m // tm,),
        in_specs=[
            pl.BlockSpec((tm, C_LO), lambda i: (i, lo_blk)),
            pl.BlockSpec((2, 2 * LORA, C_A), lambda i: (0, 0, 0)),
            pl.BlockSpec((2, 2 * LORA, C_A), lambda i: (0, 0, 0)),
            pl.BlockSpec((C_LO - 4 * LORA, C_A), lambda i: (0, 0)),
            pl.BlockSpec((2, 1, C_A), lambda i: (0, 0, 0)),
            pl.BlockSpec((2, 1, C_A), lambda i: (0, 0, 0)),
        ],
        out_specs=[
            pl.BlockSpec((2, tm, C_A), lambda i: (0, i, 0)),
            pl.BlockSpec((2, tm, C_A), lambda i: (0, i, 0)),
            pl.BlockSpec((tm, C_A), lambda i: (i, 0)),
        ],
        out_shape=[
            jax.ShapeDtypeStruct((2, m, C_A), F32),
            jax.ShapeDtypeStruct((2, m, C_A), F32),
            jax.ShapeDtypeStruct((m, C_A), F32),
        ],
        compiler_params=_cparams("parallel"),
        name="rwkv_lora",
    )(pam, w2bd, a2bd, g2pad, w0, a0)


def _order_masks(n_rows, n_cols, reverse):
    rr = lax.broadcasted_iota(jnp.int32, (n_rows, n_cols), 0) % CHUNK
    cc = lax.broadcasted_iota(jnp.int32, (n_rows, n_cols), 1) % CHUNK
    if reverse:
        return cc >= rr, cc > rr
    return cc <= rr, cc < rr


def _run_lockstep(chains):
    results = [None] * len(chains)
    pending = {i: next(c) for i, c in enumerate(chains)}
    while pending:
        products = {i: fn(a, b, dims) for i, (fn, a, b, dims) in pending.items()}
        pending = {}
        for i, prod in products.items():
            try:
                pending[i] = chains[i].send(prod)
            except StopIteration as stop:
                results[i] = stop.value
    return results


def _triangular_inverse(l, order):
    n = l.shape[0]
    ri = lax.broadcasted_iota(jnp.int32, (n, n), 0)
    ci = lax.broadcasted_iota(jnp.int32, (n, n), 1)

    def joins(k):
        return (ri // (2 * k) == ci // (2 * k)) & ((ri // k) != (ci // k))

    x = (ri == ci).astype(F32) + jnp.where(joins(1), l, 0.0)
    k = 2
    while k < order:
        xe = yield _bdot, x, jnp.where(joins(k), l, 0.0), NN
        x = x + (yield _bdot, xe, x, NN)
        k *= 2
    return x


def _split_heads(x, first):
    return jnp.concatenate([jnp.where(first, x, 0.0), jnp.where(first, 0.0, x)], 0)


def _chunk_cumsum(x, reverse):
    t = x.shape[0]
    pos = lax.broadcasted_iota(jnp.int32, x.shape, 0) % CHUNK
    s = 1
    while s < CHUNK:
        if reverse:
            x = x + jnp.where(pos < CHUNK - s, pltpu.roll(x, t - s, 0), 0.0)
        else:
            x = x + jnp.where(pos >= s, pltpu.roll(x, s, 0), 0.0)
        s *= 2
    return x


def _rwkv_chunk(s_bd, r, lw, cum, k, v, a, b, reverse):
    c = CHUNK
    cum_ex = cum - lw
    tot = cum[0:1] if reverse else cum[c - 1:c]
    e_neg = jnp.exp(-cum)
    e_hat = jnp.exp(tot - cum)
    at = a * jnp.exp(cum_ex)
    rt = r * jnp.exp(cum)
    first = lax.broadcasted_iota(jnp.int32, (c, LANES), 1) < HEAD_A
    lhs = jnp.concatenate([at, rt], 0)
    rbk = jnp.concatenate([_split_heads(b * e_neg, first), _split_heads(k * e_neg, first)], 0)
    g = yield _bdot, lhs, rbk, NT
    ri_g = lax.broadcasted_iota(jnp.int32, (2 * c, 4 * c), 0)
    pos_c = lax.broadcasted_iota(jnp.int32, (2 * c, 4 * c), 1) % c
    pos_r = ri_g % c
    own = jnp.where(ri_g < c, 0, 1)
    keep = (pos_c > pos_r - own) if reverse else (pos_c < pos_r + own)
    g = jnp.where(keep, g, 0.0)
    l_ab = g[0:c, 0:2 * c]
    l_ak = g[0:c, 2 * c:4 * c]
    m_r = g[c:2 * c, :]
    v_bd = _split_heads(v, first)
    sa = yield _bdot, lhs, s_bd, NT
    rhs = sa[0:c] + (yield _bdot, l_ak, v_bd, NN)
    l_bd = _split_heads(l_ab, first)
    t_inv = yield from _triangular_inverse(l_bd, c)
    u_bd = yield _bdot, t_inv, _split_heads(rhs, first), NN
    y = sa[c:2 * c] + (yield _bdot, m_r, jnp.concatenate([u_bd, v_bd], 0), NN)
    u = u_bd[0:c] + u_bd[c:2 * c]
    upd = yield _bdot, jnp.concatenate([u, v], 0), jnp.concatenate([b * e_hat, k * e_hat], 0), TN
    ri = lax.broadcasted_iota(jnp.int32, (LANES, LANES), 0) < HEAD_A
    ci = lax.broadcasted_iota(jnp.int32, (LANES, LANES), 1) < HEAD_A
    s_new = s_bd * jnp.exp(tot) + jnp.where(ri == ci, upd, 0.0)
    return y, s_new


def _head_sum(x, first):
    s0 = jnp.sum(jnp.where(first, x, 0.0), -1, keepdims=True)
    s1 = jnp.sum(jnp.where(first, 0.0, x), -1, keepdims=True)
    return jnp.where(first, s0, s1)


def _rwkv_scan_kernel(r_ref, k_ref, v_ref, lw_ref, ic_ref, gt_ref, kk_ref, ka_ref, rk_ref, lnw_ref, lnb_ref,
                      s0_ref, o_ref, sf_ref, acc_ref, cum_ref, *, t, pairs):
    n = t // CHUNK
    first = lax.broadcasted_iota(jnp.int32, (CHUNK, LANES), 1) < HEAD_A
    for d in range(2):
        cum_ref[d] = _chunk_cumsum(lw_ref[d], d == 1)
    sf_ref[...] = s0_ref[...]

    def body(i, carry):
        where, chains = [], []
        for d in range(2):
            reverse = d == 1
            ci = (n - 1 - i) if reverse else i
            rows = pl.ds(pl.multiple_of(ci * CHUNK, CHUNK), CHUNK)
            for p in range(pairs):
                lanes = slice(p * LANES, (p + 1) * LANES)
                r = r_ref[rows, lanes]
                k = k_ref[rows, lanes]
                v = v_ref[rows, lanes]
                iclr = ic_ref[d, rows, lanes]
                kk = k * kk_ref[:, lanes]
                kk = kk * lax.rsqrt(_head_sum(kk * kk, first) + 1e-12)
                k_d = k * (1.0 + (iclr - 1.0) * ka_ref[:, lanes])
                where.append((d, p, rows, lanes, r, k_d, v))
                chains.append(_rwkv_chunk(sf_ref[d, p], r, lw_ref[d, rows, lanes], cum_ref[d, rows, lanes],
                                          k_d, v, -kk, kk * iclr, reverse))
        for (d, p, rows, lanes, r, k_d, v), (y, s_new) in zip(where, _run_lockstep(chains)):
            sf_ref[d, p] = s_new
            mu = _head_sum(y, first) * (1.0 / HEAD_A)
            yc = y - mu
            var = _head_sum(yc * yc, first) * (1.0 / HEAD_A)
            out = yc * lax.rsqrt(var + GN_EPS) * lnw_ref[:, lanes] + lnb_ref[:, lanes]
            acc_ref[d, rows, lanes] = out + _head_sum(r * k_d * rk_ref[:, lanes], first) * v
        return carry

    lax.fori_loop(0, n, body, 0)
    o_ref[...] = ((acc_ref[0] + acc_ref[1]) * gt_ref[...]).astype(o_ref.dtype)


def _rwkv_scan(pam, lw, ic, gate, k_k, k_a, r_k, ln_w, ln_b, s0_bd, n_seq, t, row_blk0, pairs=4):
    w = pairs * LANES
    nb = C_A // w
    vec = lambda: pl.BlockSpec((1, w), lambda b, h: (0, h))
    rows = lambda off: pl.BlockSpec((t, w), lambda b, h: (row_blk0 + b, off + h))
    rows2 = lambda: pl.BlockSpec((2, t, w), lambda b, h: (0, row_blk0 + b, h))
    state = pl.BlockSpec((None, 2, pairs, LANES, LANES), lambda b, h: (b, 0, h, 0, 0))
    return pl.pallas_call(
        functools.partial(_rwkv_scan_kernel, t=t, pairs=pairs),
        grid=(n_seq, nb),
        in_specs=[
            rows(0), rows(nb), rows(2 * nb), rows2(), rows2(), rows(0),
            vec(), vec(), vec(), vec(), vec(),
            state,
        ],
        out_specs=[pl.BlockSpec((t, w), lambda b, h: (b, h)), state],
        out_shape=[
            jax.ShapeDtypeStruct((n_seq * t, C_A), BF16),
            jax.ShapeDtypeStruct((n_seq, 2, C_A // LANES, LANES, LANES), F32),
        ],
        scratch_shapes=[pltpu.VMEM((2, t, w), F32), pltpu.VMEM((2, t, w), F32)],
        compiler_params=_cparams("parallel", "parallel"),
        name="rwkv_scan",
    )(pam, pam, pam, lw, ic, gate, k_k, k_a, r_k, ln_w, ln_b, s0_bd)


def _gdn_chunk(s, q, k, v, gc, beta, reverse):
    c = CHUNK
    incl, strict = _order_masks(c, c, reverse)
    pick = (lax.broadcasted_iota(jnp.int32, (8, LANES), 1) == 0).astype(F32)
    gr = (yield _fdot, pick, jnp.broadcast_to(gc, (c, LANES)), NT)[0:1, :]
    decay = jnp.where(incl, jnp.exp(jnp.where(incl, gc - gr, 0.0)), 0.0)
    g_last = gc[0:1] if reverse else gc[c - 1:c]
    kb = k * beta
    l = jnp.where(strict, (yield _bdot, kb, k, NT) * decay, 0.0)
    attn = (yield _bdot, q, k, NT) * decay
    a_inv = yield from _triangular_inverse(-l, c)
    eg = jnp.exp(gc)
    sol = yield _bdot, a_inv, jnp.concatenate([v * beta, kb * eg], -1), NN
    u, w = sol[:, 0:HEAD_B], sol[:, HEAD_B:]
    ws = yield _bdot, jnp.concatenate([w, q * eg], 0), s, NN
    v_new = u - ws[0:c]
    o = ws[c:2 * c] + (yield _bdot, attn, v_new, NN)
    s_new = s * jnp.exp(g_last) + (yield _bdot, k * jnp.exp(g_last - gc), v_new, TN)
    return o, s_new


def _gdn_scan_kernel(q_ref, k_ref, v_ref, z_ref, ba_ref, cq_ref, ck_ref, cv_ref, gnw_ref, alog_ref, dtb_ref,
                     s0_ref, o_ref, sf_ref, qs_ref, ks_ref, vs_ref, acc_ref, gc_ref, beta_ref, *, t, heads):
    n = t // CHUNK
    hg = pl.program_id(1)
    row = lax.broadcasted_iota(jnp.int32, (t, LANES), 0)

    def conv_silu(x, w):
        y = w[0:1] * _shift_rows(x, 1, row != 0) + w[1:2] * x + w[2:3] * _shift_rows(x, -1, row != t - 1)
        return _silu(y)

    def l2norm(x):
        return x * lax.rsqrt(jnp.sum(x * x, -1, keepdims=True) + 1e-6)

    for g in range(heads):
        lanes = slice(g * LANES, (g + 1) * LANES)
        qs_ref[:, lanes] = l2norm(conv_silu(q_ref[:, lanes], cq_ref[:, lanes])) * (HEAD_B ** -0.5)
        ks_ref[:, lanes] = l2norm(conv_silu(k_ref[:, lanes], ck_ref[:, lanes]))
        vs_ref[:, lanes] = conv_silu(v_ref[:, lanes], cv_ref[:, lanes])
    ba = ba_ref[...]
    beta_ref[...] = _sigmoid(ba)
    g_log = -jnp.exp(alog_ref[...]) * _softplus(ba + dtb_ref[...])
    gc_ref[0] = _chunk_cumsum(g_log, False)
    gc_ref[1] = _chunk_cumsum(g_log, True)
    sf_ref[...] = s0_ref[...]
    lane = lax.broadcasted_iota(jnp.int32, (CHUNK, LANES), 1)

    def body(i, carry):
        chains = []
        for d in range(2):
            ci = (n - 1 - i) if d == 1 else i
            rows = pl.ds(pl.multiple_of(ci * CHUNK, CHUNK), CHUNK)
            gcs = gc_ref[d, rows, :]
            bts = beta_ref[rows, :]
            for g in range(heads):
                lanes = slice(g * LANES, (g + 1) * LANES)
                chains.append((d, g, rows, lanes, gcs, bts, sf_ref[d, g],
                               qs_ref[rows, lanes], ks_ref[rows, lanes], vs_ref[rows, lanes]))
        gens = []
        for d, g, rows, lanes, gcs, bts, s, q, k, v in chains:
            col = d * H_B + hg * heads + g
            gc = jnp.sum(jnp.where(lane == 2 * H_B + col, gcs, 0.0), -1, keepdims=True)
            beta = jnp.sum(jnp.where(lane == col, bts, 0.0), -1, keepdims=True)
            gens.append(_gdn_chunk(s, q, k, v, gc, beta, d == 1))
        for (d, g, rows, lanes, *_), (o, s_new) in zip(chains, _run_lockstep(gens)):
            sf_ref[d, g] = s_new
            acc_ref[d, rows, lanes] = o
        return carry

    lax.fori_loop(0, n, body, 0)
    for g in range(heads):
        lanes = slice(g * LANES, (g + 1) * LANES)
        o = acc_ref[0, :, lanes] + acc_ref[1, :, lanes]
        o = o * lax.rsqrt(jnp.mean(o * o, -1, keepdims=True) + RMS_EPS) * gnw_ref[...]
        o_ref[:, lanes] = (o * _silu(z_ref[:, lanes])).astype(o_ref.dtype)


def _gdn_scan(pbm, pbs, conv_w, alog_vec, dtb_vec, gn_w, s0, n_seq, t, row_blk0, heads=4):
    w = heads * LANES
    nb = C_B // w
    rows = lambda off: pl.BlockSpec((t, w), lambda b, h: (row_blk0 + b, off + h))
    cw = lambda off: pl.BlockSpec((3, w), lambda b, h: (0, off + h))
    vec = pl.BlockSpec((1, LANES), lambda b, h: (0, 0))
    state = pl.BlockSpec((None, 2, heads, HEAD_B, HEAD_B), lambda b, h: (b, 0, h, 0, 0))
    return pl.pallas_call(
        functools.partial(_gdn_scan_kernel, t=t, heads=heads),
        grid=(n_seq, nb),
        in_specs=[
            rows(0), rows(nb), rows(2 * nb), rows(3 * nb),
            pl.BlockSpec((t, LANES), lambda b, h: (row_blk0 + b, 0)),
            cw(0), cw(nb), cw(2 * nb),
            vec, vec, vec,
            state,
        ],
        out_specs=[pl.BlockSpec((t, w), lambda b, h: (b, h)), state],
        out_shape=[
            jax.ShapeDtypeStruct((n_seq * t, C_B), BF16),
            jax.ShapeDtypeStruct((n_seq, 2, H_B, HEAD_B, HEAD_B), F32),
        ],
        scratch_shapes=[pltpu.VMEM((t, w), F32)] * 3 + [pltpu.VMEM((2, t, w), F32), pltpu.VMEM((2, t, LANES), F32),
                                                         pltpu.VMEM((t, LANES), F32)],
        compiler_params=_cparams("parallel", "parallel"),
        name="gdn_scan",
    )(pbm, pbm, pbm, pbm, pbs, conv_w, conv_w, conv_w, gn_w, alog_vec, dtb_vec, s0)


def _pool_kernel(h_ref, x_ref, w_ref, sc_ref, g_ref, o_ref, *, t):
    gi = pl.program_id(1)
    h = h_ref[...]
    row = lax.broadcasted_iota(jnp.int32, (t, C_G), 0)

    def back(x, n):
        return _shift_rows(x, n, row >= n)

    def fwd(x, n):
        return _shift_rows(x, -n, row < t - n)

    p2 = back(h, 1) + h
    q2 = h + fwd(h, 1)
    p4 = back(p2, 2) + p2
    q4 = q2 + fwd(q2, 2)
    p8 = back(p4, 4) + p4
    q8 = q4 + fwd(q4, 4)
    p_h = jnp.where(gi == 0, h, jnp.where(gi == 1, p2, jnp.where(gi == 2, p4, p8)))
    q_h = jnp.where(gi == 0, h, jnp.where(gi == 1, q2, jnp.where(gi == 2, q4, q8)))
    s = back(p_h, 1) + q_h
    half = jnp.left_shift(1, gi)
    lo = jnp.clip(row - half, 0, t)
    hi = jnp.clip(row + half, 0, t)
    p = s / (hi - lo).astype(F32) - h
    y = _bdot(p, w_ref[...]) * sc_ref[...]
    o_ref[...] = x_ref[...] + g_ref[...] * y


def _pool_mixer(h, x, w_pool, scale, gate_seq, n_seq, t, row_blk0):
    return pl.pallas_call(
        functools.partial(_pool_kernel, t=t),
        grid=(n_seq, POOL_GROUPS),
        in_specs=[
            pl.BlockSpec((t, C_G), lambda b, g: (row_blk0 + b, g)),
            pl.BlockSpec((t, C_G), lambda b, g: (row_blk0 + b, g)),
            pl.BlockSpec((None, C_G, C_G), lambda b, g: (g, 0, 0)),
            pl.BlockSpec((1, C_G), lambda b, g: (0, g)),
            pl.BlockSpec((None, 1, C_G), lambda b, g: (b, 0, g)),
        ],
        out_specs=pl.BlockSpec((t, C_G), lambda b, g: (b, g)),
        out_shape=jax.ShapeDtypeStruct((n_seq * t, D_MODEL), F32),
        compiler_params=_cparams("parallel", "parallel"),
        name="pool_mixer",
    )(h, x, w_pool, scale, gate_seq)


def _pad_cols(w, n):
    return jnp.pad(w, ((0, 0), (0, n - w.shape[1])))


def _block_diag_pairs(s):
    b, two, h, n, _ = s.shape
    sp = s.reshape(b, two, h // 2, 2, n, n)
    z = jnp.zeros_like(sp[:, :, :, 0])
    top = jnp.concatenate([sp[:, :, :, 0], z], -1)
    bot = jnp.concatenate([z, sp[:, :, :, 1]], -1)
    return jnp.concatenate([top, bot], -2)


def _diag_pairs(s_bd):
    b, two, hp, _, _ = s_bd.shape
    n = HEAD_A
    return jnp.stack([s_bd[..., :n, :n], s_bd[..., n:, n:]], 3).reshape(b, two, 2 * hp, n, n)


def kernel(x_prompt, x_sample, state_rwkv, state_delta, c, c_ctx, mod_w, mod_b, norm_mix_w, norm_ffn_w, norm_final_w, ab_w_in, ab_w_out, rwkv_mu, rwkv_w0, rwkv_w2, rwkv_a0, rwkv_a2, rwkv_g2, rwkv_k_k, rwkv_k_a, rwkv_r_k, rwkv_ln_w, rwkv_ln_b, gdn_conv_w, gdn_a_log, gdn_dt_bias, gdn_norm_w, pool_w, pool_scale, ffn_w_gate, ffn_w_up, ffn_w_down):
    bp, tp, d = x_prompt.shape
    bs, ts, _ = x_sample.shape
    mp, ms = bp * tp, bs * ts
    m = mp + ms
    assert tp == ROW_BLK and ts % ROW_BLK == 0 and mp % ts == 0
    x = jnp.concatenate([x_prompt.reshape(mp, d), x_sample.reshape(ms, d)], 0)

    n_cond = 16
    cond = jnp.zeros((n_cond, d), F32).at[0].set(c_ctx).at[1:1 + bs].set(c)
    mod = _mod_all(cond, mod_w, mod_b).reshape(DEPTH, n_cond, 6, d)
    mod_blk = jnp.concatenate([
        jnp.broadcast_to(mod[:, 0:1], (DEPTH, mp // ROW_BLK, 6, d)),
        jnp.repeat(mod[:, 1:1 + bs], ts // ROW_BLK, axis=1)], 1)[:, :, :, None, :]
    zeros_blk = jnp.zeros((m // ROW_BLK, 1, d), F32)

    zero_r = jnp.zeros((bp, 2, H_A // 2, LANES, LANES), F32)
    zero_d = jnp.zeros((bp, 2, H_B, HEAD_B, HEAD_B), F32)
    new_r, new_d = [], []

    for l in range(DEPTH):
        shift_m, scale_m, gate_m, shift_f, scale_f, gate_f = (mod_blk[l, :, j] for j in range(6))
        i = l // 2
        if l % 2 == 0:
            h = _norm_mod(x, norm_mix_w[l], scale_m, shift_m, BF16)
            w_in = ab_w_in[i]
            w_pa = _pad_cols(w_in[:, :C_PA], C_PA_PAD).astype(BF16)
            w_pbm = w_in[:, C_PA:C_PA + 4 * C_B].astype(BF16)
            w_pbs = _pad_cols(w_in[:, C_PA + 4 * C_B:], LANES).astype(BF16)
            pa = _matmul(h, w_pa, F32)
            pbm = _matmul(h, w_pbm, F32)
            pbs = _matmul(h, w_pbs, F32)
            mu = _pad_cols(rwkv_mu[i][None, :], C_PA_PAD)
            pam = jnp.concatenate([
                _token_shift(pa, mu, bp, tp, 0, False),
                _token_shift(pa, mu, bs, ts, mp // ts, True)], 0)
            w2bd = jnp.zeros((2, 2 * LORA, C_A), F32).at[0, :LORA].set(rwkv_w2[i, 0]).at[1, LORA:].set(rwkv_w2[i, 1])
            a2bd = jnp.zeros((2, 2 * LORA, C_A), F32).at[0, :LORA].set(rwkv_a2[i, 0]).at[1, LORA:].set(rwkv_a2[i, 1])
            g2pad = jnp.pad(rwkv_g2[i], ((0, C_LO - 4 * LORA - GATE_LORA), (0, 0)))
            lw, ic, gate = _rwkv_lora(pam, w2bd.astype(BF16), a2bd.astype(BF16), g2pad.astype(BF16),
                                      rwkv_w0[i][:, None, :], rwkv_a0[i][:, None, :])
            vecs = [v_[i].reshape(1, C_A) for v_ in (rwkv_k_k, rwkv_k_a, rwkv_r_k, rwkv_ln_w, rwkv_ln_b)]
            oa_p, sr_p = _rwkv_scan(pam, lw, ic, gate, *vecs, zero_r, bp, tp, 0)
            oa_s, _ = _rwkv_scan(pam, lw, ic, gate, *vecs, _block_diag_pairs(state_rwkv[:, i]), bs, ts, mp // ts)
            gnw = gdn_norm_w[i].reshape(1, HEAD_B)
            on_decay_lanes = lambda p_: jnp.zeros((1, LANES), F32).at[0, 2 * H_B:4 * H_B].set(p_.reshape(2 * H_B))
            alog_vec, dtb_vec = on_decay_lanes(gdn_a_log[i]), on_decay_lanes(gdn_dt_bias[i])
            ob_p, sd_p = _gdn_scan(pbm, pbs, gdn_conv_w[i], alog_vec, dtb_vec, gnw, zero_d, bp, tp, 0)
            ob_s, _ = _gdn_scan(pbm, pbs, gdn_conv_w[i], alog_vec, dtb_vec, gnw, state_delta[:, i],
                                bs, ts, mp // ts)
            new_r.append(_diag_pairs(sr_p))
            new_d.append(sd_p)
            o_a = jnp.concatenate([oa_p, oa_s], 0)
            o_b = jnp.concatenate([ob_p, ob_s], 0)
            w_out = ab_w_out[i].astype(BF16)
            x = _matmul_residual([o_a, o_b], [w_out[:C_A], w_out[C_A:]], x, gate_m)
        else:
            h = _norm_mod(x, norm_mix_w[l], scale_m, shift_m, F32)
            gate_p = jnp.broadcast_to(mod[l, 0:1, 2][:, None, :], (bp, 1, d))
            gate_s = mod[l, 1:1 + bs, 2][:, None, :]
            wp = pool_w[i].astype(BF16)
            sc = pool_scale[i].reshape(1, d)
            x = jnp.concatenate([
                _pool_mixer(h, x, wp, sc, gate_p, bp, tp, 0),
                _pool_mixer(h, x, wp, sc, gate_s, bs, ts, mp // ts)], 0)
        h = _norm_mod(x, norm_ffn_w[l], scale_f, shift_f, BF16)
        act = _ffn_up(h, ffn_w_gate[l].astype(BF16), ffn_w_up[l].astype(BF16))
        x = _matmul_residual([act], [ffn_w_down[l].astype(BF16)], x, gate_f)

    y = _norm_mod(x, norm_final_w, zeros_blk, zeros_blk, F32)
    y_prompt = y[:mp].reshape(bp, tp, d)
    y_sample = y[mp:].reshape(bs, ts, d)
    return (y_prompt, y_sample, jnp.stack(new_r, 1), jnp.stack(new_d, 1))
```

```python
import functools

import jax
import jax.numpy as jnp
from jax import lax
from jax.experimental import pallas as pl
from jax.experimental.pallas import tpu as pltpu

F32 = jnp.float32
BF16 = jnp.bfloat16

D_MODEL = 2048
DEPTH = 4
GRID_W = 64
N_AB = 2
C_A = 1024
HEAD_A = 64
H_A = 16
LORA = 64
GATE_LORA = 160
C_B = 1024
HEAD_B = 128
H_B = 8
C_PA = 3 * C_A + 4 * LORA + GATE_LORA
C_PA_PAD = 3584
C_LO = C_PA_PAD - 3 * C_A
POOL_GROUPS = 4
C_G = D_MODEL // POOL_GROUPS
D_FF = 5632
RMS_EPS = 1e-6
GN_EPS = 64e-5
CHUNK = 64
CHUNK_B = 128
ROW_BLK = 256
LANES = 128
VMEM_LIMIT = 48 * 1024 * 1024

NT = (((1,), (1,)), ((), ()))
TN = (((0,), (0,)), ((), ()))
NN = (((1,), (0,)), ((), ()))


def _cparams(*sem):
    return pltpu.CompilerParams(dimension_semantics=sem, vmem_limit_bytes=VMEM_LIMIT)


def _bdot(a, b, dims=NN):
    return lax.dot_general(a.astype(BF16), b.astype(BF16), dims, preferred_element_type=F32)


def _fdot(a, b, dims=NN):
    return lax.dot_general(a, b, dims, precision=lax.Precision.HIGHEST, preferred_element_type=F32)


def _sigmoid(x):
    return 0.5 + 0.5 * jnp.tanh(0.5 * x)


def _silu(x):
    return x * _sigmoid(x)


def _softplus(x):
    return jnp.maximum(x, 0.0) + jnp.log(1.0 + jnp.exp(-jnp.abs(x)))


def _mod_kernel(c_ref, w_ref, b_ref, o_ref):
    act = _silu(c_ref[...])
    o_ref[...] = _fdot(act, w_ref[...]) + b_ref[...]


def _mod_all(cond, mod_w, mod_b):
    n_c, d = cond.shape
    depth, _, n = mod_w.shape
    tn = 1024
    return pl.pallas_call(
        _mod_kernel,
        grid=(depth, n // tn),
        in_specs=[
            pl.BlockSpec((n_c, d), lambda l, j: (0, 0)),
            pl.BlockSpec((None, d, tn), lambda l, j: (l, 0, j)),
            pl.BlockSpec((None, 1, tn), lambda l, j: (l, 0, j)),
        ],
        out_specs=pl.BlockSpec((None, n_c, tn), lambda l, j: (l, 0, j)),
        out_shape=jax.ShapeDtypeStruct((depth, n_c, n), F32),
        compiler_params=_cparams("parallel", "parallel"),
        name="mod_all",
    )(cond, mod_w, mod_b.reshape(depth, 1, n))


def _norm_mod_kernel(x_ref, nw_ref, sc_ref, sh_ref, o_ref):
    x = x_ref[...]
    ms = jnp.mean(x * x, -1, keepdims=True)
    y = x * lax.rsqrt(ms + RMS_EPS) * nw_ref[...]
    o_ref[...] = (y * (1.0 + sc_ref[...]) + sh_ref[...]).astype(o_ref.dtype)


def _norm_mod(x, nw, scale, shift, out_dtype):
    m, d = x.shape
    nb = m // ROW_BLK
    return pl.pallas_call(
        _norm_mod_kernel,
        grid=(nb,),
        in_specs=[
            pl.BlockSpec((ROW_BLK, d), lambda i: (i, 0)),
            pl.BlockSpec((1, d), lambda i: (0, 0)),
            pl.BlockSpec((None, 1, d), lambda i: (i, 0, 0)),
            pl.BlockSpec((None, 1, d), lambda i: (i, 0, 0)),
        ],
        out_specs=pl.BlockSpec((ROW_BLK, d), lambda i: (i, 0)),
        out_shape=jax.ShapeDtypeStruct((m, d), out_dtype),
        compiler_params=_cparams("parallel"),
        name="norm_mod",
    )(x, nw.reshape(1, d), scale, shift)


def _mm_kernel(a_ref, b_ref, o_ref):
    o_ref[...] = jnp.dot(a_ref[...], b_ref[...], preferred_element_type=F32).astype(o_ref.dtype)


def _matmul(a, b, out_dtype, tm=512, tn=512):
    m, k = a.shape
    _, n = b.shape
    tn = min(tn, n)
    return pl.pallas_call(
        _mm_kernel,
        grid=(n // tn, m // tm),
        in_specs=[
            pl.BlockSpec((tm, k), lambda j, i: (i, 0)),
            pl.BlockSpec((k, tn), lambda j, i: (0, j)),
        ],
        out_specs=pl.BlockSpec((tm, tn), lambda j, i: (i, j)),
        out_shape=jax.ShapeDtypeStruct((m, n), out_dtype),
        compiler_params=_cparams("parallel", "parallel"),
        name="matmul",
    )(a, b)


def _gated_residual(x_ref, g_ref, acc, o_ref, tm):
    for s in range(tm // ROW_BLK):
        rows = slice(s * ROW_BLK, (s + 1) * ROW_BLK)
        o_ref[rows, :] = x_ref[rows, :] + g_ref[s] * acc[rows, :]


def _cast_weight_tiles(w_refs, wb_refs):
    @pl.when(pl.program_id(1) == 0)
    def _():
        for w_ref, wb_ref in zip(w_refs, wb_refs):
            wb_ref[...] = w_ref[...].astype(BF16)


def _mm_res_kernel(*refs, n_in, tm):
    a_refs, w_refs = refs[:n_in], refs[n_in:2 * n_in]
    x_ref, g_ref, o_ref = refs[2 * n_in:2 * n_in + 3]
    wb_refs = refs[2 * n_in + 3:]
    _cast_weight_tiles(w_refs, wb_refs)
    acc = jnp.dot(a_refs[0][...], wb_refs[0][...], preferred_element_type=F32)
    for a_ref, wb_ref in zip(a_refs[1:], wb_refs[1:]):
        acc = acc + jnp.dot(a_ref[...], wb_ref[...], preferred_element_type=F32)
    _gated_residual(x_ref, g_ref, acc, o_ref, tm)


def _matmul_residual(a_list, w, layer, x, gate, tm=512, tn=512, single_buffer_w=False):
    m, n = x.shape
    sub = tm // ROW_BLK
    kb = w.shape[1] // len(a_list)
    w_mode = dict(pipeline_mode=pl.Buffered(1)) if single_buffer_w else {}
    in_specs = [pl.BlockSpec((tm, kb), lambda j, i: (i, 0)) for _ in a_list]
    in_specs += [pl.BlockSpec((None, kb, tn), functools.partial(lambda j, i, r: (layer, r, j), r=r), **w_mode)
                 for r in range(len(a_list))]
    in_specs += [
        pl.BlockSpec((tm, tn), lambda j, i: (i, j)),
        pl.BlockSpec((sub, 1, tn), lambda j, i: (i, 0, j)),
    ]
    return pl.pallas_call(
        functools.partial(_mm_res_kernel, n_in=len(a_list), tm=tm),
        grid=(n // tn, m // tm),
        in_specs=in_specs,
        out_specs=pl.BlockSpec((tm, tn), lambda j, i: (i, j)),
        out_shape=jax.ShapeDtypeStruct((m, n), F32),
        scratch_shapes=[pltpu.VMEM((kb, tn), BF16) for _ in a_list],
        compiler_params=_cparams("parallel", "arbitrary"),
        name="matmul_residual",
    )(*a_list, *([w] * len(a_list)), x, gate)


def _ffn_up_kernel(h_ref, wg_ref, wu_ref, o_ref, wgb_ref, wub_ref):
    _cast_weight_tiles((wg_ref, wu_ref), (wgb_ref, wub_ref))
    h = h_ref[...]
    g = jnp.dot(h, wgb_ref[...], preferred_element_type=F32)
    u = jnp.dot(h, wub_ref[...], preferred_element_type=F32)
    o_ref[...] = (_silu(g) * u).astype(o_ref.dtype)


def _ffn_up(h, wg, wu, layer, tm=512, tn=512):
    m, k = h.shape
    n = wg.shape[2]
    w_spec = lambda: pl.BlockSpec((None, k, tn), lambda j, i: (layer, 0, j))
    return pl.pallas_call(
        _ffn_up_kernel,
        grid=(n // tn, m // tm),
        in_specs=[pl.BlockSpec((tm, k), lambda j, i: (i, 0)), w_spec(), w_spec()],
        out_specs=pl.BlockSpec((tm, tn), lambda j, i: (i, j)),
        out_shape=jax.ShapeDtypeStruct((m, n), BF16),
        scratch_shapes=[pltpu.VMEM((k, tn), BF16)] * 2,
        compiler_params=_cparams("parallel", "arbitrary"),
        name="ffn_up",
    )(h, wg, wu)


def _shift_rows(x, shift, valid):
    t = x.shape[0]
    return jnp.where(valid, pltpu.roll(x, shift % t, 0), 0.0)


def _token_shift_kernel(p_ref, mu_ref, o_ref, *, grid):
    x = p_ref[...]
    t, c = x.shape
    row = lax.broadcasted_iota(jnp.int32, (t, c), 0)
    lane = lax.broadcasted_iota(jnp.int32, (t, c), 1)
    if grid:
        col = row % GRID_W
        sel = lane % 4
        s = jnp.where(sel == 0, _shift_rows(x, 1, col != 0),
            jnp.where(sel == 1, _shift_rows(x, -1, col != GRID_W - 1),
            jnp.where(sel == 2, _shift_rows(x, GRID_W, row >= GRID_W),
                      _shift_rows(x, -GRID_W, row < t - GRID_W))))
    else:
        s = jnp.where(lane % 2 == 0, _shift_rows(x, 1, row != 0), _shift_rows(x, -1, row != t - 1))
    o_ref[...] = x + (s - x) * mu_ref[...]


def _token_shift(pa, mu, n_seq, t, row_blk0, grid):
    tc = 512
    return pl.pallas_call(
        functools.partial(_token_shift_kernel, grid=grid),
        grid=(n_seq, C_PA_PAD // tc),
        in_specs=[
            pl.BlockSpec((t, tc), lambda b, j: (row_blk0 + b, j)),
            pl.BlockSpec((1, tc), lambda b, j: (0, j)),
        ],
        out_specs=pl.BlockSpec((t, tc), lambda b, j: (b, j)),
        out_shape=jax.ShapeDtypeStruct((n_seq * t, C_PA_PAD), F32),
        compiler_params=_cparams("parallel", "parallel"),
        name="token_shift",
    )(pa, mu)


def _rwkv_lora_kernel(lo_ref, w2_ref, a2_ref, g2_ref, w0_ref, a0_ref, lw_ref, ic_ref, gt_ref):
    lo = lo_ref[...]
    w_in = jnp.tanh(lo[:, 0:2 * LORA])
    a_in = lo[:, 2 * LORA:4 * LORA]
    g_in = _sigmoid(lo[:, 4 * LORA:])
    for d in range(2):
        w_log = -_softplus(-(w0_ref[d] + _bdot(w_in, w2_ref[d]))) - 0.5
        lw_ref[d] = -jnp.exp(w_log)
        ic_ref[d] = _sigmoid(a0_ref[d] + _bdot(a_in, a2_ref[d]))
    gt_ref[...] = _bdot(g_in, g2_ref[...])


def _rwkv_lora(pam, w2bd, a2bd, g2pad, w0, a0, tm=512):
    m = pam.shape[0]
    lo_blk = 3 * C_A // C_LO
    return pl.pallas_call(
        _rwkv_lora_kernel,
        grid=(m // tm,),
        in_specs=[
            pl.BlockSpec((tm, C_LO), lambda i: (i, lo_blk)),
            pl.BlockSpec((2, 2 * LORA, C_A), lambda i: (0, 0, 0)),
            pl.BlockSpec((2, 2 * LORA, C_A), lambda i: (0, 0, 0)),
            pl.BlockSpec((C_LO - 4 * LORA, C_A), lambda i: (0, 0)),
            pl.BlockSpec((2, 1, C_A), lambda i: (0, 0, 0)),
            pl.BlockSpec((2, 1, C_A), lambda i: (0, 0, 0)),
        ],
        out_specs=[
            pl.BlockSpec((2, tm, C_A), lambda i: (0, i, 0)),
            pl.BlockSpec((2, tm, C_A), lambda i: (0, i, 0)),
            pl.BlockSpec((tm, C_A), lambda i: (i, 0)),
        ],
        out_shape=[
            jax.ShapeDtypeStruct((2, m, C_A), F32),
            jax.ShapeDtypeStruct((2, m, C_A), F32),
            jax.ShapeDtypeStruct((m, C_A), F32),
        ],
        compiler_params=_cparams("parallel"),
        name="rwkv_lora",
    )(pam, w2bd, a2bd, g2pad, w0, a0)


def _order_masks(n, reverse):
    rr = lax.broadcasted_iota(jnp.int32, (n, n), 0)
    cc = lax.broadcasted_iota(jnp.int32, (n, n), 1)
    if reverse:
        return cc >= rr, cc > rr
    return cc <= rr, cc < rr


def _run_lockstep(chains):
    results = [None] * len(chains)
    pending = {i: next(c) for i, c in enumerate(chains)}
    while pending:
        products = {i: fn(a, b, dims) for i, (fn, a, b, dims) in pending.items()}
        pending = {}
        for i, prod in products.items():
            try:
                pending[i] = chains[i].send(prod)
            except StopIteration as stop:
                results[i] = stop.value
    return results


def _triangular_inverse(l, order):
    n = l.shape[0]
    ri = lax.broadcasted_iota(jnp.int32, (n, n), 0)
    ci = lax.broadcasted_iota(jnp.int32, (n, n), 1)

    def joins(k):
        return (ri // (2 * k) == ci // (2 * k)) & ((ri // k) != (ci // k))

    x = (ri == ci).astype(F32) + jnp.where(joins(1), l, 0.0)
    k = 2
    while k < order:
        xe = yield _bdot, x, jnp.where(joins(k), l, 0.0), NN
        x = x + (yield _bdot, xe, x, NN)
        k *= 2
    return x


def _split_heads(x, first):
    return jnp.concatenate([jnp.where(first, x, 0.0), jnp.where(first, 0.0, x)], 0)


def _chunk_cumsum(x, reverse, chunk):
    t = x.shape[0]
    pos = lax.broadcasted_iota(jnp.int32, x.shape, 0) % chunk
    s = 1
    while s < chunk:
        if reverse:
            x = x + jnp.where(pos < chunk - s, pltpu.roll(x, t - s, 0), 0.0)
        else:
            x = x + jnp.where(pos >= s, pltpu.roll(x, s, 0), 0.0)
        s *= 2
    return x


def _rwkv_chunk(s_bd, r, lw, cum, k, v, a, b, reverse):
    c = CHUNK
    cum_ex = cum - lw
    tot = cum[0:1] if reverse else cum[c - 1:c]
    e_neg = jnp.exp(-cum)
    e_hat = jnp.exp(tot - cum)
    at = a * jnp.exp(cum_ex)
    rt = r * jnp.exp(cum)
    first = lax.broadcasted_iota(jnp.int32, (c, LANES), 1) < HEAD_A
    lhs = jnp.concatenate([at, rt], 0)
    rbk = jnp.concatenate([_split_heads(b * e_neg, first), _split_heads(k * e_neg, first)], 0)
    g = yield _bdot, lhs, rbk, NT
    ri_g = lax.broadcasted_iota(jnp.int32, (2 * c, 4 * c), 0)
    pos_c = lax.broadcasted_iota(jnp.int32, (2 * c, 4 * c), 1) % c
    pos_r = ri_g % c
    own = jnp.where(ri_g < c, 0, 1)
    keep = (pos_c > pos_r - own) if reverse else (pos_c < pos_r + own)
    g = jnp.where(keep, g, 0.0)
    l_ab = g[0:c, 0:2 * c]
    l_ak = g[0:c, 2 * c:4 * c]
    m_r = g[c:2 * c, :]
    v_bd = _split_heads(v, first)
    sa = yield _bdot, lhs, s_bd, NT
    rhs = sa[0:c] + (yield _bdot, l_ak, v_bd, NN)
    l_bd = _split_heads(l_ab, first)
    t_inv = yield from _triangular_inverse(l_bd, c)
    u_bd = yield _bdot, t_inv, _split_heads(rhs, first), NN
    y = sa[c:2 * c] + (yield _bdot, m_r, jnp.concatenate([u_bd, v_bd], 0), NN)
    u = u_bd[0:c] + u_bd[c:2 * c]
    upd = yield _bdot, jnp.concatenate([u, v], 0), jnp.concatenate([b * e_hat, k * e_hat], 0), TN
    ri = lax.broadcasted_iota(jnp.int32, (LANES, LANES), 0) < HEAD_A
    ci = lax.broadcasted_iota(jnp.int32, (LANES, LANES), 1) < HEAD_A
    s_new = s_bd * jnp.exp(tot) + jnp.where(ri == ci, upd, 0.0)
    return y, s_new


def _head_sum(x, first):
    s0 = jnp.sum(jnp.where(first, x, 0.0), -1, keepdims=True)
    s1 = jnp.sum(jnp.where(first, 0.0, x), -1, keepdims=True)
    return jnp.where(first, s0, s1)


def _rwkv_scan_kernel(r_ref, k_ref, v_ref, lw_ref, ic_ref, gt_ref, kk_ref, ka_ref, rk_ref, lnw_ref, lnb_ref,
                      s0_ref, o_ref, sf_ref, acc_ref, cum_ref, *, t, pairs):
    n = t // CHUNK
    first = lax.broadcasted_iota(jnp.int32, (CHUNK, LANES), 1) < HEAD_A
    for d in range(2):
        cum_ref[d] = _chunk_cumsum(lw_ref[d], d == 1, CHUNK)
    sf_ref[...] = s0_ref[...]

    def body(i, carry):
        where, chains = [], []
        for d in range(2):
            reverse = d == 1
            ci = (n - 1 - i) if reverse else i
            rows = pl.ds(pl.multiple_of(ci * CHUNK, CHUNK), CHUNK)
            for p in range(pairs):
                lanes = slice(p * LANES, (p + 1) * LANES)
                r = r_ref[rows, lanes]
                k = k_ref[rows, lanes]
                v = v_ref[rows, lanes]
                iclr = ic_ref[d, rows, lanes]
                kk = k * kk_ref[:, lanes]
                kk = kk * lax.rsqrt(_head_sum(kk * kk, first) + 1e-12)
                k_d = k * (1.0 + (iclr - 1.0) * ka_ref[:, lanes])
                where.append((d, p, rows, lanes, r, k_d, v))
                chains.append(_rwkv_chunk(sf_ref[d, p], r, lw_ref[d, rows, lanes], cum_ref[d, rows, lanes],
                                          k_d, v, -kk, kk * iclr, reverse))
        for (d, p, rows, lanes, r, k_d, v), (y, s_new) in zip(where, _run_lockstep(chains)):
            sf_ref[d, p] = s_new
            mu = _head_sum(y, first) * (1.0 / HEAD_A)
            yc = y - mu
            var = _head_sum(yc * yc, first) * (1.0 / HEAD_A)
            out = yc * lax.rsqrt(var + GN_EPS) * lnw_ref[:, lanes] + lnb_ref[:, lanes]
            acc_ref[d, rows, lanes] = out + _head_sum(r * k_d * rk_ref[:, lanes], first) * v
        return carry

    lax.fori_loop(0, n, body, 0)
    o_ref[...] = ((acc_ref[0] + acc_ref[1]) * gt_ref[...]).astype(o_ref.dtype)


def _rwkv_scan(pam, lw, ic, gate, k_k, k_a, r_k, ln_w, ln_b, s0_bd, n_seq, t, row_blk0, pairs=4):
    w = pairs * LANES
    nb = C_A // w
    vec = lambda: pl.BlockSpec((1, w), lambda b, h: (0, h))
    rows = lambda off: pl.BlockSpec((t, w), lambda b, h: (row_blk0 + b, off + h))
    rows2 = lambda: pl.BlockSpec((2, t, w), lambda b, h: (0, row_blk0 + b, h))
    state = pl.BlockSpec((None, 2, pairs, LANES, LANES), lambda b, h: (b, 0, h, 0, 0))
    return pl.pallas_call(
        functools.partial(_rwkv_scan_kernel, t=t, pairs=pairs),
        grid=(n_seq, nb),
        in_specs=[
            rows(0), rows(nb), rows(2 * nb), rows2(), rows2(), rows(0),
            vec(), vec(), vec(), vec(), vec(),
            state,
        ],
        out_specs=[pl.BlockSpec((t, w), lambda b, h: (b, h)), state],
        out_shape=[
            jax.ShapeDtypeStruct((n_seq * t, C_A), BF16),
            jax.ShapeDtypeStruct((n_seq, 2, C_A // LANES, LANES, LANES), F32),
        ],
        scratch_shapes=[pltpu.VMEM((2, t, w), F32), pltpu.VMEM((2, t, w), F32)],
        compiler_params=_cparams("parallel", "parallel"),
        name="rwkv_scan",
    )(pam, pam, pam, lw, ic, gate, k_k, k_a, r_k, ln_w, ln_b, s0_bd)


def _gdn_chunk(s, q, k, v, gc, beta, reverse):
    c = q.shape[0]
    incl, strict = _order_masks(c, reverse)
    pick = (lax.broadcasted_iota(jnp.int32, (8, LANES), 1) == 0).astype(F32)
    gr = (yield _fdot, pick, jnp.broadcast_to(gc, (c, LANES)), NT)[0:1, :]
    decay = jnp.where(incl, jnp.exp(jnp.where(incl, gc - gr, 0.0)), 0.0)
    g_last = gc[0:1] if reverse else gc[c - 1:c]
    kb = k * beta
    l = jnp.where(strict, (yield _bdot, kb, k, NT) * decay, 0.0)
    attn = (yield _bdot, q, k, NT) * decay
    a_inv = yield from _triangular_inverse(-l, c)
    eg = jnp.exp(gc)
    sol = yield _bdot, a_inv, jnp.concatenate([v * beta, kb * eg], -1), NN
    u, w = sol[:, 0:HEAD_B], sol[:, HEAD_B:]
    ws = yield _bdot, jnp.concatenate([w, q * eg], 0), s, NN
    v_new = u - ws[0:c]
    o = ws[c:2 * c] + (yield _bdot, attn, v_new, NN)
    s_new = s * jnp.exp(g_last) + (yield _bdot, k * jnp.exp(g_last - gc), v_new, TN)
    return o, s_new


def _gdn_scan_kernel(q_ref, k_ref, v_ref, z_ref, ba_ref, cq_ref, ck_ref, cv_ref, gnw_ref, alog_ref, dtb_ref,
                     s0_ref, o_ref, sf_ref, qs_ref, ks_ref, vs_ref, acc_ref, gc_ref, beta_ref, *, t, heads):
    n = t // CHUNK_B
    hg = pl.program_id(1)
    blk = CHUNK_B
    brow = lax.broadcasted_iota(jnp.int32, (blk, LANES), 0)
    no_row = jnp.zeros((1, LANES), F32)

    def conv_silu(x_ref, w, r0, lanes):
        x = x_ref[r0:r0 + blk, lanes]
        before = x_ref[r0 - 1:r0, lanes] if r0 > 0 else no_row
        after = x_ref[r0 + blk:r0 + blk + 1, lanes] if r0 + blk < t else no_row
        prev = jnp.where(brow == 0, before, pltpu.roll(x, 1, 0))
        nxt = jnp.where(brow == blk - 1, after, pltpu.roll(x, blk - 1, 0))
        return _silu(w[0:1] * prev + w[1:2] * x + w[2:3] * nxt)

    def l2norm(x):
        return x * lax.rsqrt(jnp.sum(x * x, -1, keepdims=True) + 1e-6)

    for g in range(heads):
        lanes = slice(g * LANES, (g + 1) * LANES)
        wq, wk, wv = cq_ref[:, lanes], ck_ref[:, lanes], cv_ref[:, lanes]
        for r0 in range(0, t, blk):
            rows = slice(r0, r0 + blk)
            qs_ref[rows, lanes] = l2norm(conv_silu(q_ref, wq, r0, lanes)) * (HEAD_B ** -0.5)
            ks_ref[rows, lanes] = l2norm(conv_silu(k_ref, wk, r0, lanes))
            vs_ref[rows, lanes] = conv_silu(v_ref, wv, r0, lanes)
    ba = ba_ref[...]
    beta_ref[...] = _sigmoid(ba)
    g_log = -jnp.exp(alog_ref[...]) * _softplus(ba + dtb_ref[...])
    gc_ref[0] = _chunk_cumsum(g_log, False, CHUNK_B)
    gc_ref[1] = _chunk_cumsum(g_log, True, CHUNK_B)
    sf_ref[...] = s0_ref[...]
    lane = lax.broadcasted_iota(jnp.int32, (CHUNK_B, LANES), 1)

    def body(i, carry):
        chains = []
        for d in range(2):
            ci = (n - 1 - i) if d == 1 else i
            rows = pl.ds(pl.multiple_of(ci * CHUNK_B, CHUNK_B), CHUNK_B)
            gcs = gc_ref[d, rows, :]
            bts = beta_ref[rows, :]
            for g in range(heads):
                lanes = slice(g * LANES, (g + 1) * LANES)
                chains.append((d, g, rows, lanes, gcs, bts, sf_ref[d, g],
                               qs_ref[rows, lanes], ks_ref[rows, lanes], vs_ref[rows, lanes]))
        gens = []
        for d, g, rows, lanes, gcs, bts, s, q, k, v in chains:
            col = d * H_B + hg * heads + g
            gc = jnp.sum(jnp.where(lane == 2 * H_B + col, gcs, 0.0), -1, keepdims=True)
            beta = jnp.sum(jnp.where(lane == col, bts, 0.0), -1, keepdims=True)
            gens.append(_gdn_chunk(s, q, k, v, gc, beta, d == 1))
        for (d, g, rows, lanes, *_), (o, s_new) in zip(chains, _run_lockstep(gens)):
            sf_ref[d, g] = s_new
            acc_ref[d, rows, lanes] = o
        return carry

    lax.fori_loop(0, n, body, 0)
    for g in range(heads):
        lanes = slice(g * LANES, (g + 1) * LANES)
        for r0 in range(0, t, blk):
            rows = slice(r0, r0 + blk)
            o = acc_ref[0, rows, lanes] + acc_ref[1, rows, lanes]
            o = o * lax.rsqrt(jnp.mean(o * o, -1, keepdims=True) + RMS_EPS) * gnw_ref[...]
            o_ref[rows, lanes] = (o * _silu(z_ref[rows, lanes])).astype(o_ref.dtype)


def _gdn_scan(pbm, pbs, conv_w, alog_vec, dtb_vec, gn_w, s0, n_seq, t, row_blk0, heads=4):
    w = heads * LANES
    nb = C_B // w
    rows = lambda off: pl.BlockSpec((t, w), lambda b, h: (row_blk0 + b, off + h))
    cw = lambda off: pl.BlockSpec((3, w), lambda b, h: (0, off + h))
    vec = pl.BlockSpec((1, LANES), lambda b, h: (0, 0))
    state = pl.BlockSpec((None, 2, heads, HEAD_B, HEAD_B), lambda b, h: (b, 0, h, 0, 0))
    return pl.pallas_call(
        functools.partial(_gdn_scan_kernel, t=t, heads=heads),
        grid=(n_seq, nb),
        in_specs=[
            rows(0), rows(nb), rows(2 * nb), rows(3 * nb),
            pl.BlockSpec((t, LANES), lambda b, h: (row_blk0 + b, 0)),
            cw(0), cw(nb), cw(2 * nb),
            vec, vec, vec,
            state,
        ],
        out_specs=[pl.BlockSpec((t, w), lambda b, h: (b, h)), state],
        out_shape=[
            jax.ShapeDtypeStruct((n_seq * t, C_B), BF16),
            jax.ShapeDtypeStruct((n_seq, 2, H_B, HEAD_B, HEAD_B), F32),
        ],
        scratch_shapes=[pltpu.VMEM((t, w), F32)] * 3 + [pltpu.VMEM((2, t, w), F32), pltpu.VMEM((2, t, LANES), F32),
                                                         pltpu.VMEM((t, LANES), F32)],
        compiler_params=_cparams("parallel", "parallel"),
        name="gdn_scan",
    )(pbm, pbm, pbm, pbm, pbs, conv_w, conv_w, conv_w, gn_w, alog_vec, dtb_vec, s0)


def _pool_kernel(h_ref, x_ref, w_ref, sc_ref, g_ref, o_ref, *, t):
    gi = pl.program_id(1)
    h = h_ref[...]
    row = lax.broadcasted_iota(jnp.int32, (t, C_G), 0)

    def back(x, n):
        return _shift_rows(x, n, row >= n)

    def fwd(x, n):
        return _shift_rows(x, -n, row < t - n)

    p2 = back(h, 1) + h
    q2 = h + fwd(h, 1)
    p4 = back(p2, 2) + p2
    q4 = q2 + fwd(q2, 2)
    p8 = back(p4, 4) + p4
    q8 = q4 + fwd(q4, 4)
    p_h = jnp.where(gi == 0, h, jnp.where(gi == 1, p2, jnp.where(gi == 2, p4, p8)))
    q_h = jnp.where(gi == 0, h, jnp.where(gi == 1, q2, jnp.where(gi == 2, q4, q8)))
    s = back(p_h, 1) + q_h
    half = jnp.left_shift(1, gi)
    lo = jnp.clip(row - half, 0, t)
    hi = jnp.clip(row + half, 0, t)
    p = s / (hi - lo).astype(F32) - h
    y = _bdot(p, w_ref[...]) * sc_ref[...]
    o_ref[...] = x_ref[...] + g_ref[...] * y


def _pool_mixer(h, x, w_pool, layer, scale, gate_seq, n_seq, t, row_blk0):
    return pl.pallas_call(
        functools.partial(_pool_kernel, t=t),
        grid=(n_seq, POOL_GROUPS),
        in_specs=[
            pl.BlockSpec((t, C_G), lambda b, g: (row_blk0 + b, g)),
            pl.BlockSpec((t, C_G), lambda b, g: (row_blk0 + b, g)),
            pl.BlockSpec((None, None, C_G, C_G), lambda b, g: (layer, g, 0, 0)),
            pl.BlockSpec((1, C_G), lambda b, g: (0, g)),
            pl.BlockSpec((None, 1, C_G), lambda b, g: (b, 0, g)),
        ],
        out_specs=pl.BlockSpec((t, C_G), lambda b, g: (b, g)),
        out_shape=jax.ShapeDtypeStruct((n_seq * t, D_MODEL), F32),
        compiler_params=_cparams("parallel", "parallel"),
        name="pool_mixer",
    )(h, x, w_pool, scale, gate_seq)


def _pad_cols(w, n):
    return jnp.pad(w, ((0, 0), (0, n - w.shape[1])))


def _block_diag_pairs(s):
    b, two, h, n, _ = s.shape
    sp = s.reshape(b, two, h // 2, 2, n, n)
    z = jnp.zeros_like(sp[:, :, :, 0])
    top = jnp.concatenate([sp[:, :, :, 0], z], -1)
    bot = jnp.concatenate([z, sp[:, :, :, 1]], -1)
    return jnp.concatenate([top, bot], -2)


def _diag_pairs(s_bd):
    b, two, hp, _, _ = s_bd.shape
    n = HEAD_A
    return jnp.stack([s_bd[..., :n, :n], s_bd[..., n:, n:]], 3).reshape(b, two, 2 * hp, n, n)


def kernel(x_prompt, x_sample, state_rwkv, state_delta, c, c_ctx, mod_w, mod_b, norm_mix_w, norm_ffn_w, norm_final_w, ab_w_in, ab_w_out, rwkv_mu, rwkv_w0, rwkv_w2, rwkv_a0, rwkv_a2, rwkv_g2, rwkv_k_k, rwkv_k_a, rwkv_r_k, rwkv_ln_w, rwkv_ln_b, gdn_conv_w, gdn_a_log, gdn_dt_bias, gdn_norm_w, pool_w, pool_scale, ffn_w_gate, ffn_w_up, ffn_w_down):
    bp, tp, d = x_prompt.shape
    bs, ts, _ = x_sample.shape
    mp, ms = bp * tp, bs * ts
    m = mp + ms
    assert tp == ROW_BLK and ts % ROW_BLK == 0 and mp % ts == 0
    x = jnp.concatenate([x_prompt.reshape(mp, d), x_sample.reshape(ms, d)], 0)

    n_cond = 16
    cond = jnp.zeros((n_cond, d), F32).at[0].set(c_ctx).at[1:1 + bs].set(c)
    mod = _mod_all(cond, mod_w, mod_b).reshape(DEPTH, n_cond, 6, d)
    mod_blk = jnp.concatenate([
        jnp.broadcast_to(mod[:, 0:1], (DEPTH, mp // ROW_BLK, 6, d)),
        jnp.repeat(mod[:, 1:1 + bs], ts // ROW_BLK, axis=1)], 1)[:, :, :, None, :]
    zeros_blk = jnp.zeros((m // ROW_BLK, 1, d), F32)

    zero_r = jnp.zeros((bp, 2, H_A // 2, LANES, LANES), F32)
    zero_d = jnp.zeros((bp, 2, H_B, HEAD_B, HEAD_B), F32)
    new_r, new_d = [], []

    for l in range(DEPTH):
        shift_m, scale_m, gate_m, shift_f, scale_f, gate_f = (mod_blk[l, :, j] for j in range(6))
        i = l // 2
        if l % 2 == 0:
            h = _norm_mod(x, norm_mix_w[l], scale_m, shift_m, BF16)
            w_in = ab_w_in[i]
            w_pa = _pad_cols(w_in[:, :C_PA], C_PA_PAD).astype(BF16)
            w_pbm = w_in[:, C_PA:C_PA + 4 * C_B].astype(BF16)
            w_pbs = _pad_cols(w_in[:, C_PA + 4 * C_B:], LANES).astype(BF16)
            pa = _matmul(h, w_pa, F32)
            pbm = _matmul(h, w_pbm, F32)
            pbs = _matmul(h, w_pbs, F32)
            mu = _pad_cols(rwkv_mu[i][None, :], C_PA_PAD)
            pam = jnp.concatenate([
                _token_shift(pa, mu, bp, tp, 0, False),
                _token_shift(pa, mu, bs, ts, mp // ts, True)], 0)
            w2bd = jnp.zeros((2, 2 * LORA, C_A), F32).at[0, :LORA].set(rwkv_w2[i, 0]).at[1, LORA:].set(rwkv_w2[i, 1])
            a2bd = jnp.zeros((2, 2 * LORA, C_A), F32).at[0, :LORA].set(rwkv_a2[i, 0]).at[1, LORA:].set(rwkv_a2[i, 1])
            g2pad = jnp.pad(rwkv_g2[i], ((0, C_LO - 4 * LORA - GATE_LORA), (0, 0)))
            lw, ic, gate = _rwkv_lora(pam, w2bd.astype(BF16), a2bd.astype(BF16), g2pad.astype(BF16),
                                      rwkv_w0[i][:, None, :], rwkv_a0[i][:, None, :])
            vecs = [v_[i].reshape(1, C_A) for v_ in (rwkv_k_k, rwkv_k_a, rwkv_r_k, rwkv_ln_w, rwkv_ln_b)]
            oa_p, sr_p = _rwkv_scan(pam, lw, ic, gate, *vecs, zero_r, bp, tp, 0)
            oa_s, _ = _rwkv_scan(pam, lw, ic, gate, *vecs, _block_diag_pairs(state_rwkv[:, i]), bs, ts, mp // ts)
            gnw = gdn_norm_w[i].reshape(1, HEAD_B)
            on_decay_lanes = lambda p_: jnp.zeros((1, LANES), F32).at[0, 2 * H_B:4 * H_B].set(p_.reshape(2 * H_B))
            alog_vec, dtb_vec = on_decay_lanes(gdn_a_log[i]), on_decay_lanes(gdn_dt_bias[i])
            ob_p, sd_p = _gdn_scan(pbm, pbs, gdn_conv_w[i], alog_vec, dtb_vec, gnw, zero_d, bp, tp, 0)
            ob_s, _ = _gdn_scan(pbm, pbs, gdn_conv_w[i], alog_vec, dtb_vec, gnw, state_delta[:, i],
                                bs, ts, mp // ts)
            new_r.append(_diag_pairs(sr_p))
            new_d.append(sd_p)
            o_a = jnp.concatenate([oa_p, oa_s], 0)
            o_b = jnp.concatenate([ob_p, ob_s], 0)
            x = _matmul_residual([o_a, o_b], ab_w_out, i, x, gate_m)
        else:
            h = _norm_mod(x, norm_mix_w[l], scale_m, shift_m, F32)
            gate_p = jnp.broadcast_to(mod[l, 0:1, 2][:, None, :], (bp, 1, d))
            gate_s = mod[l, 1:1 + bs, 2][:, None, :]
            sc = pool_scale[i].reshape(1, d)
            x = jnp.concatenate([
                _pool_mixer(h, x, pool_w, i, sc, gate_p, bp, tp, 0),
                _pool_mixer(h, x, pool_w, i, sc, gate_s, bs, ts, mp // ts)], 0)
        h = _norm_mod(x, norm_ffn_w[l], scale_f, shift_f, BF16)
        act = _ffn_up(h, ffn_w_gate, ffn_w_up, l)
        x = _matmul_residual([act], ffn_w_down, l, x, gate_f, single_buffer_w=True)

    y = _norm_mod(x, norm_final_w, zeros_blk, zeros_blk, F32)
    y_prompt = y[:mp].reshape(bp, tp, d)
    y_sample = y[mp:].reshape(bs, ts, d)
    return (y_prompt, y_sample, jnp.stack(new_r, 1), jnp.stack(new_d, 1))
```

```python
import functools

import jax
import jax.numpy as jnp
from jax import lax
from jax.experimental import pallas as pl
from jax.experimental.pallas import tpu as pltpu

F32 = jnp.float32
BF16 = jnp.bfloat16

D_MODEL = 2048
DEPTH = 4
GRID_W = 64
N_AB = 2
C_A = 1024
HEAD_A = 64
H_A = 16
LORA = 64
GATE_LORA = 160
C_B = 1024
HEAD_B = 128
H_B = 8
C_PA = 3 * C_A + 4 * LORA + GATE_LORA
C_PA_PAD = 3584
C_LO = C_PA_PAD - 3 * C_A
POOL_GROUPS = 4
C_G = D_MODEL // POOL_GROUPS
D_FF = 5632
RMS_EPS = 1e-6
GN_EPS = 64e-5
CHUNK = 64
CHUNK_B = 128
ROW_BLK = 256
LANES = 128
BA_LANE0 = C_PA % LANES
VMEM_LIMIT = 48 * 1024 * 1024

NT = (((1,), (1,)), ((), ()))
TN = (((0,), (0,)), ((), ()))
NN = (((1,), (0,)), ((), ()))


def _cparams(*sem):
    return pltpu.CompilerParams(dimension_semantics=sem, vmem_limit_bytes=VMEM_LIMIT)


def _bdot(a, b, dims=NN):
    return lax.dot_general(a.astype(BF16), b.astype(BF16), dims, preferred_element_type=F32)


def _fdot(a, b, dims=NN):
    return lax.dot_general(a, b, dims, precision=lax.Precision.HIGHEST, preferred_element_type=F32)


def _sigmoid(x):
    return 0.5 + 0.5 * jnp.tanh(0.5 * x)


def _silu(x):
    return x * _sigmoid(x)


def _softplus(x):
    return jnp.maximum(x, 0.0) + jnp.log(1.0 + jnp.exp(-jnp.abs(x)))


def _mod_kernel(c_ref, w_ref, b_ref, o_ref):
    act = _silu(c_ref[...])
    o_ref[...] = _fdot(act, w_ref[...]) + b_ref[...]


def _mod_all(cond, mod_w, mod_b):
    n_c, d = cond.shape
    depth, _, n = mod_w.shape
    tn = 1024
    return pl.pallas_call(
        _mod_kernel,
        grid=(depth, n // tn),
        in_specs=[
            pl.BlockSpec((n_c, d), lambda l, j: (0, 0)),
            pl.BlockSpec((None, d, tn), lambda l, j: (l, 0, j)),
            pl.BlockSpec((None, 1, tn), lambda l, j: (l, 0, j)),
        ],
        out_specs=pl.BlockSpec((None, n_c, tn), lambda l, j: (l, 0, j)),
        out_shape=jax.ShapeDtypeStruct((depth, n_c, n), F32),
        compiler_params=_cparams("parallel", "parallel"),
        name="mod_all",
    )(cond, mod_w, mod_b.reshape(depth, 1, n))


def _norm_mod_kernel(x_ref, nw_ref, sc_ref, sh_ref, o_ref):
    x = x_ref[...]
    ms = jnp.mean(x * x, -1, keepdims=True)
    y = x * lax.rsqrt(ms + RMS_EPS) * nw_ref[...]
    o_ref[...] = (y * (1.0 + sc_ref[...]) + sh_ref[...]).astype(o_ref.dtype)


def _norm_mod(x, nw, scale, shift, out_dtype):
    m, d = x.shape
    nb = m // ROW_BLK
    return pl.pallas_call(
        _norm_mod_kernel,
        grid=(nb,),
        in_specs=[
            pl.BlockSpec((ROW_BLK, d), lambda i: (i, 0)),
            pl.BlockSpec((1, d), lambda i: (0, 0)),
            pl.BlockSpec((None, 1, d), lambda i: (i, 0, 0)),
            pl.BlockSpec((None, 1, d), lambda i: (i, 0, 0)),
        ],
        out_specs=pl.BlockSpec((ROW_BLK, d), lambda i: (i, 0)),
        out_shape=jax.ShapeDtypeStruct((m, d), out_dtype),
        compiler_params=_cparams("parallel"),
        name="norm_mod",
    )(x, nw.reshape(1, d), scale, shift)


def _norm_mod_rows(x_ref, nw_ref, sc_ref, sh_ref, tm):
    parts = []
    for s in range(tm // ROW_BLK):
        x = x_ref[s * ROW_BLK:(s + 1) * ROW_BLK, :]
        y = x * lax.rsqrt(jnp.mean(x * x, -1, keepdims=True) + RMS_EPS) * nw_ref[...]
        parts.append((y * (1.0 + sc_ref[s]) + sh_ref[s]).astype(BF16))
    return jnp.concatenate(parts, 0)


def _norm_specs(tm, d):
    sub = tm // ROW_BLK
    return [
        pl.BlockSpec((tm, d), lambda j, i: (i, 0)),
        pl.BlockSpec((1, d), lambda j, i: (0, 0)),
        pl.BlockSpec((sub, 1, d), lambda j, i: (i, 0, 0)),
        pl.BlockSpec((sub, 1, d), lambda j, i: (i, 0, 0)),
    ]


def _norm_mm_kernel(x_ref, nw_ref, sc_ref, sh_ref, b_ref, o_ref, *, tm):
    h = _norm_mod_rows(x_ref, nw_ref, sc_ref, sh_ref, tm)
    o_ref[...] = jnp.dot(h, b_ref[...], preferred_element_type=F32)


def _norm_matmul(x, nw, scale, shift, b, tm=512, tn=512):
    m, d = x.shape
    n = b.shape[1]
    return pl.pallas_call(
        functools.partial(_norm_mm_kernel, tm=tm),
        grid=(n // tn, m // tm),
        in_specs=_norm_specs(tm, d) + [pl.BlockSpec((d, tn), lambda j, i: (0, j))],
        out_specs=pl.BlockSpec((tm, tn), lambda j, i: (i, j)),
        out_shape=jax.ShapeDtypeStruct((m, n), F32),
        compiler_params=_cparams("parallel", "parallel"),
        name="norm_matmul",
    )(x, nw.reshape(1, d), scale, shift, b)


def _gated_residual(x_ref, g_ref, acc, o_ref, tm):
    for s in range(tm // ROW_BLK):
        rows = slice(s * ROW_BLK, (s + 1) * ROW_BLK)
        o_ref[rows, :] = x_ref[rows, :] + g_ref[s] * acc[rows, :]


def _cast_weight_tiles(w_refs, wb_refs):
    @pl.when(pl.program_id(1) == 0)
    def _():
        for w_ref, wb_ref in zip(w_refs, wb_refs):
            wb_ref[...] = w_ref[...].astype(BF16)


def _mm_res_kernel(*refs, n_in, tm):
    a_refs, w_refs = refs[:n_in], refs[n_in:2 * n_in]
    x_ref, g_ref, o_ref = refs[2 * n_in:2 * n_in + 3]
    wb_refs = refs[2 * n_in + 3:]
    _cast_weight_tiles(w_refs, wb_refs)
    acc = jnp.dot(a_refs[0][...], wb_refs[0][...], preferred_element_type=F32)
    for a_ref, wb_ref in zip(a_refs[1:], wb_refs[1:]):
        acc = acc + jnp.dot(a_ref[...], wb_ref[...], preferred_element_type=F32)
    _gated_residual(x_ref, g_ref, acc, o_ref, tm)


def _matmul_residual(a_list, w, layer, x, gate, tm=512, tn=512, single_buffer_w=False):
    m, n = x.shape
    sub = tm // ROW_BLK
    kb = w.shape[1] // len(a_list)
    w_mode = dict(pipeline_mode=pl.Buffered(1)) if single_buffer_w else {}
    in_specs = [pl.BlockSpec((tm, kb), lambda j, i: (i, 0)) for _ in a_list]
    in_specs += [pl.BlockSpec((None, kb, tn), functools.partial(lambda j, i, r: (layer, r, j), r=r), **w_mode)
                 for r in range(len(a_list))]
    in_specs += [
        pl.BlockSpec((tm, tn), lambda j, i: (i, j)),
        pl.BlockSpec((sub, 1, tn), lambda j, i: (i, 0, j)),
    ]
    return pl.pallas_call(
        functools.partial(_mm_res_kernel, n_in=len(a_list), tm=tm),
        grid=(n // tn, m // tm),
        in_specs=in_specs,
        out_specs=pl.BlockSpec((tm, tn), lambda j, i: (i, j)),
        out_shape=jax.ShapeDtypeStruct((m, n), F32),
        scratch_shapes=[pltpu.VMEM((kb, tn), BF16) for _ in a_list],
        compiler_params=_cparams("parallel", "arbitrary"),
        name="matmul_residual",
    )(*a_list, *([w] * len(a_list)), x, gate)


def _ffn_up_kernel(x_ref, nw_ref, sc_ref, sh_ref, wg_ref, wu_ref, o_ref, wgb_ref, wub_ref, *, tm):
    _cast_weight_tiles((wg_ref, wu_ref), (wgb_ref, wub_ref))
    h = _norm_mod_rows(x_ref, nw_ref, sc_ref, sh_ref, tm)
    g = jnp.dot(h, wgb_ref[...], preferred_element_type=F32)
    u = jnp.dot(h, wub_ref[...], preferred_element_type=F32)
    o_ref[...] = (_silu(g) * u).astype(o_ref.dtype)


def _ffn_up(x, nw, scale, shift, wg, wu, layer, tm=512, tn=512):
    m, d = x.shape
    n = wg.shape[2]
    w_spec = lambda: pl.BlockSpec((None, d, tn), lambda j, i: (layer, 0, j))
    return pl.pallas_call(
        functools.partial(_ffn_up_kernel, tm=tm),
        grid=(n // tn, m // tm),
        in_specs=_norm_specs(tm, d) + [w_spec(), w_spec()],
        out_specs=pl.BlockSpec((tm, tn), lambda j, i: (i, j)),
        out_shape=jax.ShapeDtypeStruct((m, n), BF16),
        scratch_shapes=[pltpu.VMEM((d, tn), BF16)] * 2,
        compiler_params=_cparams("parallel", "arbitrary"),
        name="ffn_up",
    )(x, nw.reshape(1, d), scale, shift, wg, wu)


def _shift_rows(x, shift, valid):
    t = x.shape[0]
    return jnp.where(valid, pltpu.roll(x, shift % t, 0), 0.0)


def _token_shift_kernel(p_ref, mu_ref, o_ref, *, ctx_blocks, t_ctx):
    x = p_ref[...]
    t, c = x.shape
    row = lax.broadcasted_iota(jnp.int32, (t, c), 0)
    lane = lax.broadcasted_iota(jnp.int32, (t, c), 1)
    mu = mu_ref[...]

    @pl.when(pl.program_id(0) < ctx_blocks)
    def _():
        pos = row % t_ctx
        s = jnp.where(lane % 2 == 0, _shift_rows(x, 1, pos != 0), _shift_rows(x, -1, pos != t_ctx - 1))
        o_ref[...] = x + (s - x) * mu

    @pl.when(pl.program_id(0) >= ctx_blocks)
    def _():
        col = row % GRID_W
        sel = lane % 4
        s = jnp.where(sel == 0, _shift_rows(x, 1, col != 0),
            jnp.where(sel == 1, _shift_rows(x, -1, col != GRID_W - 1),
            jnp.where(sel == 2, _shift_rows(x, GRID_W, row >= GRID_W),
                      _shift_rows(x, -GRID_W, row < t - GRID_W))))
        o_ref[...] = x + (s - x) * mu


def _token_shift(pa, mu, t, ctx_blocks, t_ctx):
    tc = 512
    m = pa.shape[0]
    return pl.pallas_call(
        functools.partial(_token_shift_kernel, ctx_blocks=ctx_blocks, t_ctx=t_ctx),
        grid=(m // t, C_PA_PAD // tc),
        in_specs=[
            pl.BlockSpec((t, tc), lambda b, j: (b, j)),
            pl.BlockSpec((1, tc), lambda b, j: (0, j)),
        ],
        out_specs=pl.BlockSpec((t, tc), lambda b, j: (b, j)),
        out_shape=jax.ShapeDtypeStruct((m, C_PA_PAD), F32),
        compiler_params=_cparams("parallel", "parallel"),
        name="token_shift",
    )(pa, mu)


def _rwkv_lora_kernel(lo_ref, w2_ref, a2_ref, g2_ref, w0_ref, a0_ref, lw_ref, ic_ref, gt_ref):
    lo = lo_ref[...]
    w_in = jnp.tanh(lo[:, 0:2 * LORA])
    a_in = lo[:, 2 * LORA:4 * LORA]
    g_in = _sigmoid(lo[:, 4 * LORA:])
    for d in range(2):
        w_log = -_softplus(-(w0_ref[d] + _bdot(w_in, w2_ref[d]))) - 0.5
        lw_ref[d] = -jnp.exp(w_log)
        ic_ref[d] = _sigmoid(a0_ref[d] + _bdot(a_in, a2_ref[d]))
    gt_ref[...] = _bdot(g_in, g2_ref[...])


def _rwkv_lora(pam, w2bd, a2bd, g2pad, w0, a0, tm=512):
    m = pam.shape[0]
    lo_blk = 3 * C_A // C_LO
    return pl.pallas_call(
        _rwkv_lora_kernel,
        grid=(m // tm,),
        in_specs=[
            pl.BlockSpec((tm, C_LO), lambda i: (i, lo_blk)),
            pl.BlockSpec((2, 2 * LORA, C_A), lambda i: (0, 0, 0)),
            pl.BlockSpec((2, 2 * LORA, C_A), lambda i: (0, 0, 0)),
            pl.BlockSpec((C_LO - 4 * LORA, C_A), lambda i: (0, 0)),
            pl.BlockSpec((2, 1, C_A), lambda i: (0, 0, 0)),
            pl.BlockSpec((2, 1, C_A), lambda i: (0, 0, 0)),
        ],
        out_specs=[
            pl.BlockSpec((2, tm, C_A), lambda i: (0, i, 0)),
            pl.BlockSpec((2, tm, C_A), lambda i: (0, i, 0)),
            pl.BlockSpec((tm, C_A), lambda i: (i, 0)),
        ],
        out_shape=[
            jax.ShapeDtypeStruct((2, m, C_A), F32),
            jax.ShapeDtypeStruct((2, m, C_A), F32),
            jax.ShapeDtypeStruct((m, C_A), F32),
        ],
        compiler_params=_cparams("parallel"),
        name="rwkv_lora",
    )(pam, w2bd, a2bd, g2pad, w0, a0)


def _order_masks(n, reverse):
    rr = lax.broadcasted_iota(jnp.int32, (n, n), 0)
    cc = lax.broadcasted_iota(jnp.int32, (n, n), 1)
    if reverse:
        return cc >= rr, cc > rr
    return cc <= rr, cc < rr


def _run_lockstep(chains):
    results = [None] * len(chains)
    pending = {i: next(c) for i, c in enumerate(chains)}
    while pending:
        products = {i: fn(a, b, dims) for i, (fn, a, b, dims) in pending.items()}
        pending = {}
        for i, prod in products.items():
            try:
                pending[i] = chains[i].send(prod)
            except StopIteration as stop:
                results[i] = stop.value
    return results


def _triangular_inverse(l, order):
    n = l.shape[0]
    ri = lax.broadcasted_iota(jnp.int32, (n, n), 0)
    ci = lax.broadcasted_iota(jnp.int32, (n, n), 1)

    def joins(k):
        return (ri // (2 * k) == ci // (2 * k)) & ((ri // k) != (ci // k))

    x = (ri == ci).astype(F32) + jnp.where(joins(1), l, 0.0)
    k = 2
    while k < order:
        xe = yield _bdot, x, jnp.where(joins(k), l, 0.0), NN
        x = x + (yield _bdot, xe, x, NN)
        k *= 2
    return x


def _split_heads(x, first):
    return jnp.concatenate([jnp.where(first, x, 0.0), jnp.where(first, 0.0, x)], 0)


def _chunk_cumsum(x, reverse, chunk):
    t = x.shape[0]
    pos = lax.broadcasted_iota(jnp.int32, x.shape, 0) % chunk
    s = 1
    while s < chunk:
        if reverse:
            x = x + jnp.where(pos < chunk - s, pltpu.roll(x, t - s, 0), 0.0)
        else:
            x = x + jnp.where(pos >= s, pltpu.roll(x, s, 0), 0.0)
        s *= 2
    return x


def _rwkv_chunk(s_bd, r, lw, cum, k, v, a, b, reverse):
    c = CHUNK
    cum_ex = cum - lw
    tot = cum[0:1] if reverse else cum[c - 1:c]
    e_neg = jnp.exp(-cum)
    e_hat = jnp.exp(tot - cum)
    at = a * jnp.exp(cum_ex)
    rt = r * jnp.exp(cum)
    first = lax.broadcasted_iota(jnp.int32, (c, LANES), 1) < HEAD_A
    lhs = jnp.concatenate([at, rt], 0)
    rbk = jnp.concatenate([_split_heads(b * e_neg, first), _split_heads(k * e_neg, first)], 0)
    g = yield _bdot, lhs, rbk, NT
    ri_g = lax.broadcasted_iota(jnp.int32, (2 * c, 4 * c), 0)
    pos_c = lax.broadcasted_iota(jnp.int32, (2 * c, 4 * c), 1) % c
    pos_r = ri_g % c
    own = jnp.where(ri_g < c, 0, 1)
    keep = (pos_c > pos_r - own) if reverse else (pos_c < pos_r + own)
    g = jnp.where(keep, g, 0.0)
    l_ab = g[0:c, 0:2 * c]
    l_ak = g[0:c, 2 * c:4 * c]
    m_r = g[c:2 * c, :]
    v_bd = _split_heads(v, first)
    sa = yield _bdot, lhs, s_bd, NT
    rhs = sa[0:c] + (yield _bdot, l_ak, v_bd, NN)
    l_bd = _split_heads(l_ab, first)
    t_inv = yield from _triangular_inverse(l_bd, c)
    u_bd = yield _bdot, t_inv, _split_heads(rhs, first), NN
    y = sa[c:2 * c] + (yield _bdot, m_r, jnp.concatenate([u_bd, v_bd], 0), NN)
    u = u_bd[0:c] + u_bd[c:2 * c]
    upd = yield _bdot, jnp.concatenate([u, v], 0), jnp.concatenate([b * e_hat, k * e_hat], 0), TN
    ri = lax.broadcasted_iota(jnp.int32, (LANES, LANES), 0) < HEAD_A
    ci = lax.broadcasted_iota(jnp.int32, (LANES, LANES), 1) < HEAD_A
    s_new = s_bd * jnp.exp(tot) + jnp.where(ri == ci, upd, 0.0)
    return y, s_new


def _head_sum(x, first):
    s0 = jnp.sum(jnp.where(first, x, 0.0), -1, keepdims=True)
    s1 = jnp.sum(jnp.where(first, 0.0, x), -1, keepdims=True)
    return jnp.where(first, s0, s1)


def _rwkv_scan_kernel(r_ref, k_ref, v_ref, lw_ref, ic_ref, gt_ref, kk_ref, ka_ref, rk_ref, lnw_ref, lnb_ref,
                      s0_ref, o_ref, sf_ref, acc_ref, cum_ref, *, t, pairs):
    n = t // CHUNK
    first = lax.broadcasted_iota(jnp.int32, (CHUNK, LANES), 1) < HEAD_A
    for d in range(2):
        cum_ref[d] = _chunk_cumsum(lw_ref[d], d == 1, CHUNK)
    sf_ref[...] = s0_ref[...]

    def body(i, carry):
        where, chains = [], []
        for d in range(2):
            reverse = d == 1
            ci = (n - 1 - i) if reverse else i
            rows = pl.ds(pl.multiple_of(ci * CHUNK, CHUNK), CHUNK)
            for p in range(pairs):
                lanes = slice(p * LANES, (p + 1) * LANES)
                r = r_ref[rows, lanes]
                k = k_ref[rows, lanes]
                v = v_ref[rows, lanes]
                iclr = ic_ref[d, rows, lanes]
                kk = k * kk_ref[:, lanes]
                kk = kk * lax.rsqrt(_head_sum(kk * kk, first) + 1e-12)
                k_d = k * (1.0 + (iclr - 1.0) * ka_ref[:, lanes])
                where.append((d, p, rows, lanes, r, k_d, v))
                chains.append(_rwkv_chunk(sf_ref[d, p], r, lw_ref[d, rows, lanes], cum_ref[d, rows, lanes],
                                          k_d, v, -kk, kk * iclr, reverse))
        for (d, p, rows, lanes, r, k_d, v), (y, s_new) in zip(where, _run_lockstep(chains)):
            sf_ref[d, p] = s_new
            mu = _head_sum(y, first) * (1.0 / HEAD_A)
            yc = y - mu
            var = _head_sum(yc * yc, first) * (1.0 / HEAD_A)
            out = yc * lax.rsqrt(var + GN_EPS) * lnw_ref[:, lanes] + lnb_ref[:, lanes]
            acc_ref[d, rows, lanes] = out + _head_sum(r * k_d * rk_ref[:, lanes], first) * v
        return carry

    lax.fori_loop(0, n, body, 0)
    o_ref[...] = ((acc_ref[0] + acc_ref[1]) * gt_ref[...]).astype(o_ref.dtype)


def _rwkv_scan(pam, lw, ic, gate, k_k, k_a, r_k, ln_w, ln_b, s0_bd, n_seq, t, row_blk0, pairs=4):
    w = pairs * LANES
    nb = C_A // w
    vec = lambda: pl.BlockSpec((1, w), lambda b, h: (0, h))
    rows = lambda off: pl.BlockSpec((t, w), lambda b, h: (row_blk0 + b, off + h))
    rows2 = lambda: pl.BlockSpec((2, t, w), lambda b, h: (0, row_blk0 + b, h))
    state = pl.BlockSpec((None, 2, pairs, LANES, LANES), lambda b, h: (b, 0, h, 0, 0))
    return pl.pallas_call(
        functools.partial(_rwkv_scan_kernel, t=t, pairs=pairs),
        grid=(n_seq, nb),
        in_specs=[
            rows(0), rows(nb), rows(2 * nb), rows2(), rows2(), rows(0),
            vec(), vec(), vec(), vec(), vec(),
            state,
        ],
        out_specs=[pl.BlockSpec((t, w), lambda b, h: (b, h)), state],
        out_shape=[
            jax.ShapeDtypeStruct((n_seq * t, C_A), BF16),
            jax.ShapeDtypeStruct((n_seq, 2, C_A // LANES, LANES, LANES), F32),
        ],
        scratch_shapes=[pltpu.VMEM((2, t, w), F32), pltpu.VMEM((2, t, w), F32)],
        compiler_params=_cparams("parallel", "parallel"),
        name="rwkv_scan",
    )(pam, pam, pam, lw, ic, gate, k_k, k_a, r_k, ln_w, ln_b, s0_bd)


def _gdn_chunk(s, q, k, v, gc, beta, reverse):
    c = q.shape[0]
    incl, strict = _order_masks(c, reverse)
    pick = (lax.broadcasted_iota(jnp.int32, (8, LANES), 1) == 0).astype(F32)
    gr = (yield _fdot, pick, jnp.broadcast_to(gc, (c, LANES)), NT)[0:1, :]
    decay = jnp.where(incl, jnp.exp(jnp.where(incl, gc - gr, 0.0)), 0.0)
    g_last = gc[0:1] if reverse else gc[c - 1:c]
    kb = k * beta
    l = jnp.where(strict, (yield _bdot, kb, k, NT) * decay, 0.0)
    attn = (yield _bdot, q, k, NT) * decay
    a_inv = yield from _triangular_inverse(-l, c)
    eg = jnp.exp(gc)
    sol = yield _bdot, a_inv, jnp.concatenate([v * beta, kb * eg], -1), NN
    u, w = sol[:, 0:HEAD_B], sol[:, HEAD_B:]
    ws = yield _bdot, jnp.concatenate([w, q * eg], 0), s, NN
    v_new = u - ws[0:c]
    o = ws[c:2 * c] + (yield _bdot, attn, v_new, NN)
    s_new = s * jnp.exp(g_last) + (yield _bdot, k * jnp.exp(g_last - gc), v_new, TN)
    return o, s_new


def _gdn_scan_kernel(q_ref, k_ref, v_ref, z_ref, ba_ref, cq_ref, ck_ref, cv_ref, gnw_ref, alog_ref, dtb_ref,
                     s0_ref, o_ref, sf_ref, qs_ref, ks_ref, vs_ref, acc_ref, gc_ref, beta_ref, *, t, heads):
    n = t // CHUNK_B
    hg = pl.program_id(1)
    blk = CHUNK_B
    brow = lax.broadcasted_iota(jnp.int32, (blk, LANES), 0)
    no_row = jnp.zeros((1, LANES), F32)

    def conv_silu(x_ref, w, r0, lanes):
        x = x_ref[r0:r0 + blk, lanes]
        before = x_ref[r0 - 1:r0, lanes] if r0 > 0 else no_row
        after = x_ref[r0 + blk:r0 + blk + 1, lanes] if r0 + blk < t else no_row
        prev = jnp.where(brow == 0, before, pltpu.roll(x, 1, 0))
        nxt = jnp.where(brow == blk - 1, after, pltpu.roll(x, blk - 1, 0))
        return _silu(w[0:1] * prev + w[1:2] * x + w[2:3] * nxt)

    def l2norm(x):
        return x * lax.rsqrt(jnp.sum(x * x, -1, keepdims=True) + 1e-6)

    for g in range(heads):
        lanes = slice(g * LANES, (g + 1) * LANES)
        wq, wk, wv = cq_ref[:, lanes], ck_ref[:, lanes], cv_ref[:, lanes]
        for r0 in range(0, t, blk):
            rows = slice(r0, r0 + blk)
            qs_ref[rows, lanes] = l2norm(conv_silu(q_ref, wq, r0, lanes)) * (HEAD_B ** -0.5)
            ks_ref[rows, lanes] = l2norm(conv_silu(k_ref, wk, r0, lanes))
            vs_ref[rows, lanes] = conv_silu(v_ref, wv, r0, lanes)
    ba = ba_ref[...]
    beta_ref[...] = _sigmoid(ba)
    g_log = -jnp.exp(alog_ref[...]) * _softplus(ba + dtb_ref[...])
    gc_ref[0] = _chunk_cumsum(g_log, False, CHUNK_B)
    gc_ref[1] = _chunk_cumsum(g_log, True, CHUNK_B)
    sf_ref[...] = s0_ref[...]
    lane = lax.broadcasted_iota(jnp.int32, (CHUNK_B, LANES), 1)

    def body(i, carry):
        chains = []
        for d in range(2):
            ci = (n - 1 - i) if d == 1 else i
            rows = pl.ds(pl.multiple_of(ci * CHUNK_B, CHUNK_B), CHUNK_B)
            gcs = gc_ref[d, rows, :]
            bts = beta_ref[rows, :]
            for g in range(heads):
                lanes = slice(g * LANES, (g + 1) * LANES)
                chains.append((d, g, rows, lanes, gcs, bts, sf_ref[d, g],
                               qs_ref[rows, lanes], ks_ref[rows, lanes], vs_ref[rows, lanes]))
        gens = []
        for d, g, rows, lanes, gcs, bts, s, q, k, v in chains:
            col = d * H_B + hg * heads + g
            gc = jnp.sum(jnp.where(lane == BA_LANE0 + 2 * H_B + col, gcs, 0.0), -1, keepdims=True)
            beta = jnp.sum(jnp.where(lane == BA_LANE0 + col, bts, 0.0), -1, keepdims=True)
            gens.append(_gdn_chunk(s, q, k, v, gc, beta, d == 1))
        for (d, g, rows, lanes, *_), (o, s_new) in zip(chains, _run_lockstep(gens)):
            sf_ref[d, g] = s_new
            acc_ref[d, rows, lanes] = o
        return carry

    lax.fori_loop(0, n, body, 0)
    for g in range(heads):
        lanes = slice(g * LANES, (g + 1) * LANES)
        for r0 in range(0, t, blk):
            rows = slice(r0, r0 + blk)
            o = acc_ref[0, rows, lanes] + acc_ref[1, rows, lanes]
            o = o * lax.rsqrt(jnp.mean(o * o, -1, keepdims=True) + RMS_EPS) * gnw_ref[...]
            o_ref[rows, lanes] = (o * _silu(z_ref[rows, lanes])).astype(o_ref.dtype)


def _gdn_scan(pbm, pa, conv_w, alog_vec, dtb_vec, gn_w, s0, n_seq, t, row_blk0, heads=4):
    w = heads * LANES
    nb = C_B // w
    rows = lambda off: pl.BlockSpec((t, w), lambda b, h: (row_blk0 + b, off + h))
    cw = lambda off: pl.BlockSpec((3, w), lambda b, h: (0, off + h))
    vec = pl.BlockSpec((1, LANES), lambda b, h: (0, 0))
    state = pl.BlockSpec((None, 2, heads, HEAD_B, HEAD_B), lambda b, h: (b, 0, h, 0, 0))
    return pl.pallas_call(
        functools.partial(_gdn_scan_kernel, t=t, heads=heads),
        grid=(n_seq, nb),
        in_specs=[
            rows(0), rows(nb), rows(2 * nb), rows(3 * nb),
            pl.BlockSpec((t, LANES), lambda b, h: (row_blk0 + b, C_PA // LANES)),
            cw(0), cw(nb), cw(2 * nb),
            vec, vec, vec,
            state,
        ],
        out_specs=[pl.BlockSpec((t, w), lambda b, h: (b, h)), state],
        out_shape=[
            jax.ShapeDtypeStruct((n_seq * t, C_B), BF16),
            jax.ShapeDtypeStruct((n_seq, 2, H_B, HEAD_B, HEAD_B), F32),
        ],
        scratch_shapes=[pltpu.VMEM((t, w), F32)] * 3 + [pltpu.VMEM((2, t, w), F32), pltpu.VMEM((2, t, LANES), F32),
                                                         pltpu.VMEM((t, LANES), F32)],
        compiler_params=_cparams("parallel", "parallel"),
        name="gdn_scan",
    )(pbm, pbm, pbm, pbm, pa, conv_w, conv_w, conv_w, gn_w, alog_vec, dtb_vec, s0)


def _pool_kernel(h_ref, x_ref, w_ref, sc_ref, g_ref, o_ref, *, ctx_blocks, t_ctx):
    gi = pl.program_id(1)
    h = h_ref[...]
    t = jnp.where(pl.program_id(0) < ctx_blocks, t_ctx, h.shape[0])
    row = lax.broadcasted_iota(jnp.int32, h.shape, 0) % t

    def back(x, n):
        return _shift_rows(x, n, row >= n)

    def fwd(x, n):
        return _shift_rows(x, -n, row < t - n)

    p2 = back(h, 1) + h
    q2 = h + fwd(h, 1)
    p4 = back(p2, 2) + p2
    q4 = q2 + fwd(q2, 2)
    p8 = back(p4, 4) + p4
    q8 = q4 + fwd(q4, 4)
    p_h = jnp.where(gi == 0, h, jnp.where(gi == 1, p2, jnp.where(gi == 2, p4, p8)))
    q_h = jnp.where(gi == 0, h, jnp.where(gi == 1, q2, jnp.where(gi == 2, q4, q8)))
    s = back(p_h, 1) + q_h
    half = jnp.left_shift(1, gi)
    lo = jnp.clip(row - half, 0, t)
    hi = jnp.clip(row + half, 0, t)
    p = s / (hi - lo).astype(F32) - h
    y = _bdot(p, w_ref[...]) * sc_ref[...]
    o_ref[...] = x_ref[...] + g_ref[...] * y


def _pool_mixer(h, x, w_pool, layer, scale, gate_blk, t, ctx_blocks, t_ctx):
    m = x.shape[0]
    return pl.pallas_call(
        functools.partial(_pool_kernel, ctx_blocks=ctx_blocks, t_ctx=t_ctx),
        grid=(m // t, POOL_GROUPS),
        in_specs=[
            pl.BlockSpec((t, C_G), lambda b, g: (b, g)),
            pl.BlockSpec((t, C_G), lambda b, g: (b, g)),
            pl.BlockSpec((None, None, C_G, C_G), lambda b, g: (layer, g, 0, 0)),
            pl.BlockSpec((1, C_G), lambda b, g: (0, g)),
            pl.BlockSpec((None, 1, C_G), lambda b, g: (b, 0, g)),
        ],
        out_specs=pl.BlockSpec((t, C_G), lambda b, g: (b, g)),
        out_shape=jax.ShapeDtypeStruct((m, D_MODEL), F32),
        compiler_params=_cparams("parallel", "parallel"),
        name="pool_mixer",
    )(h, x, w_pool, scale, gate_blk)


def _pad_cols(w, n):
    return jnp.pad(w, ((0, 0), (0, n - w.shape[1])))


def _block_diag_pairs(s):
    b, two, h, n, _ = s.shape
    sp = s.reshape(b, two, h // 2, 2, n, n)
    z = jnp.zeros_like(sp[:, :, :, 0])
    top = jnp.concatenate([sp[:, :, :, 0], z], -1)
    bot = jnp.concatenate([z, sp[:, :, :, 1]], -1)
    return jnp.concatenate([top, bot], -2)


def _diag_pairs(s_bd):
    b, two, hp, _, _ = s_bd.shape
    n = HEAD_A
    return jnp.stack([s_bd[..., :n, :n], s_bd[..., n:, n:]], 3).reshape(b, two, 2 * hp, n, n)


def kernel(x_prompt, x_sample, state_rwkv, state_delta, c, c_ctx, mod_w, mod_b, norm_mix_w, norm_ffn_w, norm_final_w, ab_w_in, ab_w_out, rwkv_mu, rwkv_w0, rwkv_w2, rwkv_a0, rwkv_a2, rwkv_g2, rwkv_k_k, rwkv_k_a, rwkv_r_k, rwkv_ln_w, rwkv_ln_b, gdn_conv_w, gdn_a_log, gdn_dt_bias, gdn_norm_w, pool_w, pool_scale, ffn_w_gate, ffn_w_up, ffn_w_down):
    bp, tp, d = x_prompt.shape
    bs, ts, _ = x_sample.shape
    mp, ms = bp * tp, bs * ts
    m = mp + ms
    assert tp == ROW_BLK and ts % ROW_BLK == 0 and mp % ts == 0
    x = jnp.concatenate([x_prompt.reshape(mp, d), x_sample.reshape(ms, d)], 0)

    n_cond = 16
    cond = jnp.zeros((n_cond, d), F32).at[0].set(c_ctx).at[1:1 + bs].set(c)
    mod = _mod_all(cond, mod_w, mod_b).reshape(DEPTH, n_cond, 6, d)
    mod_blk = jnp.concatenate([
        jnp.broadcast_to(mod[:, 0:1], (DEPTH, mp // ROW_BLK, 6, d)),
        jnp.repeat(mod[:, 1:1 + bs], ts // ROW_BLK, axis=1)], 1)[:, :, :, None, :]
    zeros_blk = jnp.zeros((m // ROW_BLK, 1, d), F32)

    zero_r = jnp.zeros((bp, 2, H_A // 2, LANES, LANES), F32)
    zero_d = jnp.zeros((bp, 2, H_B, HEAD_B, HEAD_B), F32)
    new_r, new_d = [], []

    for l in range(DEPTH):
        shift_m, scale_m, gate_m, shift_f, scale_f, gate_f = (mod_blk[l, :, j] for j in range(6))
        i = l // 2
        if l % 2 == 0:
            w_in = ab_w_in[i]
            w_pa = _pad_cols(jnp.concatenate([w_in[:, :C_PA], w_in[:, C_PA + 4 * C_B:]], 1), C_PA_PAD).astype(BF16)
            w_pbm = w_in[:, C_PA:C_PA + 4 * C_B].astype(BF16)
            pa = _norm_matmul(x, norm_mix_w[l], scale_m, shift_m, w_pa)
            pbm = _norm_matmul(x, norm_mix_w[l], scale_m, shift_m, w_pbm)
            mu = _pad_cols(rwkv_mu[i][None, :], C_PA_PAD)
            pam = _token_shift(pa, mu, ts, mp // ts, tp)
            w2bd = jnp.zeros((2, 2 * LORA, C_A), F32).at[0, :LORA].set(rwkv_w2[i, 0]).at[1, LORA:].set(rwkv_w2[i, 1])
            a2bd = jnp.zeros((2, 2 * LORA, C_A), F32).at[0, :LORA].set(rwkv_a2[i, 0]).at[1, LORA:].set(rwkv_a2[i, 1])
            g2pad = jnp.pad(rwkv_g2[i], ((0, C_LO - 4 * LORA - GATE_LORA), (0, 0)))
            lw, ic, gate = _rwkv_lora(pam, w2bd.astype(BF16), a2bd.astype(BF16), g2pad.astype(BF16),
                                      rwkv_w0[i][:, None, :], rwkv_a0[i][:, None, :])
            vecs = [v_[i].reshape(1, C_A) for v_ in (rwkv_k_k, rwkv_k_a, rwkv_r_k, rwkv_ln_w, rwkv_ln_b)]
            oa_p, sr_p = _rwkv_scan(pam, lw, ic, gate, *vecs, zero_r, bp, tp, 0)
            oa_s, _ = _rwkv_scan(pam, lw, ic, gate, *vecs, _block_diag_pairs(state_rwkv[:, i]), bs, ts, mp // ts)
            gnw = gdn_norm_w[i].reshape(1, HEAD_B)
            on_decay_lanes = lambda p_: jnp.zeros((1, LANES), F32).at[
                0, BA_LANE0 + 2 * H_B:BA_LANE0 + 4 * H_B].set(p_.reshape(2 * H_B))
            alog_vec, dtb_vec = on_decay_lanes(gdn_a_log[i]), on_decay_lanes(gdn_dt_bias[i])
            ob_p, sd_p = _gdn_scan(pbm, pa, gdn_conv_w[i], alog_vec, dtb_vec, gnw, zero_d, bp, tp, 0)
            ob_s, _ = _gdn_scan(pbm, pa, gdn_conv_w[i], alog_vec, dtb_vec, gnw, state_delta[:, i],
                                bs, ts, mp // ts)
            new_r.append(_diag_pairs(sr_p))
            new_d.append(sd_p)
            o_a = jnp.concatenate([oa_p, oa_s], 0)
            o_b = jnp.concatenate([ob_p, ob_s], 0)
            x = _matmul_residual([o_a, o_b], ab_w_out, i, x, gate_m)
        else:
            h = _norm_mod(x, norm_mix_w[l], scale_m, shift_m, F32)
            gate_blk = jnp.concatenate([jnp.broadcast_to(mod[l, 0:1, 2], (mp // ts, d)), mod[l, 1:1 + bs, 2]], 0)
            x = _pool_mixer(h, x, pool_w, i, pool_scale[i].reshape(1, d), gate_blk[:, None, :], ts, mp // ts, tp)
        act = _ffn_up(x, norm_ffn_w[l], scale_f, shift_f, ffn_w_gate, ffn_w_up, l)
        x = _matmul_residual([act], ffn_w_down, l, x, gate_f, single_buffer_w=True)

    y = _norm_mod(x, norm_final_w, zeros_blk, zeros_blk, F32)
    y_prompt = y[:mp].reshape(bp, tp, d)
    y_sample = y[mp:].reshape(bs, ts, d)
    return (y_prompt, y_sample, jnp.stack(new_r, 1), jnp.stack(new_d, 1))
```

```python
import functools

import jax
import jax.numpy as jnp
from jax import lax
from jax.experimental import pallas as pl
from jax.experimental.pallas import tpu as pltpu

F32 = jnp.float32
BF16 = jnp.bfloat16

D_MODEL = 2048
DEPTH = 4
GRID_W = 64
N_AB = 2
C_A = 1024
HEAD_A = 64
H_A = 16
LORA = 64
GATE_LORA = 160
C_B = 1024
HEAD_B = 128
H_B = 8
C_PA = 3 * C_A + 4 * LORA + GATE_LORA
C_PA_PAD = 3584
C_LO = C_PA_PAD - 3 * C_A
POOL_GROUPS = 4
C_G = D_MODEL // POOL_GROUPS
D_FF = 5632
RMS_EPS = 1e-6
GN_EPS = 64e-5
CHUNK = 64
CHUNK_B = 128
ROW_BLK = 256
LANES = 128
BA_LANE0 = C_PA % LANES
VMEM_LIMIT = 48 * 1024 * 1024

NT = (((1,), (1,)), ((), ()))
TN = (((0,), (0,)), ((), ()))
NN = (((1,), (0,)), ((), ()))


def _cparams(*sem):
    return pltpu.CompilerParams(dimension_semantics=sem, vmem_limit_bytes=VMEM_LIMIT)


def _bdot(a, b, dims=NN):
    return lax.dot_general(a.astype(BF16), b.astype(BF16), dims, preferred_element_type=F32)


def _fdot(a, b, dims=NN):
    return lax.dot_general(a, b, dims, precision=lax.Precision.HIGHEST, preferred_element_type=F32)


def _sigmoid(x):
    return 0.5 + 0.5 * jnp.tanh(0.5 * x)


def _silu(x):
    return x * _sigmoid(x)


def _softplus(x):
    return jnp.maximum(x, 0.0) + jnp.log(1.0 + jnp.exp(-jnp.abs(x)))


def _mod_kernel(c_ref, w_ref, b_ref, o_ref):
    act = _silu(c_ref[...])
    o_ref[...] = _fdot(act, w_ref[...]) + b_ref[...]


def _mod_all(cond, mod_w, mod_b):
    n_c, d = cond.shape
    depth, _, n = mod_w.shape
    tn = 1024
    return pl.pallas_call(
        _mod_kernel,
        grid=(depth, n // tn),
        in_specs=[
            pl.BlockSpec((n_c, d), lambda l, j: (0, 0)),
            pl.BlockSpec((None, d, tn), lambda l, j: (l, 0, j)),
            pl.BlockSpec((None, 1, tn), lambda l, j: (l, 0, j)),
        ],
        out_specs=pl.BlockSpec((None, n_c, tn), lambda l, j: (l, 0, j)),
        out_shape=jax.ShapeDtypeStruct((depth, n_c, n), F32),
        compiler_params=_cparams("parallel", "parallel"),
        name="mod_all",
    )(cond, mod_w, mod_b.reshape(depth, 1, n))


def _norm_mod_kernel(x_ref, nw_ref, sc_ref, sh_ref, o_ref):
    x = x_ref[...]
    ms = jnp.mean(x * x, -1, keepdims=True)
    y = x * lax.rsqrt(ms + RMS_EPS) * nw_ref[...]
    o_ref[...] = (y * (1.0 + sc_ref[...]) + sh_ref[...]).astype(o_ref.dtype)


def _norm_mod(x, nw, scale, shift, out_dtype):
    m, d = x.shape
    nb = m // ROW_BLK
    return pl.pallas_call(
        _norm_mod_kernel,
        grid=(nb,),
        in_specs=[
            pl.BlockSpec((ROW_BLK, d), lambda i: (i, 0)),
            pl.BlockSpec((1, d), lambda i: (0, 0)),
            pl.BlockSpec((None, 1, d), lambda i: (i, 0, 0)),
            pl.BlockSpec((None, 1, d), lambda i: (i, 0, 0)),
        ],
        out_specs=pl.BlockSpec((ROW_BLK, d), lambda i: (i, 0)),
        out_shape=jax.ShapeDtypeStruct((m, d), out_dtype),
        compiler_params=_cparams("parallel"),
        name="norm_mod",
    )(x, nw.reshape(1, d), scale, shift)


def _norm_mod_rows(x_ref, nw_ref, sc_ref, sh_ref, s):
    x = x_ref[s * ROW_BLK:(s + 1) * ROW_BLK, :]
    y = x * lax.rsqrt(jnp.mean(x * x, -1, keepdims=True) + RMS_EPS) * nw_ref[...]
    return (y * (1.0 + sc_ref[s]) + sh_ref[s]).astype(BF16)


def _norm_specs(tm, d):
    sub = tm // ROW_BLK
    return [
        pl.BlockSpec((tm, d), lambda j, i: (i, 0)),
        pl.BlockSpec((1, d), lambda j, i: (0, 0)),
        pl.BlockSpec((sub, 1, d), lambda j, i: (i, 0, 0)),
        pl.BlockSpec((sub, 1, d), lambda j, i: (i, 0, 0)),
    ]


def _norm_mm_kernel(x_ref, nw_ref, sc_ref, sh_ref, b_ref, o_ref, *, tm):
    for s in range(tm // ROW_BLK):
        h = _norm_mod_rows(x_ref, nw_ref, sc_ref, sh_ref, s)
        o_ref[s * ROW_BLK:(s + 1) * ROW_BLK, :] = jnp.dot(h, b_ref[...], preferred_element_type=F32)


def _norm_matmul(x, nw, scale, shift, b, tm=512, tn=512):
    m, d = x.shape
    n = b.shape[1]
    return pl.pallas_call(
        functools.partial(_norm_mm_kernel, tm=tm),
        grid=(n // tn, m // tm),
        in_specs=_norm_specs(tm, d) + [pl.BlockSpec((d, tn), lambda j, i: (0, j))],
        out_specs=pl.BlockSpec((tm, tn), lambda j, i: (i, j)),
        out_shape=jax.ShapeDtypeStruct((m, n), F32),
        compiler_params=_cparams("parallel", "parallel"),
        name="norm_matmul",
    )(x, nw.reshape(1, d), scale, shift, b)


def _gated_residual(x_ref, g_ref, acc, o_ref, tm):
    for s in range(tm // ROW_BLK):
        rows = slice(s * ROW_BLK, (s + 1) * ROW_BLK)
        o_ref[rows, :] = x_ref[rows, :] + g_ref[s] * acc[rows, :]


def _cast_weight_tiles(w_refs, wb_refs):
    @pl.when(pl.program_id(1) == 0)
    def _():
        for w_ref, wb_ref in zip(w_refs, wb_refs):
            wb_ref[...] = w_ref[...].astype(BF16)


def _mm_res_kernel(*refs, n_in, tm):
    a_refs, w_refs = refs[:n_in], refs[n_in:2 * n_in]
    x_ref, g_ref, o_ref = refs[2 * n_in:2 * n_in + 3]
    wb_refs = refs[2 * n_in + 3:]
    _cast_weight_tiles(w_refs, wb_refs)
    acc = jnp.dot(a_refs[0][...], wb_refs[0][...], preferred_element_type=F32)
    for a_ref, wb_ref in zip(a_refs[1:], wb_refs[1:]):
        acc = acc + jnp.dot(a_ref[...], wb_ref[...], preferred_element_type=F32)
    _gated_residual(x_ref, g_ref, acc, o_ref, tm)


def _matmul_residual(a_list, w, layer, x, gate, tm=512, tn=512, single_buffer_w=False):
    m, n = x.shape
    sub = tm // ROW_BLK
    kb = w.shape[1] // len(a_list)
    w_mode = dict(pipeline_mode=pl.Buffered(1)) if single_buffer_w else {}
    in_specs = [pl.BlockSpec((tm, kb), lambda j, i: (i, 0)) for _ in a_list]
    in_specs += [pl.BlockSpec((None, kb, tn), functools.partial(lambda j, i, r: (layer, r, j), r=r), **w_mode)
                 for r in range(len(a_list))]
    in_specs += [
        pl.BlockSpec((tm, tn), lambda j, i: (i, j)),
        pl.BlockSpec((sub, 1, tn), lambda j, i: (i, 0, j)),
    ]
    return pl.pallas_call(
        functools.partial(_mm_res_kernel, n_in=len(a_list), tm=tm),
        grid=(n // tn, m // tm),
        in_specs=in_specs,
        out_specs=pl.BlockSpec((tm, tn), lambda j, i: (i, j)),
        out_shape=jax.ShapeDtypeStruct((m, n), F32),
        scratch_shapes=[pltpu.VMEM((kb, tn), BF16) for _ in a_list],
        compiler_params=_cparams("parallel", "arbitrary"),
        name="matmul_residual",
    )(*a_list, *([w] * len(a_list)), x, gate)


def _ffn_up_kernel(x_ref, nw_ref, sc_ref, sh_ref, wg_ref, wu_ref, o_ref, wgb_ref, wub_ref, *, tm):
    _cast_weight_tiles((wg_ref, wu_ref), (wgb_ref, wub_ref))
    for s in range(tm // ROW_BLK):
        h = _norm_mod_rows(x_ref, nw_ref, sc_ref, sh_ref, s)
        g = jnp.dot(h, wgb_ref[...], preferred_element_type=F32)
        u = jnp.dot(h, wub_ref[...], preferred_element_type=F32)
        o_ref[s * ROW_BLK:(s + 1) * ROW_BLK, :] = (_silu(g) * u).astype(o_ref.dtype)


def _ffn_up(x, nw, scale, shift, wg, wu, layer, tm=512, tn=512):
    m, d = x.shape
    n = wg.shape[2]
    w_spec = lambda: pl.BlockSpec((None, d, tn), lambda j, i: (layer, 0, j))
    return pl.pallas_call(
        functools.partial(_ffn_up_kernel, tm=tm),
        grid=(n // tn, m // tm),
        in_specs=_norm_specs(tm, d) + [w_spec(), w_spec()],
        out_specs=pl.BlockSpec((tm, tn), lambda j, i: (i, j)),
        out_shape=jax.ShapeDtypeStruct((m, n), BF16),
        scratch_shapes=[pltpu.VMEM((d, tn), BF16)] * 2,
        compiler_params=_cparams("parallel", "arbitrary"),
        name="ffn_up",
    )(x, nw.reshape(1, d), scale, shift, wg, wu)


def _shift_rows(x, shift, valid):
    t = x.shape[0]
    return jnp.where(valid, pltpu.roll(x, shift % t, 0), 0.0)


def _token_shift_kernel(p_ref, mu_ref, o_ref, *, ctx_blocks, t_ctx):
    x = p_ref[...]
    t, c = x.shape
    row = lax.broadcasted_iota(jnp.int32, (t, c), 0)
    lane = lax.broadcasted_iota(jnp.int32, (t, c), 1)
    mu = mu_ref[...]

    @pl.when(pl.program_id(0) < ctx_blocks)
    def _():
        pos = row % t_ctx
        s = jnp.where(lane % 2 == 0, _shift_rows(x, 1, pos != 0), _shift_rows(x, -1, pos != t_ctx - 1))
        o_ref[...] = x + (s - x) * mu

    @pl.when(pl.program_id(0) >= ctx_blocks)
    def _():
        col = row % GRID_W
        sel = lane % 4
        s = jnp.where(sel == 0, _shift_rows(x, 1, col != 0),
            jnp.where(sel == 1, _shift_rows(x, -1, col != GRID_W - 1),
            jnp.where(sel == 2, _shift_rows(x, GRID_W, row >= GRID_W),
                      _shift_rows(x, -GRID_W, row < t - GRID_W))))
        o_ref[...] = x + (s - x) * mu


def _token_shift(pa, mu, t, ctx_blocks, t_ctx):
    tc = 512
    m = pa.shape[0]
    return pl.pallas_call(
        functools.partial(_token_shift_kernel, ctx_blocks=ctx_blocks, t_ctx=t_ctx),
        grid=(m // t, C_PA_PAD // tc),
        in_specs=[
            pl.BlockSpec((t, tc), lambda b, j: (b, j)),
            pl.BlockSpec((1, tc), lambda b, j: (0, j)),
        ],
        out_specs=pl.BlockSpec((t, tc), lambda b, j: (b, j)),
        out_shape=jax.ShapeDtypeStruct((m, C_PA_PAD), F32),
        compiler_params=_cparams("parallel", "parallel"),
        name="token_shift",
    )(pa, mu)


def _rwkv_lora_kernel(lo_ref, w2_ref, a2_ref, g2_ref, w0_ref, a0_ref, lw_ref, ic_ref, gt_ref):
    lo = lo_ref[...]
    w_in = jnp.tanh(lo[:, 0:2 * LORA])
    a_in = lo[:, 2 * LORA:4 * LORA]
    g_in = _sigmoid(lo[:, 4 * LORA:])
    for d in range(2):
        w_log = -_softplus(-(w0_ref[d] + _bdot(w_in, w2_ref[d]))) - 0.5
        lw_ref[d] = -jnp.exp(w_log)
        ic_ref[d] = _sigmoid(a0_ref[d] + _bdot(a_in, a2_ref[d]))
    gt_ref[...] = _bdot(g_in, g2_ref[...])


def _rwkv_lora(pam, w2bd, a2bd, g2pad, w0, a0, tm=512):
    m = pam.shape[0]
    lo_blk = 3 * C_A // C_LO
    return pl.pallas_call(
        _rwkv_lora_kernel,
        grid=(m // tm,),
        in_specs=[
            pl.BlockSpec((tm, C_LO), lambda i: (i, lo_blk)),
            pl.BlockSpec((2, 2 * LORA, C_A), lambda i: (0, 0, 0)),
            pl.BlockSpec((2, 2 * LORA, C_A), lambda i: (0, 0, 0)),
            pl.BlockSpec((C_LO - 4 * LORA, C_A), lambda i: (0, 0)),
            pl.BlockSpec((2, 1, C_A), lambda i: (0, 0, 0)),
            pl.BlockSpec((2, 1, C_A), lambda i: (0, 0, 0)),
        ],
        out_specs=[
            pl.BlockSpec((2, tm, C_A), lambda i: (0, i, 0)),
            pl.BlockSpec((2, tm, C_A), lambda i: (0, i, 0)),
            pl.BlockSpec((tm, C_A), lambda i: (i, 0)),
        ],
        out_shape=[
            jax.ShapeDtypeStruct((2, m, C_A), F32),
            jax.ShapeDtypeStruct((2, m, C_A), F32),
            jax.ShapeDtypeStruct((m, C_A), F32),
        ],
        compiler_params=_cparams("parallel"),
        name="rwkv_lora",
    )(pam, w2bd, a2bd, g2pad, w0, a0)


def _order_masks(n, reverse):
    rr = lax.broadcasted_iota(jnp.int32, (n, n), 0)
    cc = lax.broadcasted_iota(jnp.int32, (n, n), 1)
    if reverse:
        return cc >= rr, cc > rr
    return cc <= rr, cc < rr


def _run_lockstep(chains):
    results = [None] * len(chains)
    pending = {i: next(c) for i, c in enumerate(chains)}
    while pending:
        products = {i: fn(a, b, dims) for i, (fn, a, b, dims) in pending.items()}
        pending = {}
        for i, prod in products.items():
            try:
                pending[i] = chains[i].send(prod)
            except StopIteration as stop:
                results[i] = stop.value
    return results


def _triangular_inverse(l, order):
    n = l.shape[0]
    ri = lax.broadcasted_iota(jnp.int32, (n, n), 0)
    ci = lax.broadcasted_iota(jnp.int32, (n, n), 1)

    def joins(k):
        return (ri // (2 * k) == ci // (2 * k)) & ((ri // k) != (ci // k))

    x = (ri == ci).astype(F32) + jnp.where(joins(1), l, 0.0)
    k = 2
    while k < order:
        xe = yield _bdot, x, jnp.where(joins(k), l, 0.0), NN
        x = x + (yield _bdot, xe, x, NN)
        k *= 2
    return x


def _split_heads(x, first):
    return jnp.concatenate([jnp.where(first, x, 0.0), jnp.where(first, 0.0, x)], 0)


def _chunk_cumsum(x, reverse, chunk):
    t = x.shape[0]
    pos = lax.broadcasted_iota(jnp.int32, x.shape, 0) % chunk
    s = 1
    while s < chunk:
        if reverse:
            x = x + jnp.where(pos < chunk - s, pltpu.roll(x, t - s, 0), 0.0)
        else:
            x = x + jnp.where(pos >= s, pltpu.roll(x, s, 0), 0.0)
        s *= 2
    return x


def _rwkv_chunk(s_bd, r, lw, cum, k, v, a, b, reverse):
    c = CHUNK
    cum_ex = cum - lw
    tot = cum[0:1] if reverse else cum[c - 1:c]
    e_neg = jnp.exp(-cum)
    e_hat = jnp.exp(tot - cum)
    at = a * jnp.exp(cum_ex)
    rt = r * jnp.exp(cum)
    first = lax.broadcasted_iota(jnp.int32, (c, LANES), 1) < HEAD_A
    lhs = jnp.concatenate([at, rt], 0)
    rbk = jnp.concatenate([_split_heads(b * e_neg, first), _split_heads(k * e_neg, first)], 0)
    g = yield _bdot, lhs, rbk, NT
    ri_g = lax.broadcasted_iota(jnp.int32, (2 * c, 4 * c), 0)
    pos_c = lax.broadcasted_iota(jnp.int32, (2 * c, 4 * c), 1) % c
    pos_r = ri_g % c
    own = jnp.where(ri_g < c, 0, 1)
    keep = (pos_c > pos_r - own) if reverse else (pos_c < pos_r + own)
    g = jnp.where(keep, g, 0.0)
    l_ab = g[0:c, 0:2 * c]
    l_ak = g[0:c, 2 * c:4 * c]
    m_r = g[c:2 * c, :]
    v_bd = _split_heads(v, first)
    sa = yield _bdot, lhs, s_bd, NT
    rhs = sa[0:c] + (yield _bdot, l_ak, v_bd, NN)
    l_bd = _split_heads(l_ab, first)
    t_inv = yield from _triangular_inverse(l_bd, c)
    u_bd = yield _bdot, t_inv, _split_heads(rhs, first), NN
    y = sa[c:2 * c] + (yield _bdot, m_r, jnp.concatenate([u_bd, v_bd], 0), NN)
    u = u_bd[0:c] + u_bd[c:2 * c]
    upd = yield _bdot, jnp.concatenate([u, v], 0), jnp.concatenate([b * e_hat, k * e_hat], 0), TN
    ri = lax.broadcasted_iota(jnp.int32, (LANES, LANES), 0) < HEAD_A
    ci = lax.broadcasted_iota(jnp.int32, (LANES, LANES), 1) < HEAD_A
    s_new = s_bd * jnp.exp(tot) + jnp.where(ri == ci, upd, 0.0)
    return y, s_new


def _head_sum(x, first):
    s0 = jnp.sum(jnp.where(first, x, 0.0), -1, keepdims=True)
    s1 = jnp.sum(jnp.where(first, 0.0, x), -1, keepdims=True)
    return jnp.where(first, s0, s1)


def _rwkv_scan_kernel(r_ref, k_ref, v_ref, lw_ref, ic_ref, gt_ref, kk_ref, ka_ref, rk_ref, lnw_ref, lnb_ref,
                      s0_ref, o_ref, sf_ref, acc_ref, cum_ref, *, t, pairs):
    n = t // CHUNK
    first = lax.broadcasted_iota(jnp.int32, (CHUNK, LANES), 1) < HEAD_A
    for d in range(2):
        cum_ref[d] = _chunk_cumsum(lw_ref[d], d == 1, CHUNK)
    sf_ref[...] = s0_ref[...]

    def body(i, carry):
        where, chains = [], []
        for d in range(2):
            reverse = d == 1
            ci = (n - 1 - i) if reverse else i
            rows = pl.ds(pl.multiple_of(ci * CHUNK, CHUNK), CHUNK)
            for p in range(pairs):
                lanes = slice(p * LANES, (p + 1) * LANES)
                r = r_ref[rows, lanes]
                k = k_ref[rows, lanes]
                v = v_ref[rows, lanes]
                iclr = ic_ref[d, rows, lanes]
                kk = k * kk_ref[:, lanes]
                kk = kk * lax.rsqrt(_head_sum(kk * kk, first) + 1e-12)
                k_d = k * (1.0 + (iclr - 1.0) * ka_ref[:, lanes])
                where.append((d, p, rows, lanes, r, k_d, v))
                chains.append(_rwkv_chunk(sf_ref[d, p], r, lw_ref[d, rows, lanes], cum_ref[d, rows, lanes],
                                          k_d, v, -kk, kk * iclr, reverse))
        for (d, p, rows, lanes, r, k_d, v), (y, s_new) in zip(where, _run_lockstep(chains)):
            sf_ref[d, p] = s_new
            mu = _head_sum(y, first) * (1.0 / HEAD_A)
            yc = y - mu
            var = _head_sum(yc * yc, first) * (1.0 / HEAD_A)
            out = yc * lax.rsqrt(var + GN_EPS) * lnw_ref[:, lanes] + lnb_ref[:, lanes]
            acc_ref[d, rows, lanes] = out + _head_sum(r * k_d * rk_ref[:, lanes], first) * v
        return carry

    lax.fori_loop(0, n, body, 0)
    o_ref[...] = ((acc_ref[0] + acc_ref[1]) * gt_ref[...]).astype(o_ref.dtype)


def _rwkv_scan(pam, lw, ic, gate, k_k, k_a, r_k, ln_w, ln_b, s0_bd, n_seq, t, row_blk0, pairs=4):
    w = pairs * LANES
    nb = C_A // w
    vec = lambda: pl.BlockSpec((1, w), lambda b, h: (0, h))
    rows = lambda off: pl.BlockSpec((t, w), lambda b, h: (row_blk0 + b, off + h))
    rows2 = lambda: pl.BlockSpec((2, t, w), lambda b, h: (0, row_blk0 + b, h))
    state = pl.BlockSpec((None, 2, pairs, LANES, LANES), lambda b, h: (b, 0, h, 0, 0))
    return pl.pallas_call(
        functools.partial(_rwkv_scan_kernel, t=t, pairs=pairs),
        grid=(n_seq, nb),
        in_specs=[
            rows(0), rows(nb), rows(2 * nb), rows2(), rows2(), rows(0),
            vec(), vec(), vec(), vec(), vec(),
            state,
        ],
        out_specs=[pl.BlockSpec((t, w), lambda b, h: (b, h)), state],
        out_shape=[
            jax.ShapeDtypeStruct((n_seq * t, C_A), BF16),
            jax.ShapeDtypeStruct((n_seq, 2, C_A // LANES, LANES, LANES), F32),
        ],
        scratch_shapes=[pltpu.VMEM((2, t, w), F32), pltpu.VMEM((2, t, w), F32)],
        compiler_params=_cparams("parallel", "parallel"),
        name="rwkv_scan",
    )(pam, pam, pam, lw, ic, gate, k_k, k_a, r_k, ln_w, ln_b, s0_bd)


def _gdn_chunk(s, q, k, v, gc, beta, reverse):
    c = q.shape[0]
    incl, strict = _order_masks(c, reverse)
    pick = (lax.broadcasted_iota(jnp.int32, (8, LANES), 1) == 0).astype(F32)
    gr = (yield _fdot, pick, jnp.broadcast_to(gc, (c, LANES)), NT)[0:1, :]
    decay = jnp.where(incl, jnp.exp(jnp.where(incl, gc - gr, 0.0)), 0.0)
    g_last = gc[0:1] if reverse else gc[c - 1:c]
    kb = k * beta
    l = jnp.where(strict, (yield _bdot, kb, k, NT) * decay, 0.0)
    attn = (yield _bdot, q, k, NT) * decay
    a_inv = yield from _triangular_inverse(-l, c)
    eg = jnp.exp(gc)
    sol = yield _bdot, a_inv, jnp.concatenate([v * beta, kb * eg], -1), NN
    u, w = sol[:, 0:HEAD_B], sol[:, HEAD_B:]
    ws = yield _bdot, jnp.concatenate([w, q * eg], 0), s, NN
    v_new = u - ws[0:c]
    o = ws[c:2 * c] + (yield _bdot, attn, v_new, NN)
    s_new = s * jnp.exp(g_last) + (yield _bdot, k * jnp.exp(g_last - gc), v_new, TN)
    return o, s_new


def _gdn_scan_kernel(q_ref, k_ref, v_ref, z_ref, ba_ref, cq_ref, ck_ref, cv_ref, gnw_ref, alog_ref, dtb_ref,
                     s0_ref, o_ref, sf_ref, qs_ref, ks_ref, vs_ref, acc_ref, gc_ref, beta_ref, *, t, heads):
    n = t // CHUNK_B
    hg = pl.program_id(1)
    blk = CHUNK_B
    brow = lax.broadcasted_iota(jnp.int32, (blk, LANES), 0)
    no_row = jnp.zeros((1, LANES), F32)

    def conv_silu(x_ref, w, r0, lanes):
        x = x_ref[r0:r0 + blk, lanes]
        before = x_ref[r0 - 1:r0, lanes] if r0 > 0 else no_row
        after = x_ref[r0 + blk:r0 + blk + 1, lanes] if r0 + blk < t else no_row
        prev = jnp.where(brow == 0, before, pltpu.roll(x, 1, 0))
        nxt = jnp.where(brow == blk - 1, after, pltpu.roll(x, blk - 1, 0))
        return _silu(w[0:1] * prev + w[1:2] * x + w[2:3] * nxt)

    def l2norm(x):
        return x * lax.rsqrt(jnp.sum(x * x, -1, keepdims=True) + 1e-6)

    for g in range(heads):
        lanes = slice(g * LANES, (g + 1) * LANES)
        wq, wk, wv = cq_ref[:, lanes], ck_ref[:, lanes], cv_ref[:, lanes]
        for r0 in range(0, t, blk):
            rows = slice(r0, r0 + blk)
            qs_ref[rows, lanes] = l2norm(conv_silu(q_ref, wq, r0, lanes)) * (HEAD_B ** -0.5)
            ks_ref[rows, lanes] = l2norm(conv_silu(k_ref, wk, r0, lanes))
            vs_ref[rows, lanes] = conv_silu(v_ref, wv, r0, lanes)
    ba = ba_ref[...]
    beta_ref[...] = _sigmoid(ba)
    g_log = -jnp.exp(alog_ref[...]) * _softplus(ba + dtb_ref[...])
    gc_ref[0] = _chunk_cumsum(g_log, False, CHUNK_B)
    gc_ref[1] = _chunk_cumsum(g_log, True, CHUNK_B)
    sf_ref[...] = s0_ref[...]
    lane = lax.broadcasted_iota(jnp.int32, (CHUNK_B, LANES), 1)

    def body(i, carry):
        chains = []
        for d in range(2):
            ci = (n - 1 - i) if d == 1 else i
            rows = pl.ds(pl.multiple_of(ci * CHUNK_B, CHUNK_B), CHUNK_B)
            gcs = gc_ref[d, rows, :]
            bts = beta_ref[rows, :]
            for g in range(heads):
                lanes = slice(g * LANES, (g + 1) * LANES)
                chains.append((d, g, rows, lanes, gcs, bts, sf_ref[d, g],
                               qs_ref[rows, lanes], ks_ref[rows, lanes], vs_ref[rows, lanes]))
        gens = []
        for d, g, rows, lanes, gcs, bts, s, q, k, v in chains:
            col = d * H_B + hg * heads + g
            gc = jnp.sum(jnp.where(lane == BA_LANE0 + 2 * H_B + col, gcs, 0.0), -1, keepdims=True)
            beta = jnp.sum(jnp.where(lane == BA_LANE0 + col, bts, 0.0), -1, keepdims=True)
            gens.append(_gdn_chunk(s, q, k, v, gc, beta, d == 1))
        for (d, g, rows, lanes, *_), (o, s_new) in zip(chains, _run_lockstep(gens)):
            sf_ref[d, g] = s_new
            acc_ref[d, rows, lanes] = o
        return carry

    lax.fori_loop(0, n, body, 0)
    for g in range(heads):
        lanes = slice(g * LANES, (g + 1) * LANES)
        for r0 in range(0, t, blk):
            rows = slice(r0, r0 + blk)
            o = acc_ref[0, rows, lanes] + acc_ref[1, rows, lanes]
            o = o * lax.rsqrt(jnp.mean(o * o, -1, keepdims=True) + RMS_EPS) * gnw_ref[...]
            o_ref[rows, lanes] = (o * _silu(z_ref[rows, lanes])).astype(o_ref.dtype)


def _gdn_scan(pbm, pa, conv_w, alog_vec, dtb_vec, gn_w, s0, n_seq, t, row_blk0, heads=4):
    w = heads * LANES
    nb = C_B // w
    rows = lambda off: pl.BlockSpec((t, w), lambda b, h: (row_blk0 + b, off + h))
    cw = lambda off: pl.BlockSpec((3, w), lambda b, h: (0, off + h))
    vec = pl.BlockSpec((1, LANES), lambda b, h: (0, 0))
    state = pl.BlockSpec((None, 2, heads, HEAD_B, HEAD_B), lambda b, h: (b, 0, h, 0, 0))
    return pl.pallas_call(
        functools.partial(_gdn_scan_kernel, t=t, heads=heads),
        grid=(n_seq, nb),
        in_specs=[
            rows(0), rows(nb), rows(2 * nb), rows(3 * nb),
            pl.BlockSpec((t, LANES), lambda b, h: (row_blk0 + b, C_PA // LANES)),
            cw(0), cw(nb), cw(2 * nb),
            vec, vec, vec,
            state,
        ],
        out_specs=[pl.BlockSpec((t, w), lambda b, h: (b, h)), state],
        out_shape=[
            jax.ShapeDtypeStruct((n_seq * t, C_B), BF16),
            jax.ShapeDtypeStruct((n_seq, 2, H_B, HEAD_B, HEAD_B), F32),
        ],
        scratch_shapes=[pltpu.VMEM((t, w), F32)] * 3 + [pltpu.VMEM((2, t, w), F32), pltpu.VMEM((2, t, LANES), F32),
                                                         pltpu.VMEM((t, LANES), F32)],
        compiler_params=_cparams("parallel", "parallel"),
        name="gdn_scan",
    )(pbm, pbm, pbm, pbm, pa, conv_w, conv_w, conv_w, gn_w, alog_vec, dtb_vec, s0)


def _pool_kernel(h_ref, x_ref, w_ref, sc_ref, g_ref, o_ref, *, ctx_blocks, t_ctx):
    gi = pl.program_id(1)
    h = h_ref[...]
    t = jnp.where(pl.program_id(0) < ctx_blocks, t_ctx, h.shape[0])
    row = lax.broadcasted_iota(jnp.int32, h.shape, 0) & (t - 1)

    def back(x, n):
        return _shift_rows(x, n, row >= n)

    def fwd(x, n):
        return _shift_rows(x, -n, row < t - n)

    p2 = back(h, 1) + h
    q2 = h + fwd(h, 1)
    p4 = back(p2, 2) + p2
    q4 = q2 + fwd(q2, 2)
    p8 = back(p4, 4) + p4
    q8 = q4 + fwd(q4, 4)
    p_h = jnp.where(gi == 0, h, jnp.where(gi == 1, p2, jnp.where(gi == 2, p4, p8)))
    q_h = jnp.where(gi == 0, h, jnp.where(gi == 1, q2, jnp.where(gi == 2, q4, q8)))
    s = back(p_h, 1) + q_h
    half = jnp.left_shift(1, gi)
    lo = jnp.clip(row - half, 0, t)
    hi = jnp.clip(row + half, 0, t)
    p = s / (hi - lo).astype(F32) - h
    y = _bdot(p, w_ref[...]) * sc_ref[...]
    o_ref[...] = x_ref[...] + g_ref[...] * y


def _pool_mixer(h, x, w_pool, layer, scale, gate_blk, t, ctx_blocks, t_ctx):
    m = x.shape[0]
    return pl.pallas_call(
        functools.partial(_pool_kernel, ctx_blocks=ctx_blocks, t_ctx=t_ctx),
        grid=(m // t, POOL_GROUPS),
        in_specs=[
            pl.BlockSpec((t, C_G), lambda b, g: (b, g)),
            pl.BlockSpec((t, C_G), lambda b, g: (b, g)),
            pl.BlockSpec((None, None, C_G, C_G), lambda b, g: (layer, g, 0, 0)),
            pl.BlockSpec((1, C_G), lambda b, g: (0, g)),
            pl.BlockSpec((None, 1, C_G), lambda b, g: (b, 0, g)),
        ],
        out_specs=pl.BlockSpec((t, C_G), lambda b, g: (b, g)),
        out_shape=jax.ShapeDtypeStruct((m, D_MODEL), F32),
        compiler_params=_cparams("parallel", "parallel"),
        name="pool_mixer",
    )(h, x, w_pool, scale, gate_blk)


def _pad_cols(w, n):
    return jnp.pad(w, ((0, 0), (0, n - w.shape[1])))


def _block_diag_pairs(s):
    b, two, h, n, _ = s.shape
    sp = s.reshape(b, two, h // 2, 2, n, n)
    z = jnp.zeros_like(sp[:, :, :, 0])
    top = jnp.concatenate([sp[:, :, :, 0], z], -1)
    bot = jnp.concatenate([z, sp[:, :, :, 1]], -1)
    return jnp.concatenate([top, bot], -2)


def _diag_pairs(s_bd):
    b, two, hp, _, _ = s_bd.shape
    n = HEAD_A
    return jnp.stack([s_bd[..., :n, :n], s_bd[..., n:, n:]], 3).reshape(b, two, 2 * hp, n, n)


def kernel(x_prompt, x_sample, state_rwkv, state_delta, c, c_ctx, mod_w, mod_b, norm_mix_w, norm_ffn_w, norm_final_w, ab_w_in, ab_w_out, rwkv_mu, rwkv_w0, rwkv_w2, rwkv_a0, rwkv_a2, rwkv_g2, rwkv_k_k, rwkv_k_a, rwkv_r_k, rwkv_ln_w, rwkv_ln_b, gdn_conv_w, gdn_a_log, gdn_dt_bias, gdn_norm_w, pool_w, pool_scale, ffn_w_gate, ffn_w_up, ffn_w_down):
    bp, tp, d = x_prompt.shape
    bs, ts, _ = x_sample.shape
    mp, ms = bp * tp, bs * ts
    m = mp + ms
    assert tp == ROW_BLK and ts % ROW_BLK == 0 and mp % ts == 0
    x = jnp.concatenate([x_prompt.reshape(mp, d), x_sample.reshape(ms, d)], 0)

    n_cond = 16
    cond = jnp.zeros((n_cond, d), F32).at[0].set(c_ctx).at[1:1 + bs].set(c)
    mod = _mod_all(cond, mod_w, mod_b).reshape(DEPTH, n_cond, 6, d)
    mod_blk = jnp.concatenate([
        jnp.broadcast_to(mod[:, 0:1], (DEPTH, mp // ROW_BLK, 6, d)),
        jnp.repeat(mod[:, 1:1 + bs], ts // ROW_BLK, axis=1)], 1)[:, :, :, None, :]
    zeros_blk = jnp.zeros((m // ROW_BLK, 1, d), F32)

    zero_r = jnp.zeros((bp, 2, H_A // 2, LANES, LANES), F32)
    zero_d = jnp.zeros((bp, 2, H_B, HEAD_B, HEAD_B), F32)
    new_r, new_d = [], []

    for l in range(DEPTH):
        shift_m, scale_m, gate_m, shift_f, scale_f, gate_f = (mod_blk[l, :, j] for j in range(6))
        i = l // 2
        if l % 2 == 0:
            w_in = ab_w_in[i]
            w_pa = _pad_cols(jnp.concatenate([w_in[:, :C_PA], w_in[:, C_PA + 4 * C_B:]], 1), C_PA_PAD).astype(BF16)
            w_pbm = w_in[:, C_PA:C_PA + 4 * C_B].astype(BF16)
            pa = _norm_matmul(x, norm_mix_w[l], scale_m, shift_m, w_pa)
            pbm = _norm_matmul(x, norm_mix_w[l], scale_m, shift_m, w_pbm)
            mu = _pad_cols(rwkv_mu[i][None, :], C_PA_PAD)
            pam = _token_shift(pa, mu, ts, mp // ts, tp)
            w2bd = jnp.zeros((2, 2 * LORA, C_A), F32).at[0, :LORA].set(rwkv_w2[i, 0]).at[1, LORA:].set(rwkv_w2[i, 1])
            a2bd = jnp.zeros((2, 2 * LORA, C_A), F32).at[0, :LORA].set(rwkv_a2[i, 0]).at[1, LORA:].set(rwkv_a2[i, 1])
            g2pad = jnp.pad(rwkv_g2[i], ((0, C_LO - 4 * LORA - GATE_LORA), (0, 0)))
            lw, ic, gate = _rwkv_lora(pam, w2bd.astype(BF16), a2bd.astype(BF16), g2pad.astype(BF16),
                                      rwkv_w0[i][:, None, :], rwkv_a0[i][:, None, :])
            vecs = [v_[i].reshape(1, C_A) for v_ in (rwkv_k_k, rwkv_k_a, rwkv_r_k, rwkv_ln_w, rwkv_ln_b)]
            oa_p, sr_p = _rwkv_scan(pam, lw, ic, gate, *vecs, zero_r, bp, tp, 0)
            oa_s, _ = _rwkv_scan(pam, lw, ic, gate, *vecs, _block_diag_pairs(state_rwkv[:, i]), bs, ts, mp // ts)
            gnw = gdn_norm_w[i].reshape(1, HEAD_B)
            on_decay_lanes = lambda p_: jnp.zeros((1, LANES), F32).at[
                0, BA_LANE0 + 2 * H_B:BA_LANE0 + 4 * H_B].set(p_.reshape(2 * H_B))
            alog_vec, dtb_vec = on_decay_lanes(gdn_a_log[i]), on_decay_lanes(gdn_dt_bias[i])
            ob_p, sd_p = _gdn_scan(pbm, pa, gdn_conv_w[i], alog_vec, dtb_vec, gnw, zero_d, bp, tp, 0)
            ob_s, _ = _gdn_scan(pbm, pa, gdn_conv_w[i], alog_vec, dtb_vec, gnw, state_delta[:, i],
                                bs, ts, mp // ts)
            new_r.append(_diag_pairs(sr_p))
            new_d.append(sd_p)
            o_a = jnp.concatenate([oa_p, oa_s], 0)
            o_b = jnp.concatenate([ob_p, ob_s], 0)
            x = _matmul_residual([o_a, o_b], ab_w_out, i, x, gate_m)
        else:
            h = _norm_mod(x, norm_mix_w[l], scale_m, shift_m, F32)
            gate_blk = jnp.concatenate([jnp.broadcast_to(mod[l, 0:1, 2], (mp // ts, d)), mod[l, 1:1 + bs, 2]], 0)
            x = _pool_mixer(h, x, pool_w, i, pool_scale[i].reshape(1, d), gate_blk[:, None, :], ts, mp // ts, tp)
        act = _ffn_up(x, norm_ffn_w[l], scale_f, shift_f, ffn_w_gate, ffn_w_up, l)
        x = _matmul_residual([act], ffn_w_down, l, x, gate_f, single_buffer_w=True)

    y = _norm_mod(x, norm_final_w, zeros_blk, zeros_blk, F32)
    y_prompt = y[:mp].reshape(bp, tp, d)
    y_sample = y[mp:].reshape(bs, ts, d)
    return (y_prompt, y_sample, jnp.stack(new_r, 1), jnp.stack(new_d, 1))
```

```python
import functools

import jax
import jax.numpy as jnp
from jax import lax
from jax.experimental import pallas as pl
from jax.experimental.pallas import tpu as pltpu

F32 = jnp.float32
BF16 = jnp.bfloat16

D_MODEL = 2048
DEPTH = 4
GRID_W = 64
N_AB = 2
C_A = 1024
HEAD_A = 64
H_A = 16
LORA = 64
GATE_LORA = 160
C_B = 1024
HEAD_B = 128
H_B = 8
C_PA = 3 * C_A + 4 * LORA + GATE_LORA
C_PA_PAD = 3584
C_LO = C_PA_PAD - 3 * C_A
POOL_GROUPS = 4
C_G = D_MODEL // POOL_GROUPS
D_FF = 5632
RMS_EPS = 1e-6
GN_EPS = 64e-5
CHUNK = 64
CHUNK_B = 128
ROW_BLK = 256
LANES = 128
BA_LANE0 = C_PA % LANES
VMEM_LIMIT = 48 * 1024 * 1024

NT = (((1,), (1,)), ((), ()))
TN = (((0,), (0,)), ((), ()))
NN = (((1,), (0,)), ((), ()))


def _cparams(*sem):
    return pltpu.CompilerParams(dimension_semantics=sem, vmem_limit_bytes=VMEM_LIMIT)


def _bdot(a, b, dims=NN):
    return lax.dot_general(a.astype(BF16), b.astype(BF16), dims, preferred_element_type=F32)


def _fdot(a, b, dims=NN):
    return lax.dot_general(a, b, dims, precision=lax.Precision.HIGHEST, preferred_element_type=F32)


def _sigmoid(x):
    return 0.5 + 0.5 * jnp.tanh(0.5 * x)


def _silu(x):
    return x * _sigmoid(x)


def _softplus(x):
    return jnp.maximum(x, 0.0) + jnp.log(1.0 + jnp.exp(-jnp.abs(x)))


def _mod_kernel(c_ref, w_ref, b_ref, o_ref):
    act = _silu(c_ref[...])
    o_ref[...] = _fdot(act, w_ref[...]) + b_ref[...]


def _mod_all(cond, mod_w, mod_b):
    n_c, d = cond.shape
    depth, _, n = mod_w.shape
    tn = 1024
    return pl.pallas_call(
        _mod_kernel,
        grid=(depth, n // tn),
        in_specs=[
            pl.BlockSpec((n_c, d), lambda l, j: (0, 0)),
            pl.BlockSpec((None, d, tn), lambda l, j: (l, 0, j)),
            pl.BlockSpec((None, 1, tn), lambda l, j: (l, 0, j)),
        ],
        out_specs=pl.BlockSpec((None, n_c, tn), lambda l, j: (l, 0, j)),
        out_shape=jax.ShapeDtypeStruct((depth, n_c, n), F32),
        compiler_params=_cparams("parallel", "parallel"),
        name="mod_all",
    )(cond, mod_w, mod_b.reshape(depth, 1, n))


def _norm_mod_kernel(x_ref, nw_ref, sc_ref, sh_ref, o_ref):
    x = x_ref[...]
    ms = jnp.mean(x * x, -1, keepdims=True)
    y = x * lax.rsqrt(ms + RMS_EPS) * nw_ref[...]
    o_ref[...] = (y * (1.0 + sc_ref[...]) + sh_ref[...]).astype(o_ref.dtype)


def _norm_mod(x, nw, scale, shift, out_dtype):
    m, d = x.shape
    nb = m // ROW_BLK
    return pl.pallas_call(
        _norm_mod_kernel,
        grid=(nb,),
        in_specs=[
            pl.BlockSpec((ROW_BLK, d), lambda i: (i, 0)),
            pl.BlockSpec((1, d), lambda i: (0, 0)),
            pl.BlockSpec((None, 1, d), lambda i: (i, 0, 0)),
            pl.BlockSpec((None, 1, d), lambda i: (i, 0, 0)),
        ],
        out_specs=pl.BlockSpec((ROW_BLK, d), lambda i: (i, 0)),
        out_shape=jax.ShapeDtypeStruct((m, d), out_dtype),
        compiler_params=_cparams("parallel"),
        name="norm_mod",
    )(x, nw.reshape(1, d), scale, shift)


def _norm_mod_rows(x_ref, nw_ref, sc_ref, sh_ref, s):
    x = x_ref[s * ROW_BLK:(s + 1) * ROW_BLK, :]
    y = x * lax.rsqrt(jnp.mean(x * x, -1, keepdims=True) + RMS_EPS) * nw_ref[...]
    return (y * (1.0 + sc_ref[s]) + sh_ref[s]).astype(BF16)


def _norm_specs(tm, d):
    sub = tm // ROW_BLK
    return [
        pl.BlockSpec((tm, d), lambda j, i: (i, 0)),
        pl.BlockSpec((1, d), lambda j, i: (0, 0)),
        pl.BlockSpec((sub, 1, d), lambda j, i: (i, 0, 0)),
        pl.BlockSpec((sub, 1, d), lambda j, i: (i, 0, 0)),
    ]


def _norm_mm_kernel(x_ref, nw_ref, sc_ref, sh_ref, b_ref, o_ref, h_ref, *, tm):
    @pl.when(pl.program_id(1) == 0)
    def _():
        for s in range(tm // ROW_BLK):
            h_ref[s * ROW_BLK:(s + 1) * ROW_BLK, :] = _norm_mod_rows(x_ref, nw_ref, sc_ref, sh_ref, s)

    o_ref[...] = jnp.dot(h_ref[...], b_ref[...], preferred_element_type=F32)


def _norm_matmul(x, nw, scale, shift, b, tm=1024, tn=512):
    m, d = x.shape
    n = b.shape[1]
    sub = tm // ROW_BLK
    return pl.pallas_call(
        functools.partial(_norm_mm_kernel, tm=tm),
        grid=(m // tm, n // tn),
        in_specs=[
            pl.BlockSpec((tm, d), lambda i, j: (i, 0)),
            pl.BlockSpec((1, d), lambda i, j: (0, 0)),
            pl.BlockSpec((sub, 1, d), lambda i, j: (i, 0, 0)),
            pl.BlockSpec((sub, 1, d), lambda i, j: (i, 0, 0)),
            pl.BlockSpec((d, tn), lambda i, j: (0, j)),
        ],
        out_specs=pl.BlockSpec((tm, tn), lambda i, j: (i, j)),
        out_shape=jax.ShapeDtypeStruct((m, n), F32),
        scratch_shapes=[pltpu.VMEM((tm, d), BF16)],
        compiler_params=_cparams("parallel", "arbitrary"),
        name="norm_matmul",
    )(x, nw.reshape(1, d), scale, shift, b)


def _gated_residual(x_ref, g_ref, acc, o_ref, tm):
    for s in range(tm // ROW_BLK):
        rows = slice(s * ROW_BLK, (s + 1) * ROW_BLK)
        o_ref[rows, :] = x_ref[rows, :] + g_ref[s] * acc[rows, :]


def _cast_weight_tiles(w_refs, wb_refs):
    @pl.when(pl.program_id(1) == 0)
    def _():
        for w_ref, wb_ref in zip(w_refs, wb_refs):
            wb_ref[...] = w_ref[...].astype(BF16)


def _mm_res_kernel(*refs, n_in, tm):
    a_refs, w_refs = refs[:n_in], refs[n_in:2 * n_in]
    x_ref, g_ref, o_ref = refs[2 * n_in:2 * n_in + 3]
    wb_refs = refs[2 * n_in + 3:]
    _cast_weight_tiles(w_refs, wb_refs)
    acc = jnp.dot(a_refs[0][...], wb_refs[0][...], preferred_element_type=F32)
    for a_ref, wb_ref in zip(a_refs[1:], wb_refs[1:]):
        acc = acc + jnp.dot(a_ref[...], wb_ref[...], preferred_element_type=F32)
    _gated_residual(x_ref, g_ref, acc, o_ref, tm)


def _matmul_residual(a_list, w, layer, x, gate, tm=512, tn=512, single_buffer_w=False):
    m, n = x.shape
    sub = tm // ROW_BLK
    kb = w.shape[1] // len(a_list)
    w_mode = dict(pipeline_mode=pl.Buffered(1)) if single_buffer_w else {}
    in_specs = [pl.BlockSpec((tm, kb), lambda j, i: (i, 0)) for _ in a_list]
    in_specs += [pl.BlockSpec((None, kb, tn), functools.partial(lambda j, i, r: (layer, r, j), r=r), **w_mode)
                 for r in range(len(a_list))]
    in_specs += [
        pl.BlockSpec((tm, tn), lambda j, i: (i, j)),
        pl.BlockSpec((sub, 1, tn), lambda j, i: (i, 0, j)),
    ]
    return pl.pallas_call(
        functools.partial(_mm_res_kernel, n_in=len(a_list), tm=tm),
        grid=(n // tn, m // tm),
        in_specs=in_specs,
        out_specs=pl.BlockSpec((tm, tn), lambda j, i: (i, j)),
        out_shape=jax.ShapeDtypeStruct((m, n), F32),
        scratch_shapes=[pltpu.VMEM((kb, tn), BF16) for _ in a_list],
        compiler_params=_cparams("parallel", "arbitrary"),
        name="matmul_residual",
    )(*a_list, *([w] * len(a_list)), x, gate)


def _ffn_up_kernel(x_ref, nw_ref, sc_ref, sh_ref, wg_ref, wu_ref, o_ref, wgb_ref, wub_ref, *, tm):
    _cast_weight_tiles((wg_ref, wu_ref), (wgb_ref, wub_ref))
    for s in range(tm // ROW_BLK):
        h = _norm_mod_rows(x_ref, nw_ref, sc_ref, sh_ref, s)
        g = jnp.dot(h, wgb_ref[...], preferred_element_type=F32)
        u = jnp.dot(h, wub_ref[...], preferred_element_type=F32)
        o_ref[s * ROW_BLK:(s + 1) * ROW_BLK, :] = (_silu(g) * u).astype(o_ref.dtype)


def _ffn_up(x, nw, scale, shift, wg, wu, layer, tm=512, tn=512):
    m, d = x.shape
    n = wg.shape[2]
    w_spec = lambda: pl.BlockSpec((None, d, tn), lambda j, i: (layer, 0, j))
    return pl.pallas_call(
        functools.partial(_ffn_up_kernel, tm=tm),
        grid=(n // tn, m // tm),
        in_specs=_norm_specs(tm, d) + [w_spec(), w_spec()],
        out_specs=pl.BlockSpec((tm, tn), lambda j, i: (i, j)),
        out_shape=jax.ShapeDtypeStruct((m, n), BF16),
        scratch_shapes=[pltpu.VMEM((d, tn), BF16)] * 2,
        compiler_params=_cparams("parallel", "arbitrary"),
        name="ffn_up",
    )(x, nw.reshape(1, d), scale, shift, wg, wu)


def _shift_rows(x, shift, valid):
    t = x.shape[0]
    return jnp.where(valid, pltpu.roll(x, shift % t, 0), 0.0)


def _token_shift_kernel(p_ref, mu_ref, o_ref, *, ctx_blocks, t_ctx):
    x = p_ref[...]
    t, c = x.shape
    row = lax.broadcasted_iota(jnp.int32, (t, c), 0)
    lane = lax.broadcasted_iota(jnp.int32, (t, c), 1)
    mu = mu_ref[...]

    @pl.when(pl.program_id(0) < ctx_blocks)
    def _():
        pos = row % t_ctx
        s = jnp.where(lane % 2 == 0, _shift_rows(x, 1, pos != 0), _shift_rows(x, -1, pos != t_ctx - 1))
        o_ref[...] = x + (s - x) * mu

    @pl.when(pl.program_id(0) >= ctx_blocks)
    def _():
        col = row % GRID_W
        sel = lane % 4
        s = jnp.where(sel == 0, _shift_rows(x, 1, col != 0),
            jnp.where(sel == 1, _shift_rows(x, -1, col != GRID_W - 1),
            jnp.where(sel == 2, _shift_rows(x, GRID_W, row >= GRID_W),
                      _shift_rows(x, -GRID_W, row < t - GRID_W))))
        o_ref[...] = x + (s - x) * mu


def _token_shift(pa, mu, t, ctx_blocks, t_ctx):
    tc = 512
    m = pa.shape[0]
    return pl.pallas_call(
        functools.partial(_token_shift_kernel, ctx_blocks=ctx_blocks, t_ctx=t_ctx),
        grid=(m // t, C_PA_PAD // tc),
        in_specs=[
            pl.BlockSpec((t, tc), lambda b, j: (b, j)),
            pl.BlockSpec((1, tc), lambda b, j: (0, j)),
        ],
        out_specs=pl.BlockSpec((t, tc), lambda b, j: (b, j)),
        out_shape=jax.ShapeDtypeStruct((m, C_PA_PAD), F32),
        compiler_params=_cparams("parallel", "parallel"),
        name="token_shift",
    )(pa, mu)


def _rwkv_lora_kernel(lo_ref, w2_ref, a2_ref, g2_ref, w0_ref, a0_ref, lw_ref, ic_ref, gt_ref):
    lo = lo_ref[...]
    w_in = jnp.tanh(lo[:, 0:2 * LORA])
    a_in = lo[:, 2 * LORA:4 * LORA]
    g_in = _sigmoid(lo[:, 4 * LORA:])
    for d in range(2):
        w_log = -_softplus(-(w0_ref[d] + _bdot(w_in, w2_ref[d]))) - 0.5
        lw_ref[d] = -jnp.exp(w_log)
        ic_ref[d] = _sigmoid(a0_ref[d] + _bdot(a_in, a2_ref[d]))
    gt_ref[...] = _bdot(g_in, g2_ref[...])


def _rwkv_lora(pam, w2bd, a2bd, g2pad, w0, a0, tm=512):
    m = pam.shape[0]
    lo_blk = 3 * C_A // C_LO
    return pl.pallas_call(
        _rwkv_lora_kernel,
        grid=(m // tm,),
        in_specs=[
            pl.BlockSpec((tm, C_LO), lambda i: (i, lo_blk)),
            pl.BlockSpec((2, 2 * LORA, C_A), lambda i: (0, 0, 0)),
            pl.BlockSpec((2, 2 * LORA, C_A), lambda i: (0, 0, 0)),
            pl.BlockSpec((C_LO - 4 * LORA, C_A), lambda i: (0, 0)),
            pl.BlockSpec((2, 1, C_A), lambda i: (0, 0, 0)),
            pl.BlockSpec((2, 1, C_A), lambda i: (0, 0, 0)),
        ],
        out_specs=[
            pl.BlockSpec((2, tm, C_A), lambda i: (0, i, 0)),
            pl.BlockSpec((2, tm, C_A), lambda i: (0, i, 0)),
            pl.BlockSpec((tm, C_A), lambda i: (i, 0)),
        ],
        out_shape=[
            jax.ShapeDtypeStruct((2, m, C_A), F32),
            jax.ShapeDtypeStruct((2, m, C_A), F32),
            jax.ShapeDtypeStruct((m, C_A), F32),
        ],
        compiler_params=_cparams("parallel"),
        name="rwkv_lora",
    )(pam, w2bd, a2bd, g2pad, w0, a0)


def _order_masks(n, reverse):
    rr = lax.broadcasted_iota(jnp.int32, (n, n), 0)
    cc = lax.broadcasted_iota(jnp.int32, (n, n), 1)
    if reverse:
        return cc >= rr, cc > rr
    return cc <= rr, cc < rr


def _run_lockstep(chains):
    results = [None] * len(chains)
    pending = {i: next(c) for i, c in enumerate(chains)}
    while pending:
        products = {i: fn(a, b, dims) for i, (fn, a, b, dims) in pending.items()}
        pending = {}
        for i, prod in products.items():
            try:
                pending[i] = chains[i].send(prod)
            except StopIteration as stop:
                results[i] = stop.value
    return results


def _triangular_inverse(l, order):
    n = l.shape[0]
    ri = lax.broadcasted_iota(jnp.int32, (n, n), 0)
    ci = lax.broadcasted_iota(jnp.int32, (n, n), 1)

    def joins(k):
        return (ri // (2 * k) == ci // (2 * k)) & ((ri // k) != (ci // k))

    x = (ri == ci).astype(F32) + jnp.where(joins(1), l, 0.0)
    k = 2
    while k < order:
        xe = yield _bdot, x, jnp.where(joins(k), l, 0.0), NN
        x = x + (yield _bdot, xe, x, NN)
        k *= 2
    return x


def _split_heads(x, first):
    return jnp.concatenate([jnp.where(first, x, 0.0), jnp.where(first, 0.0, x)], 0)


def _chunk_cumsum(x, reverse, chunk):
    t = x.shape[0]
    pos = lax.broadcasted_iota(jnp.int32, x.shape, 0) % chunk
    s = 1
    while s < chunk:
        if reverse:
            x = x + jnp.where(pos < chunk - s, pltpu.roll(x, t - s, 0), 0.0)
        else:
            x = x + jnp.where(pos >= s, pltpu.roll(x, s, 0), 0.0)
        s *= 2
    return x


def _rwkv_chunk(s_bd, r, lw, cum, k, v, a, b, reverse):
    c = CHUNK
    cum_ex = cum - lw
    tot = cum[0:1] if reverse else cum[c - 1:c]
    e_neg = jnp.exp(-cum)
    e_hat = jnp.exp(tot - cum)
    at = a * jnp.exp(cum_ex)
    rt = r * jnp.exp(cum)
    first = lax.broadcasted_iota(jnp.int32, (c, LANES), 1) < HEAD_A
    lhs = jnp.concatenate([at, rt], 0)
    rbk = jnp.concatenate([_split_heads(b * e_neg, first), _split_heads(k * e_neg, first)], 0)
    g = yield _bdot, lhs, rbk, NT
    ri_g = lax.broadcasted_iota(jnp.int32, (2 * c, 4 * c), 0)
    pos_c = lax.broadcasted_iota(jnp.int32, (2 * c, 4 * c), 1) % c
    pos_r = ri_g % c
    own = jnp.where(ri_g < c, 0, 1)
    keep = (pos_c > pos_r - own) if reverse else (pos_c < pos_r + own)
    g = jnp.where(keep, g, 0.0)
    l_ab = g[0:c, 0:2 * c]
    l_ak = g[0:c, 2 * c:4 * c]
    m_r = g[c:2 * c, :]
    v_bd = _split_heads(v, first)
    sa = yield _bdot, lhs, s_bd, NT
    rhs = sa[0:c] + (yield _bdot, l_ak, v_bd, NN)
    l_bd = _split_heads(l_ab, first)
    t_inv = yield from _triangular_inverse(l_bd, c)
    u_bd = yield _bdot, t_inv, _split_heads(rhs, first), NN
    y = sa[c:2 * c] + (yield _bdot, m_r, jnp.concatenate([u_bd, v_bd], 0), NN)
    u = u_bd[0:c] + u_bd[c:2 * c]
    upd = yield _bdot, jnp.concatenate([u, v], 0), jnp.concatenate([b * e_hat, k * e_hat], 0), TN
    ri = lax.broadcasted_iota(jnp.int32, (LANES, LANES), 0) < HEAD_A
    ci = lax.broadcasted_iota(jnp.int32, (LANES, LANES), 1) < HEAD_A
    s_new = s_bd * jnp.exp(tot) + jnp.where(ri == ci, upd, 0.0)
    return y, s_new


def _head_sum(x, first):
    s0 = jnp.sum(jnp.where(first, x, 0.0), -1, keepdims=True)
    s1 = jnp.sum(jnp.where(first, 0.0, x), -1, keepdims=True)
    return jnp.where(first, s0, s1)


def _rwkv_scan_kernel(r_ref, k_ref, v_ref, lw_ref, ic_ref, gt_ref, kk_ref, ka_ref, rk_ref, lnw_ref, lnb_ref,
                      s0_ref, o_ref, sf_ref, acc_ref, cum_ref, *, t, pairs):
    n = t // CHUNK
    first = lax.broadcasted_iota(jnp.int32, (CHUNK, LANES), 1) < HEAD_A
    for d in range(2):
        cum_ref[d] = _chunk_cumsum(lw_ref[d], d == 1, CHUNK)
    sf_ref[...] = s0_ref[...]

    def body(i, carry):
        where, chains = [], []
        for d in range(2):
            reverse = d == 1
            ci = (n - 1 - i) if reverse else i
            rows = pl.ds(pl.multiple_of(ci * CHUNK, CHUNK), CHUNK)
            for p in range(pairs):
                lanes = slice(p * LANES, (p + 1) * LANES)
                r = r_ref[rows, lanes]
                k = k_ref[rows, lanes]
                v = v_ref[rows, lanes]
                iclr = ic_ref[d, rows, lanes]
                kk = k * kk_ref[:, lanes]
                kk = kk * lax.rsqrt(_head_sum(kk * kk, first) + 1e-12)
                k_d = k * (1.0 + (iclr - 1.0) * ka_ref[:, lanes])
                where.append((d, p, rows, lanes, r, k_d, v))
                chains.append(_rwkv_chunk(sf_ref[d, p], r, lw_ref[d, rows, lanes], cum_ref[d, rows, lanes],
                                          k_d, v, -kk, kk * iclr, reverse))
        for (d, p, rows, lanes, r, k_d, v), (y, s_new) in zip(where, _run_lockstep(chains)):
            sf_ref[d, p] = s_new
            mu = _head_sum(y, first) * (1.0 / HEAD_A)
            yc = y - mu
            var = _head_sum(yc * yc, first) * (1.0 / HEAD_A)
            out = yc * lax.rsqrt(var + GN_EPS) * lnw_ref[:, lanes] + lnb_ref[:, lanes]
            acc_ref[d, rows, lanes] = out + _head_sum(r * k_d * rk_ref[:, lanes], first) * v
        return carry

    lax.fori_loop(0, n, body, 0)
    o_ref[...] = ((acc_ref[0] + acc_ref[1]) * gt_ref[...]).astype(o_ref.dtype)


def _rwkv_scan(pam, lw, ic, gate, k_k, k_a, r_k, ln_w, ln_b, s0_bd, n_seq, t, row_blk0, pairs=4):
    w = pairs * LANES
    nb = C_A // w
    vec = lambda: pl.BlockSpec((1, w), lambda b, h: (0, h))
    rows = lambda off: pl.BlockSpec((t, w), lambda b, h: (row_blk0 + b, off + h))
    rows2 = lambda: pl.BlockSpec((2, t, w), lambda b, h: (0, row_blk0 + b, h))
    state = pl.BlockSpec((None, 2, pairs, LANES, LANES), lambda b, h: (b, 0, h, 0, 0))
    return pl.pallas_call(
        functools.partial(_rwkv_scan_kernel, t=t, pairs=pairs),
        grid=(n_seq, nb),
        in_specs=[
            rows(0), rows(nb), rows(2 * nb), rows2(), rows2(), rows(0),
            vec(), vec(), vec(), vec(), vec(),
            state,
        ],
        out_specs=[pl.BlockSpec((t, w), lambda b, h: (b, h)), state],
        out_shape=[
            jax.ShapeDtypeStruct((n_seq * t, C_A), BF16),
            jax.ShapeDtypeStruct((n_seq, 2, C_A // LANES, LANES, LANES), F32),
        ],
        scratch_shapes=[pltpu.VMEM((2, t, w), F32), pltpu.VMEM((2, t, w), F32)],
        compiler_params=_cparams("parallel", "parallel"),
        name="rwkv_scan",
    )(pam, pam, pam, lw, ic, gate, k_k, k_a, r_k, ln_w, ln_b, s0_bd)


def _gdn_chunk(s, q, k, v, gc, beta, reverse):
    c = q.shape[0]
    incl, strict = _order_masks(c, reverse)
    pick = (lax.broadcasted_iota(jnp.int32, (8, LANES), 1) == 0).astype(F32)
    gr = (yield _fdot, pick, jnp.broadcast_to(gc, (c, LANES)), NT)[0:1, :]
    decay = jnp.where(incl, jnp.exp(jnp.where(incl, gc - gr, 0.0)), 0.0)
    g_last = gc[0:1] if reverse else gc[c - 1:c]
    kb = k * beta
    l = jnp.where(strict, (yield _bdot, kb, k, NT) * decay, 0.0)
    attn = (yield _bdot, q, k, NT) * decay
    a_inv = yield from _triangular_inverse(-l, c)
    eg = jnp.exp(gc)
    sol = yield _bdot, a_inv, jnp.concatenate([v * beta, kb * eg], -1), NN
    u, w = sol[:, 0:HEAD_B], sol[:, HEAD_B:]
    ws = yield _bdot, jnp.concatenate([w, q * eg], 0), s, NN
    v_new = u - ws[0:c]
    o = ws[c:2 * c] + (yield _bdot, attn, v_new, NN)
    s_new = s * jnp.exp(g_last) + (yield _bdot, k * jnp.exp(g_last - gc), v_new, TN)
    return o, s_new


def _gdn_scan_kernel(q_ref, k_ref, v_ref, z_ref, ba_ref, cq_ref, ck_ref, cv_ref, gnw_ref, alog_ref, dtb_ref,
                     s0_ref, o_ref, sf_ref, qs_ref, ks_ref, vs_ref, acc_ref, gc_ref, beta_ref, *, t, heads):
    n = t // CHUNK_B
    hg = pl.program_id(1)
    blk = CHUNK_B
    brow = lax.broadcasted_iota(jnp.int32, (blk, LANES), 0)
    no_row = jnp.zeros((1, LANES), F32)

    def conv_silu(x_ref, w, r0, lanes):
        x = x_ref[r0:r0 + blk, lanes]
        before = x_ref[r0 - 1:r0, lanes] if r0 > 0 else no_row
        after = x_ref[r0 + blk:r0 + blk + 1, lanes] if r0 + blk < t else no_row
        prev = jnp.where(brow == 0, before, pltpu.roll(x, 1, 0))
        nxt = jnp.where(brow == blk - 1, after, pltpu.roll(x, blk - 1, 0))
        return _silu(w[0:1] * prev + w[1:2] * x + w[2:3] * nxt)

    def l2norm(x):
        return x * lax.rsqrt(jnp.sum(x * x, -1, keepdims=True) + 1e-6)

    for g in range(heads):
        lanes = slice(g * LANES, (g + 1) * LANES)
        wq, wk, wv = cq_ref[:, lanes], ck_ref[:, lanes], cv_ref[:, lanes]
        for r0 in range(0, t, blk):
            rows = slice(r0, r0 + blk)
            qs_ref[rows, lanes] = l2norm(conv_silu(q_ref, wq, r0, lanes)) * (HEAD_B ** -0.5)
            ks_ref[rows, lanes] = l2norm(conv_silu(k_ref, wk, r0, lanes))
            vs_ref[rows, lanes] = conv_silu(v_ref, wv, r0, lanes)
    ba = ba_ref[...]
    beta_ref[...] = _sigmoid(ba)
    g_log = -jnp.exp(alog_ref[...]) * _softplus(ba + dtb_ref[...])
    gc_ref[0] = _chunk_cumsum(g_log, False, CHUNK_B)
    gc_ref[1] = _chunk_cumsum(g_log, True, CHUNK_B)
    sf_ref[...] = s0_ref[...]
    lane = lax.broadcasted_iota(jnp.int32, (CHUNK_B, LANES), 1)

    def body(i, carry):
        chains = []
        for d in range(2):
            ci = (n - 1 - i) if d == 1 else i
            rows = pl.ds(pl.multiple_of(ci * CHUNK_B, CHUNK_B), CHUNK_B)
            gcs = gc_ref[d, rows, :]
            bts = beta_ref[rows, :]
            for g in range(heads):
                lanes = slice(g * LANES, (g + 1) * LANES)
                chains.append((d, g, rows, lanes, gcs, bts, sf_ref[d, g],
                               qs_ref[rows, lanes], ks_ref[rows, lanes], vs_ref[rows, lanes]))
        gens = []
        for d, g, rows, lanes, gcs, bts, s, q, k, v in chains:
            col = d * H_B + hg * heads + g
            gc = jnp.sum(jnp.where(lane == BA_LANE0 + 2 * H_B + col, gcs, 0.0), -1, keepdims=True)
            beta = jnp.sum(jnp.where(lane == BA_LANE0 + col, bts, 0.0), -1, keepdims=True)
            gens.append(_gdn_chunk(s, q, k, v, gc, beta, d == 1))
        for (d, g, rows, lanes, *_), (o, s_new) in zip(chains, _run_lockstep(gens)):
            sf_ref[d, g] = s_new
            acc_ref[d, rows, lanes] = o
        return carry

    lax.fori_loop(0, n, body, 0)
    for g in range(heads):
        lanes = slice(g * LANES, (g + 1) * LANES)
        for r0 in range(0, t, blk):
            rows = slice(r0, r0 + blk)
            o = acc_ref[0, rows, lanes] + acc_ref[1, rows, lanes]
            o = o * lax.rsqrt(jnp.mean(o * o, -1, keepdims=True) + RMS_EPS) * gnw_ref[...]
            o_ref[rows, lanes] = (o * _silu(z_ref[rows, lanes])).astype(o_ref.dtype)


def _gdn_scan(p, conv_w, alog_vec, dtb_vec, gn_w, s0, n_seq, t, row_blk0, heads=4):
    w = heads * LANES
    nb = C_B // w
    first = C_PA_PAD // w
    rows = lambda off: pl.BlockSpec((t, w), lambda b, h: (row_blk0 + b, first + off + h))
    cw = lambda off: pl.BlockSpec((3, w), lambda b, h: (0, off + h))
    vec = pl.BlockSpec((1, LANES), lambda b, h: (0, 0))
    state = pl.BlockSpec((None, 2, heads, HEAD_B, HEAD_B), lambda b, h: (b, 0, h, 0, 0))
    return pl.pallas_call(
        functools.partial(_gdn_scan_kernel, t=t, heads=heads),
        grid=(n_seq, nb),
        in_specs=[
            rows(0), rows(nb), rows(2 * nb), rows(3 * nb),
            pl.BlockSpec((t, LANES), lambda b, h: (row_blk0 + b, C_PA // LANES)),
            cw(0), cw(nb), cw(2 * nb),
            vec, vec, vec,
            state,
        ],
        out_specs=[pl.BlockSpec((t, w), lambda b, h: (b, h)), state],
        out_shape=[
            jax.ShapeDtypeStruct((n_seq * t, C_B), BF16),
            jax.ShapeDtypeStruct((n_seq, 2, H_B, HEAD_B, HEAD_B), F32),
        ],
        scratch_shapes=[pltpu.VMEM((t, w), F32)] * 3 + [pltpu.VMEM((2, t, w), F32), pltpu.VMEM((2, t, LANES), F32),
                                                         pltpu.VMEM((t, LANES), F32)],
        compiler_params=_cparams("parallel", "parallel"),
        name="gdn_scan",
    )(p, p, p, p, p, conv_w, conv_w, conv_w, gn_w, alog_vec, dtb_vec, s0)


def _pool_kernel(h_ref, x_ref, w_ref, sc_ref, g_ref, o_ref, *, ctx_blocks, t_ctx):
    gi = pl.program_id(1)
    h = h_ref[...]
    t = jnp.where(pl.program_id(0) < ctx_blocks, t_ctx, h.shape[0])
    row = lax.broadcasted_iota(jnp.int32, h.shape, 0) & (t - 1)

    def back(x, n):
        return _shift_rows(x, n, row >= n)

    def fwd(x, n):
        return _shift_rows(x, -n, row < t - n)

    p2 = back(h, 1) + h
    q2 = h + fwd(h, 1)
    p4 = back(p2, 2) + p2
    q4 = q2 + fwd(q2, 2)
    p8 = back(p4, 4) + p4
    q8 = q4 + fwd(q4, 4)
    p_h = jnp.where(gi == 0, h, jnp.where(gi == 1, p2, jnp.where(gi == 2, p4, p8)))
    q_h = jnp.where(gi == 0, h, jnp.where(gi == 1, q2, jnp.where(gi == 2, q4, q8)))
    s = back(p_h, 1) + q_h
    half = jnp.left_shift(1, gi)
    lo = jnp.clip(row - half, 0, t)
    hi = jnp.clip(row + half, 0, t)
    p = s / (hi - lo).astype(F32) - h
    y = _bdot(p, w_ref[...]) * sc_ref[...]
    o_ref[...] = x_ref[...] + g_ref[...] * y


def _pool_mixer(h, x, w_pool, layer, scale, gate_blk, t, ctx_blocks, t_ctx):
    m = x.shape[0]
    return pl.pallas_call(
        functools.partial(_pool_kernel, ctx_blocks=ctx_blocks, t_ctx=t_ctx),
        grid=(m // t, POOL_GROUPS),
        in_specs=[
            pl.BlockSpec((t, C_G), lambda b, g: (b, g)),
            pl.BlockSpec((t, C_G), lambda b, g: (b, g)),
            pl.BlockSpec((None, None, C_G, C_G), lambda b, g: (layer, g, 0, 0)),
            pl.BlockSpec((1, C_G), lambda b, g: (0, g)),
            pl.BlockSpec((None, 1, C_G), lambda b, g: (b, 0, g)),
        ],
        out_specs=pl.BlockSpec((t, C_G), lambda b, g: (b, g)),
        out_shape=jax.ShapeDtypeStruct((m, D_MODEL), F32),
        compiler_params=_cparams("parallel", "parallel"),
        name="pool_mixer",
    )(h, x, w_pool, scale, gate_blk)


def _pad_cols(w, n):
    return jnp.pad(w, ((0, 0), (0, n - w.shape[1])))


def _block_diag_pairs(s):
    b, two, h, n, _ = s.shape
    sp = s.reshape(b, two, h // 2, 2, n, n)
    z = jnp.zeros_like(sp[:, :, :, 0])
    top = jnp.concatenate([sp[:, :, :, 0], z], -1)
    bot = jnp.concatenate([z, sp[:, :, :, 1]], -1)
    return jnp.concatenate([top, bot], -2)


def _diag_pairs(s_bd):
    b, two, hp, _, _ = s_bd.shape
    n = HEAD_A
    return jnp.stack([s_bd[..., :n, :n], s_bd[..., n:, n:]], 3).reshape(b, two, 2 * hp, n, n)


def kernel(x_prompt, x_sample, state_rwkv, state_delta, c, c_ctx, mod_w, mod_b, norm_mix_w, norm_ffn_w, norm_final_w, ab_w_in, ab_w_out, rwkv_mu, rwkv_w0, rwkv_w2, rwkv_a0, rwkv_a2, rwkv_g2, rwkv_k_k, rwkv_k_a, rwkv_r_k, rwkv_ln_w, rwkv_ln_b, gdn_conv_w, gdn_a_log, gdn_dt_bias, gdn_norm_w, pool_w, pool_scale, ffn_w_gate, ffn_w_up, ffn_w_down):
    bp, tp, d = x_prompt.shape
    bs, ts, _ = x_sample.shape
    mp, ms = bp * tp, bs * ts
    m = mp + ms
    assert tp == ROW_BLK and ts % ROW_BLK == 0 and mp % ts == 0
    x = jnp.concatenate([x_prompt.reshape(mp, d), x_sample.reshape(ms, d)], 0)

    n_cond = 16
    cond = jnp.zeros((n_cond, d), F32).at[0].set(c_ctx).at[1:1 + bs].set(c)
    mod = _mod_all(cond, mod_w, mod_b).reshape(DEPTH, n_cond, 6, d)
    mod_blk = jnp.concatenate([
        jnp.broadcast_to(mod[:, 0:1], (DEPTH, mp // ROW_BLK, 6, d)),
        jnp.repeat(mod[:, 1:1 + bs], ts // ROW_BLK, axis=1)], 1)[:, :, :, None, :]
    zeros_blk = jnp.zeros((m // ROW_BLK, 1, d), F32)

    zero_r = jnp.zeros((bp, 2, H_A // 2, LANES, LANES), F32)
    zero_d = jnp.zeros((bp, 2, H_B, HEAD_B, HEAD_B), F32)
    new_r, new_d = [], []

    for l in range(DEPTH):
        shift_m, scale_m, gate_m, shift_f, scale_f, gate_f = (mod_blk[l, :, j] for j in range(6))
        i = l // 2
        if l % 2 == 0:
            w_in = ab_w_in[i]
            w_p = jnp.concatenate([
                w_in[:, :C_PA], w_in[:, C_PA + 4 * C_B:], jnp.zeros((d, C_PA_PAD - C_PA - 4 * H_B), F32),
                w_in[:, C_PA:C_PA + 4 * C_B]], 1).astype(BF16)
            p = _norm_matmul(x, norm_mix_w[l], scale_m, shift_m, w_p)
            mu = _pad_cols(rwkv_mu[i][None, :], C_PA_PAD)
            pam = _token_shift(p, mu, ts, mp // ts, tp)
            w2bd = jnp.zeros((2, 2 * LORA, C_A), F32).at[0, :LORA].set(rwkv_w2[i, 0]).at[1, LORA:].set(rwkv_w2[i, 1])
            a2bd = jnp.zeros((2, 2 * LORA, C_A), F32).at[0, :LORA].set(rwkv_a2[i, 0]).at[1, LORA:].set(rwkv_a2[i, 1])
            g2pad = jnp.pad(rwkv_g2[i], ((0, C_LO - 4 * LORA - GATE_LORA), (0, 0)))
            lw, ic, gate = _rwkv_lora(pam, w2bd.astype(BF16), a2bd.astype(BF16), g2pad.astype(BF16),
                                      rwkv_w0[i][:, None, :], rwkv_a0[i][:, None, :])
            vecs = [v_[i].reshape(1, C_A) for v_ in (rwkv_k_k, rwkv_k_a, rwkv_r_k, rwkv_ln_w, rwkv_ln_b)]
            oa_p, sr_p = _rwkv_scan(pam, lw, ic, gate, *vecs, zero_r, bp, tp, 0)
            oa_s, _ = _rwkv_scan(pam, lw, ic, gate, *vecs, _block_diag_pairs(state_rwkv[:, i]), bs, ts, mp // ts)
            gnw = gdn_norm_w[i].reshape(1, HEAD_B)
            on_decay_lanes = lambda p_: jnp.zeros((1, LANES), F32).at[
                0, BA_LANE0 + 2 * H_B:BA_LANE0 + 4 * H_B].set(p_.reshape(2 * H_B))
            alog_vec, dtb_vec = on_decay_lanes(gdn_a_log[i]), on_decay_lanes(gdn_dt_bias[i])
            ob_p, sd_p = _gdn_scan(p, gdn_conv_w[i], alog_vec, dtb_vec, gnw, zero_d, bp, tp, 0)
            ob_s, _ = _gdn_scan(p, gdn_conv_w[i], alog_vec, dtb_vec, gnw, state_delta[:, i],
                                bs, ts, mp // ts)
            new_r.append(_diag_pairs(sr_p))
            new_d.append(sd_p)
            o_a = jnp.concatenate([oa_p, oa_s], 0)
            o_b = jnp.concatenate([ob_p, ob_s], 0)
            x = _matmul_residual([o_a, o_b], ab_w_out, i, x, gate_m)
        else:
            h = _norm_mod(x, norm_mix_w[l], scale_m, shift_m, F32)
            gate_blk = jnp.concatenate([jnp.broadcast_to(mod[l, 0:1, 2], (mp // ts, d)), mod[l, 1:1 + bs, 2]], 0)
            x = _pool_mixer(h, x, pool_w, i, pool_scale[i].reshape(1, d), gate_blk[:, None, :], ts, mp // ts, tp)
        act = _ffn_up(x, norm_ffn_w[l], scale_f, shift_f, ffn_w_gate, ffn_w_up, l)
        x = _matmul_residual([act], ffn_w_down, l, x, gate_f, single_buffer_w=True)

    y = _norm_mod(x, norm_final_w, zeros_blk, zeros_blk, F32)
    y_prompt = y[:mp].reshape(bp, tp, d)
    y_sample = y[mp:].reshape(bs, ts, d)
    return (y_prompt, y_sample, jnp.stack(new_r, 1), jnp.stack(new_d, 1))
```

```python
import functools

import jax
import jax.numpy as jnp
from jax import lax
from jax.experimental import pallas as pl
from jax.experimental.pallas import tpu as pltpu

F32 = jnp.float32
BF16 = jnp.bfloat16

D_MODEL = 2048
DEPTH = 4
GRID_W = 64
N_AB = 2
C_A = 1024
HEAD_A = 64
H_A = 16
LORA = 64
GATE_LORA = 160
C_B = 1024
HEAD_B = 128
H_B = 8
C_PA = 3 * C_A + 4 * LORA + GATE_LORA
C_PA_PAD = 3584
C_LO = C_PA_PAD - 3 * C_A
POOL_GROUPS = 4
C_G = D_MODEL // POOL_GROUPS
D_FF = 5632
RMS_EPS = 1e-6
GN_EPS = 64e-5
CHUNK = 64
CHUNK_B = 128
ROW_BLK = 256
LANES = 128
BA_LANE0 = C_PA % LANES
VMEM_LIMIT = 48 * 1024 * 1024

NT = (((1,), (1,)), ((), ()))
TN = (((0,), (0,)), ((), ()))
NN = (((1,), (0,)), ((), ()))


def _cparams(*sem):
    return pltpu.CompilerParams(dimension_semantics=sem, vmem_limit_bytes=VMEM_LIMIT)


def _bdot(a, b, dims=NN):
    return lax.dot_general(a.astype(BF16), b.astype(BF16), dims, preferred_element_type=F32)


def _fdot(a, b, dims=NN):
    return lax.dot_general(a, b, dims, precision=lax.Precision.HIGHEST, preferred_element_type=F32)


def _sigmoid(x):
    return 0.5 + 0.5 * jnp.tanh(0.5 * x)


def _silu(x):
    return x * _sigmoid(x)


def _softplus(x):
    return jnp.maximum(x, 0.0) + jnp.log(1.0 + jnp.exp(-jnp.abs(x)))


def _mod_kernel(c_ref, w_ref, b_ref, o_ref):
    act = _silu(c_ref[...])
    o_ref[...] = _fdot(act, w_ref[...]) + b_ref[...]


def _mod_all(cond, mod_w, mod_b):
    n_c, d = cond.shape
    depth, _, n = mod_w.shape
    tn = 1024
    return pl.pallas_call(
        _mod_kernel,
        grid=(depth, n // tn),
        in_specs=[
            pl.BlockSpec((n_c, d), lambda l, j: (0, 0)),
            pl.BlockSpec((None, d, tn), lambda l, j: (l, 0, j)),
            pl.BlockSpec((None, 1, tn), lambda l, j: (l, 0, j)),
        ],
        out_specs=pl.BlockSpec((None, n_c, tn), lambda l, j: (l, 0, j)),
        out_shape=jax.ShapeDtypeStruct((depth, n_c, n), F32),
        compiler_params=_cparams("parallel", "parallel"),
        name="mod_all",
    )(cond, mod_w, mod_b.reshape(depth, 1, n))


def _norm_mod_kernel(x_ref, nw_ref, sc_ref, sh_ref, o_ref):
    x = x_ref[...]
    ms = jnp.mean(x * x, -1, keepdims=True)
    y = x * lax.rsqrt(ms + RMS_EPS) * nw_ref[...]
    o_ref[...] = (y * (1.0 + sc_ref[...]) + sh_ref[...]).astype(o_ref.dtype)


def _norm_mod(x, nw, scale, shift, out_dtype):
    m, d = x.shape
    nb = m // ROW_BLK
    return pl.pallas_call(
        _norm_mod_kernel,
        grid=(nb,),
        in_specs=[
            pl.BlockSpec((ROW_BLK, d), lambda i: (i, 0)),
            pl.BlockSpec((1, d), lambda i: (0, 0)),
            pl.BlockSpec((None, 1, d), lambda i: (i, 0, 0)),
            pl.BlockSpec((None, 1, d), lambda i: (i, 0, 0)),
        ],
        out_specs=pl.BlockSpec((ROW_BLK, d), lambda i: (i, 0)),
        out_shape=jax.ShapeDtypeStruct((m, d), out_dtype),
        compiler_params=_cparams("parallel"),
        name="norm_mod",
    )(x, nw.reshape(1, d), scale, shift)


def _norm_mod_rows(x_ref, nw_ref, sc_ref, sh_ref, s):
    x = x_ref[s * ROW_BLK:(s + 1) * ROW_BLK, :]
    y = x * lax.rsqrt(jnp.mean(x * x, -1, keepdims=True) + RMS_EPS) * nw_ref[...]
    return (y * (1.0 + sc_ref[s]) + sh_ref[s]).astype(BF16)


def _norm_mm_kernel(x_ref, nw_ref, sc_ref, sh_ref, b_ref, o_ref, h_ref, *, tm):
    @pl.when(pl.program_id(1) == 0)
    def _():
        for s in range(tm // ROW_BLK):
            h_ref[s * ROW_BLK:(s + 1) * ROW_BLK, :] = _norm_mod_rows(x_ref, nw_ref, sc_ref, sh_ref, s)

    o_ref[...] = jnp.dot(h_ref[...], b_ref[...], preferred_element_type=F32)


def _norm_matmul(x, nw, scale, shift, b, tm=1024, tn=512):
    m, d = x.shape
    n = b.shape[1]
    sub = tm // ROW_BLK
    return pl.pallas_call(
        functools.partial(_norm_mm_kernel, tm=tm),
        grid=(m // tm, n // tn),
        in_specs=[
            pl.BlockSpec((tm, d), lambda i, j: (i, 0)),
            pl.BlockSpec((1, d), lambda i, j: (0, 0)),
            pl.BlockSpec((sub, 1, d), lambda i, j: (i, 0, 0)),
            pl.BlockSpec((sub, 1, d), lambda i, j: (i, 0, 0)),
            pl.BlockSpec((d, tn), lambda i, j: (0, j)),
        ],
        out_specs=pl.BlockSpec((tm, tn), lambda i, j: (i, j)),
        out_shape=jax.ShapeDtypeStruct((m, n), F32),
        scratch_shapes=[pltpu.VMEM((tm, d), BF16)],
        compiler_params=_cparams("parallel", "arbitrary"),
        name="norm_matmul",
    )(x, nw.reshape(1, d), scale, shift, b)


def _gated_residual(x_ref, g_ref, acc, o_ref, tm):
    for s in range(tm // ROW_BLK):
        rows = slice(s * ROW_BLK, (s + 1) * ROW_BLK)
        o_ref[rows, :] = x_ref[rows, :] + g_ref[s] * acc[rows, :]


def _cast_weight_tiles(w_refs, wb_refs):
    @pl.when(pl.program_id(1) == 0)
    def _():
        for w_ref, wb_ref in zip(w_refs, wb_refs):
            wb_ref[...] = w_ref[...].astype(BF16)


def _mm_res_kernel(*refs, n_in, tm):
    a_refs, w_refs = refs[:n_in], refs[n_in:2 * n_in]
    x_ref, g_ref, o_ref = refs[2 * n_in:2 * n_in + 3]
    wb_refs = refs[2 * n_in + 3:]
    _cast_weight_tiles(w_refs, wb_refs)
    acc = jnp.dot(a_refs[0][...], wb_refs[0][...], preferred_element_type=F32)
    for a_ref, wb_ref in zip(a_refs[1:], wb_refs[1:]):
        acc = acc + jnp.dot(a_ref[...], wb_ref[...], preferred_element_type=F32)
    _gated_residual(x_ref, g_ref, acc, o_ref, tm)


def _matmul_residual(a_list, w, layer, x, gate, tm=512, tn=512, single_buffer_w=False):
    m, n = x.shape
    sub = tm // ROW_BLK
    kb = w.shape[1] // len(a_list)
    w_mode = dict(pipeline_mode=pl.Buffered(1)) if single_buffer_w else {}
    in_specs = [pl.BlockSpec((tm, kb), lambda j, i: (i, 0)) for _ in a_list]
    in_specs += [pl.BlockSpec((None, kb, tn), functools.partial(lambda j, i, r: (layer, r, j), r=r), **w_mode)
                 for r in range(len(a_list))]
    in_specs += [
        pl.BlockSpec((tm, tn), lambda j, i: (i, j)),
        pl.BlockSpec((sub, 1, tn), lambda j, i: (i, 0, j)),
    ]
    return pl.pallas_call(
        functools.partial(_mm_res_kernel, n_in=len(a_list), tm=tm),
        grid=(n // tn, m // tm),
        in_specs=in_specs,
        out_specs=pl.BlockSpec((tm, tn), lambda j, i: (i, j)),
        out_shape=jax.ShapeDtypeStruct((m, n), F32),
        scratch_shapes=[pltpu.VMEM((kb, tn), BF16) for _ in a_list],
        compiler_params=_cparams("parallel", "arbitrary"),
        name="matmul_residual",
    )(*a_list, *([w] * len(a_list)), x, gate)


def _ffn_up_kernel(x_ref, nw_ref, sc_ref, sh_ref, wg_ref, wu_ref, o_ref, h_ref, *, tm):
    @pl.when(pl.program_id(1) == 0)
    def _():
        for s in range(tm // ROW_BLK):
            h_ref[s * ROW_BLK:(s + 1) * ROW_BLK, :] = _norm_mod_rows(x_ref, nw_ref, sc_ref, sh_ref, s)

    h = h_ref[...]
    g = jnp.dot(h, wg_ref[...].astype(BF16), preferred_element_type=F32)
    u = jnp.dot(h, wu_ref[...].astype(BF16), preferred_element_type=F32)
    o_ref[...] = (_silu(g) * u).astype(o_ref.dtype)


def _ffn_up(x, nw, scale, shift, wg, wu, layer, tm=1024, tn=256):
    m, d = x.shape
    n = wg.shape[2]
    sub = tm // ROW_BLK
    w_spec = lambda: pl.BlockSpec((None, d, tn), lambda i, j: (layer, 0, j))
    return pl.pallas_call(
        functools.partial(_ffn_up_kernel, tm=tm),
        grid=(m // tm, n // tn),
        in_specs=[
            pl.BlockSpec((tm, d), lambda i, j: (i, 0)),
            pl.BlockSpec((1, d), lambda i, j: (0, 0)),
            pl.BlockSpec((sub, 1, d), lambda i, j: (i, 0, 0)),
            pl.BlockSpec((sub, 1, d), lambda i, j: (i, 0, 0)),
            w_spec(), w_spec(),
        ],
        out_specs=pl.BlockSpec((tm, tn), lambda i, j: (i, j)),
        out_shape=jax.ShapeDtypeStruct((m, n), BF16),
        scratch_shapes=[pltpu.VMEM((tm, d), BF16)],
        compiler_params=_cparams("parallel", "arbitrary"),
        name="ffn_up",
    )(x, nw.reshape(1, d), scale, shift, wg, wu)


def _shift_rows(x, shift, valid):
    t = x.shape[0]
    return jnp.where(valid, pltpu.roll(x, shift % t, 0), 0.0)


def _token_shift_kernel(p_ref, mu_ref, o_ref, *, ctx_blocks, t_ctx):
    x = p_ref[...]
    t, c = x.shape
    row = lax.broadcasted_iota(jnp.int32, (t, c), 0)
    lane = lax.broadcasted_iota(jnp.int32, (t, c), 1)
    mu = mu_ref[...]

    @pl.when(pl.program_id(0) < ctx_blocks)
    def _():
        pos = row % t_ctx
        s = jnp.where(lane % 2 == 0, _shift_rows(x, 1, pos != 0), _shift_rows(x, -1, pos != t_ctx - 1))
        o_ref[...] = x + (s - x) * mu

    @pl.when(pl.program_id(0) >= ctx_blocks)
    def _():
        col = row % GRID_W
        sel = lane % 4
        s = jnp.where(sel == 0, _shift_rows(x, 1, col != 0),
            jnp.where(sel == 1, _shift_rows(x, -1, col != GRID_W - 1),
            jnp.where(sel == 2, _shift_rows(x, GRID_W, row >= GRID_W),
                      _shift_rows(x, -GRID_W, row < t - GRID_W))))
        o_ref[...] = x + (s - x) * mu


def _token_shift(pa, mu, t, ctx_blocks, t_ctx):
    tc = 512
    m = pa.shape[0]
    return pl.pallas_call(
        functools.partial(_token_shift_kernel, ctx_blocks=ctx_blocks, t_ctx=t_ctx),
        grid=(m // t, C_PA_PAD // tc),
        in_specs=[
            pl.BlockSpec((t, tc), lambda b, j: (b, j)),
            pl.BlockSpec((1, tc), lambda b, j: (0, j)),
        ],
        out_specs=pl.BlockSpec((t, tc), lambda b, j: (b, j)),
        out_shape=jax.ShapeDtypeStruct((m, C_PA_PAD), F32),
        compiler_params=_cparams("parallel", "parallel"),
        name="token_shift",
    )(pa, mu)


def _rwkv_lora_kernel(lo_ref, w2_ref, a2_ref, g2_ref, w0_ref, a0_ref, lw_ref, ic_ref, gt_ref):
    lo = lo_ref[...]
    w_in = jnp.tanh(lo[:, 0:2 * LORA])
    a_in = lo[:, 2 * LORA:4 * LORA]
    g_in = _sigmoid(lo[:, 4 * LORA:])
    for d in range(2):
        w_log = -_softplus(-(w0_ref[d] + _bdot(w_in, w2_ref[d]))) - 0.5
        lw_ref[d] = -jnp.exp(w_log)
        ic_ref[d] = _sigmoid(a0_ref[d] + _bdot(a_in, a2_ref[d]))
    gt_ref[...] = _bdot(g_in, g2_ref[...])


def _rwkv_lora(pam, w2bd, a2bd, g2pad, w0, a0, tm=512):
    m = pam.shape[0]
    lo_blk = 3 * C_A // C_LO
    return pl.pallas_call(
        _rwkv_lora_kernel,
        grid=(m // tm,),
        in_specs=[
            pl.BlockSpec((tm, C_LO), lambda i: (i, lo_blk)),
            pl.BlockSpec((2, 2 * LORA, C_A), lambda i: (0, 0, 0)),
            pl.BlockSpec((2, 2 * LORA, C_A), lambda i: (0, 0, 0)),
            pl.BlockSpec((C_LO - 4 * LORA, C_A), lambda i: (0, 0)),
            pl.BlockSpec((2, 1, C_A), lambda i: (0, 0, 0)),
            pl.BlockSpec((2, 1, C_A), lambda i: (0, 0, 0)),
        ],
        out_specs=[
            pl.BlockSpec((2, tm, C_A), lambda i: (0, i, 0)),
            pl.BlockSpec((2, tm, C_A), lambda i: (0, i, 0)),
            pl.BlockSpec((tm, C_A), lambda i: (i, 0)),
        ],
        out_shape=[
            jax.ShapeDtypeStruct((2, m, C_A), F32),
            jax.ShapeDtypeStruct((2, m, C_A), F32),
            jax.ShapeDtypeStruct((m, C_A), F32),
        ],
        compiler_params=_cparams("parallel"),
        name="rwkv_lora",
    )(pam, w2bd, a2bd, g2pad, w0, a0)


def _order_masks(n, reverse):
    rr = lax.broadcasted_iota(jnp.int32, (n, n), 0)
    cc = lax.broadcasted_iota(jnp.int32, (n, n), 1)
    if reverse:
        return cc >= rr, cc > rr
    return cc <= rr, cc < rr


def _run_lockstep(chains):
    results = [None] * len(chains)
    pending = {i: next(c) for i, c in enumerate(chains)}
    while pending:
        products = {i: fn(a, b, dims) for i, (fn, a, b, dims) in pending.items()}
        pending = {}
        for i, prod in products.items():
            try:
                pending[i] = chains[i].send(prod)
            except StopIteration as stop:
                results[i] = stop.value
    return results


def _triangular_inverse(l, order):
    n = l.shape[0]
    ri = lax.broadcasted_iota(jnp.int32, (n, n), 0)
    ci = lax.broadcasted_iota(jnp.int32, (n, n), 1)

    def joins(k, g):
        return (ri // (g * k) == ci // (g * k)) & ((ri // k) != (ci // k))

    x = (ri == ci).astype(F32) + jnp.where(joins(1, 2), l, 0.0)
    k = 2
    while k < order:
        g = 4 if 4 * k <= order else 2
        nil = yield _bdot, x, jnp.where(joins(k, g), l, 0.0), NN
        x = x + (yield _bdot, nil, x, NN)
        if g == 4:
            nil2 = yield _bdot, nil, nil, NN
            x = x + (yield _bdot, nil2, x, NN)
        k *= g
    return x


def _split_heads(x, first):
    return jnp.concatenate([jnp.where(first, x, 0.0), jnp.where(first, 0.0, x)], 0)


def _chunk_cumsum(x, reverse, chunk):
    t = x.shape[0]
    pos = lax.broadcasted_iota(jnp.int32, x.shape, 0) % chunk
    s = 1
    while s < chunk:
        if reverse:
            x = x + jnp.where(pos < chunk - s, pltpu.roll(x, t - s, 0), 0.0)
        else:
            x = x + jnp.where(pos >= s, pltpu.roll(x, s, 0), 0.0)
        s *= 2
    return x


def _rwkv_chunk(s_bd, r, lw, cum, k, v, a, b, reverse):
    c = CHUNK
    cum_ex = cum - lw
    tot = cum[0:1] if reverse else cum[c - 1:c]
    e_neg = jnp.exp(-cum)
    e_hat = jnp.exp(tot - cum)
    at = a * jnp.exp(cum_ex)
    rt = r * jnp.exp(cum)
    first = lax.broadcasted_iota(jnp.int32, (c, LANES), 1) < HEAD_A
    lhs = jnp.concatenate([at, rt], 0)
    rbk = jnp.concatenate([_split_heads(b * e_neg, first), _split_heads(k * e_neg, first)], 0)
    g = yield _bdot, lhs, rbk, NT
    ri_g = lax.broadcasted_iota(jnp.int32, (2 * c, 4 * c), 0)
    pos_c = lax.broadcasted_iota(jnp.int32, (2 * c, 4 * c), 1) % c
    pos_r = ri_g % c
    own = jnp.where(ri_g < c, 0, 1)
    keep = (pos_c > pos_r - own) if reverse else (pos_c < pos_r + own)
    g = jnp.where(keep, g, 0.0)
    l_ab = g[0:c, 0:2 * c]
    l_ak = g[0:c, 2 * c:4 * c]
    m_r = g[c:2 * c, :]
    v_bd = _split_heads(v, first)
    sa = yield _bdot, lhs, s_bd, NT
    rhs = sa[0:c] + (yield _bdot, l_ak, v_bd, NN)
    l_bd = _split_heads(l_ab, first)
    t_inv = yield from _triangular_inverse(l_bd, c)
    u_bd = yield _bdot, t_inv, _split_heads(rhs, first), NN
    y = sa[c:2 * c] + (yield _bdot, m_r, jnp.concatenate([u_bd, v_bd], 0), NN)
    u = u_bd[0:c] + u_bd[c:2 * c]
    upd = yield _bdot, jnp.concatenate([u, v], 0), jnp.concatenate([b * e_hat, k * e_hat], 0), TN
    ri = lax.broadcasted_iota(jnp.int32, (LANES, LANES), 0) < HEAD_A
    ci = lax.broadcasted_iota(jnp.int32, (LANES, LANES), 1) < HEAD_A
    s_new = s_bd * jnp.exp(tot) + jnp.where(ri == ci, upd, 0.0)
    return y, s_new


def _head_sum(x, first):
    s0 = jnp.sum(jnp.where(first, x, 0.0), -1, keepdims=True)
    s1 = jnp.sum(jnp.where(first, 0.0, x), -1, keepdims=True)
    return jnp.where(first, s0, s1)


def _rwkv_scan_kernel(r_ref, k_ref, v_ref, lw_ref, ic_ref, gt_ref, kk_ref, ka_ref, rk_ref, lnw_ref, lnb_ref,
                      s0_ref, o_ref, sf_ref, acc_ref, cum_ref, *, t, pairs):
    n = t // CHUNK
    first = lax.broadcasted_iota(jnp.int32, (CHUNK, LANES), 1) < HEAD_A
    for d in range(2):
        cum_ref[d] = _chunk_cumsum(lw_ref[d], d == 1, CHUNK)
    sf_ref[...] = s0_ref[...]

    def body(i, carry):
        where, chains = [], []
        for d in range(2):
            reverse = d == 1
            ci = (n - 1 - i) if reverse else i
            rows = pl.ds(pl.multiple_of(ci * CHUNK, CHUNK), CHUNK)
            for p in range(pairs):
                lanes = slice(p * LANES, (p + 1) * LANES)
                r = r_ref[rows, lanes]
                k = k_ref[rows, lanes]
                v = v_ref[rows, lanes]
                iclr = ic_ref[d, rows, lanes]
                kk = k * kk_ref[:, lanes]
                kk = kk * lax.rsqrt(_head_sum(kk * kk, first) + 1e-12)
                k_d = k * (1.0 + (iclr - 1.0) * ka_ref[:, lanes])
                where.append((d, p, rows, lanes, r, k_d, v))
                chains.append(_rwkv_chunk(sf_ref[d, p], r, lw_ref[d, rows, lanes], cum_ref[d, rows, lanes],
                                          k_d, v, -kk, kk * iclr, reverse))
        for (d, p, rows, lanes, r, k_d, v), (y, s_new) in zip(where, _run_lockstep(chains)):
            sf_ref[d, p] = s_new
            mu = _head_sum(y, first) * (1.0 / HEAD_A)
            yc = y - mu
            var = _head_sum(yc * yc, first) * (1.0 / HEAD_A)
            out = yc * lax.rsqrt(var + GN_EPS) * lnw_ref[:, lanes] + lnb_ref[:, lanes]
            acc_ref[d, rows, lanes] = out + _head_sum(r * k_d * rk_ref[:, lanes], first) * v
        return carry

    lax.fori_loop(0, n, body, 0)
    o_ref[...] = ((acc_ref[0] + acc_ref[1]) * gt_ref[...]).astype(o_ref.dtype)


def _rwkv_scan(pam, lw, ic, gate, k_k, k_a, r_k, ln_w, ln_b, s0_bd, n_seq, t, row_blk0, pairs=4):
    w = pairs * LANES
    nb = C_A // w
    vec = lambda: pl.BlockSpec((1, w), lambda b, h: (0, h))
    rows = lambda off: pl.BlockSpec((t, w), lambda b, h: (row_blk0 + b, off + h))
    rows2 = lambda: pl.BlockSpec((2, t, w), lambda b, h: (0, row_blk0 + b, h))
    state = pl.BlockSpec((None, 2, pairs, LANES, LANES), lambda b, h: (b, 0, h, 0, 0))
    return pl.pallas_call(
        functools.partial(_rwkv_scan_kernel, t=t, pairs=pairs),
        grid=(n_seq, nb),
        in_specs=[
            rows(0), rows(nb), rows(2 * nb), rows2(), rows2(), rows(0),
            vec(), vec(), vec(), vec(), vec(),
            state,
        ],
        out_specs=[pl.BlockSpec((t, w), lambda b, h: (b, h)), state],
        out_shape=[
            jax.ShapeDtypeStruct((n_seq * t, C_A), BF16),
            jax.ShapeDtypeStruct((n_seq, 2, C_A // LANES, LANES, LANES), F32),
        ],
        scratch_shapes=[pltpu.VMEM((2, t, w), F32), pltpu.VMEM((2, t, w), F32)],
        compiler_params=_cparams("parallel", "parallel"),
        name="rwkv_scan",
    )(pam, pam, pam, lw, ic, gate, k_k, k_a, r_k, ln_w, ln_b, s0_bd)


def _gdn_chunk(s, q, k, v, gc, beta, reverse):
    c = q.shape[0]
    incl, strict = _order_masks(c, reverse)
    pick = (lax.broadcasted_iota(jnp.int32, (8, LANES), 1) == 0).astype(F32)
    gr = (yield _fdot, pick, jnp.broadcast_to(gc, (c, LANES)), NT)[0:1, :]
    decay = jnp.where(incl, jnp.exp(jnp.where(incl, gc - gr, 0.0)), 0.0)
    g_last = gc[0:1] if reverse else gc[c - 1:c]
    kb = k * beta
    l = jnp.where(strict, (yield _bdot, kb, k, NT) * decay, 0.0)
    attn = (yield _bdot, q, k, NT) * decay
    a_inv = yield from _triangular_inverse(-l, c)
    eg = jnp.exp(gc)
    sol = yield _bdot, a_inv, jnp.concatenate([v * beta, kb * eg], -1), NN
    u, w = sol[:, 0:HEAD_B], sol[:, HEAD_B:]
    ws = yield _bdot, jnp.concatenate([w, q * eg], 0), s, NN
    v_new = u - ws[0:c]
    o = ws[c:2 * c] + (yield _bdot, attn, v_new, NN)
    s_new = s * jnp.exp(g_last) + (yield _bdot, k * jnp.exp(g_last - gc), v_new, TN)
    return o, s_new


def _gdn_scan_kernel(q_ref, k_ref, v_ref, z_ref, ba_ref, cq_ref, ck_ref, cv_ref, gnw_ref, alog_ref, dtb_ref,
                     s0_ref, o_ref, sf_ref, qs_ref, ks_ref, vs_ref, acc_ref, gc_ref, beta_ref, *, t, heads):
    n = t // CHUNK_B
    hg = pl.program_id(1)
    blk = CHUNK_B
    brow = lax.broadcasted_iota(jnp.int32, (blk, LANES), 0)
    no_row = jnp.zeros((1, LANES), F32)

    def conv_silu(x_ref, w, r0, lanes):
        x = x_ref[r0:r0 + blk, lanes]
        before = x_ref[r0 - 1:r0, lanes] if r0 > 0 else no_row
        after = x_ref[r0 + blk:r0 + blk + 1, lanes] if r0 + blk < t else no_row
        prev = jnp.where(brow == 0, before, pltpu.roll(x, 1, 0))
        nxt = jnp.where(brow == blk - 1, after, pltpu.roll(x, blk - 1, 0))
        return _silu(w[0:1] * prev + w[1:2] * x + w[2:3] * nxt)

    def l2norm(x):
        return x * lax.rsqrt(jnp.sum(x * x, -1, keepdims=True) + 1e-6)

    for g in range(heads):
        lanes = slice(g * LANES, (g + 1) * LANES)
        wq, wk, wv = cq_ref[:, lanes], ck_ref[:, lanes], cv_ref[:, lanes]
        for r0 in range(0, t, blk):
            rows = slice(r0, r0 + blk)
            qs_ref[rows, lanes] = l2norm(conv_silu(q_ref, wq, r0, lanes)) * (HEAD_B ** -0.5)
            ks_ref[rows, lanes] = l2norm(conv_silu(k_ref, wk, r0, lanes))
            vs_ref[rows, lanes] = conv_silu(v_ref, wv, r0, lanes)
    ba = ba_ref[...]
    beta_ref[...] = _sigmoid(ba)
    g_log = -jnp.exp(alog_ref[...]) * _softplus(ba + dtb_ref[...])
    gc_ref[0] = _chunk_cumsum(g_log, False, CHUNK_B)
    gc_ref[1] = _chunk_cumsum(g_log, True, CHUNK_B)
    sf_ref[...] = s0_ref[...]
    lane = lax.broadcasted_iota(jnp.int32, (CHUNK_B, LANES), 1)

    def body(i, carry):
        chains = []
        for d in range(2):
            ci = (n - 1 - i) if d == 1 else i
            rows = pl.ds(pl.multiple_of(ci * CHUNK_B, CHUNK_B), CHUNK_B)
            gcs = gc_ref[d, rows, :]
            bts = beta_ref[rows, :]
            for g in range(heads):
                lanes = slice(g * LANES, (g + 1) * LANES)
                chains.append((d, g, rows, lanes, gcs, bts, sf_ref[d, g],
                               qs_ref[rows, lanes], ks_ref[rows, lanes], vs_ref[rows, lanes]))
        gens = []
        for d, g, rows, lanes, gcs, bts, s, q, k, v in chains:
            col = d * H_B + hg * heads + g
            gc = jnp.sum(jnp.where(lane == BA_LANE0 + 2 * H_B + col, gcs, 0.0), -1, keepdims=True)
            beta = jnp.sum(jnp.where(lane == BA_LANE0 + col, bts, 0.0), -1, keepdims=True)
            gens.append(_gdn_chunk(s, q, k, v, gc, beta, d == 1))
        for (d, g, rows, lanes, *_), (o, s_new) in zip(chains, _run_lockstep(gens)):
            sf_ref[d, g] = s_new
            acc_ref[d, rows, lanes] = o
        return carry

    lax.fori_loop(0, n, body, 0)
    for g in range(heads):
        lanes = slice(g * LANES, (g + 1) * LANES)
        for r0 in range(0, t, blk):
            rows = slice(r0, r0 + blk)
            o = acc_ref[0, rows, lanes] + acc_ref[1, rows, lanes]
            o = o * lax.rsqrt(jnp.mean(o * o, -1, keepdims=True) + RMS_EPS) * gnw_ref[...]
            o_ref[rows, lanes] = (o * _silu(z_ref[rows, lanes])).astype(o_ref.dtype)


def _gdn_scan(p, conv_w, alog_vec, dtb_vec, gn_w, s0, n_seq, t, row_blk0, heads=4):
    w = heads * LANES
    nb = C_B // w
    first = C_PA_PAD // w
    rows = lambda off: pl.BlockSpec((t, w), lambda b, h: (row_blk0 + b, first + off + h))
    cw = lambda off: pl.BlockSpec((3, w), lambda b, h: (0, off + h))
    vec = pl.BlockSpec((1, LANES), lambda b, h: (0, 0))
    state = pl.BlockSpec((None, 2, heads, HEAD_B, HEAD_B), lambda b, h: (b, 0, h, 0, 0))
    return pl.pallas_call(
        functools.partial(_gdn_scan_kernel, t=t, heads=heads),
        grid=(n_seq, nb),
        in_specs=[
            rows(0), rows(nb), rows(2 * nb), rows(3 * nb),
            pl.BlockSpec((t, LANES), lambda b, h: (row_blk0 + b, C_PA // LANES)),
            cw(0), cw(nb), cw(2 * nb),
            vec, vec, vec,
            state,
        ],
        out_specs=[pl.BlockSpec((t, w), lambda b, h: (b, h)), state],
        out_shape=[
            jax.ShapeDtypeStruct((n_seq * t, C_B), BF16),
            jax.ShapeDtypeStruct((n_seq, 2, H_B, HEAD_B, HEAD_B), F32),
        ],
        scratch_shapes=[pltpu.VMEM((t, w), F32)] * 3 + [pltpu.VMEM((2, t, w), F32), pltpu.VMEM((2, t, LANES), F32),
                                                         pltpu.VMEM((t, LANES), F32)],
        compiler_params=_cparams("parallel", "parallel"),
        name="gdn_scan",
    )(p, p, p, p, p, conv_w, conv_w, conv_w, gn_w, alog_vec, dtb_vec, s0)


def _pool_kernel(h_ref, x_ref, w_ref, sc_ref, g_ref, o_ref, *, ctx_blocks, t_ctx):
    gi = pl.program_id(1)
    h = h_ref[...]
    t = jnp.where(pl.program_id(0) < ctx_blocks, t_ctx, h.shape[0])
    row = lax.broadcasted_iota(jnp.int32, h.shape, 0) & (t - 1)

    def back(x, n):
        return _shift_rows(x, n, row >= n)

    def fwd(x, n):
        return _shift_rows(x, -n, row < t - n)

    p2 = back(h, 1) + h
    q2 = h + fwd(h, 1)
    p4 = back(p2, 2) + p2
    q4 = q2 + fwd(q2, 2)
    p8 = back(p4, 4) + p4
    q8 = q4 + fwd(q4, 4)
    p_h = jnp.where(gi == 0, h, jnp.where(gi == 1, p2, jnp.where(gi == 2, p4, p8)))
    q_h = jnp.where(gi == 0, h, jnp.where(gi == 1, q2, jnp.where(gi == 2, q4, q8)))
    s = back(p_h, 1) + q_h
    half = jnp.left_shift(1, gi)
    lo = jnp.clip(row - half, 0, t)
    hi = jnp.clip(row + half, 0, t)
    p = s / (hi - lo).astype(F32) - h
    y = _bdot(p, w_ref[...]) * sc_ref[...]
    o_ref[...] = x_ref[...] + g_ref[...] * y


def _pool_mixer(h, x, w_pool, layer, scale, gate_blk, t, ctx_blocks, t_ctx):
    m = x.shape[0]
    return pl.pallas_call(
        functools.partial(_pool_kernel, ctx_blocks=ctx_blocks, t_ctx=t_ctx),
        grid=(m // t, POOL_GROUPS),
        in_specs=[
            pl.BlockSpec((t, C_G), lambda b, g: (b, g)),
            pl.BlockSpec((t, C_G), lambda b, g: (b, g)),
            pl.BlockSpec((None, None, C_G, C_G), lambda b, g: (layer, g, 0, 0)),
            pl.BlockSpec((1, C_G), lambda b, g: (0, g)),
            pl.BlockSpec((None, 1, C_G), lambda b, g: (b, 0, g)),
        ],
        out_specs=pl.BlockSpec((t, C_G), lambda b, g: (b, g)),
        out_shape=jax.ShapeDtypeStruct((m, D_MODEL), F32),
        compiler_params=_cparams("parallel", "parallel"),
        name="pool_mixer",
    )(h, x, w_pool, scale, gate_blk)


def _pad_cols(w, n):
    return jnp.pad(w, ((0, 0), (0, n - w.shape[1])))


def _block_diag_pairs(s):
    b, two, h, n, _ = s.shape
    sp = s.reshape(b, two, h // 2, 2, n, n)
    z = jnp.zeros_like(sp[:, :, :, 0])
    top = jnp.concatenate([sp[:, :, :, 0], z], -1)
    bot = jnp.concatenate([z, sp[:, :, :, 1]], -1)
    return jnp.concatenate([top, bot], -2)


def _diag_pairs(s_bd):
    b, two, hp, _, _ = s_bd.shape
    n = HEAD_A
    return jnp.stack([s_bd[..., :n, :n], s_bd[..., n:, n:]], 3).reshape(b, two, 2 * hp, n, n)


def kernel(x_prompt, x_sample, state_rwkv, state_delta, c, c_ctx, mod_w, mod_b, norm_mix_w, norm_ffn_w, norm_final_w, ab_w_in, ab_w_out, rwkv_mu, rwkv_w0, rwkv_w2, rwkv_a0, rwkv_a2, rwkv_g2, rwkv_k_k, rwkv_k_a, rwkv_r_k, rwkv_ln_w, rwkv_ln_b, gdn_conv_w, gdn_a_log, gdn_dt_bias, gdn_norm_w, pool_w, pool_scale, ffn_w_gate, ffn_w_up, ffn_w_down):
    bp, tp, d = x_prompt.shape
    bs, ts, _ = x_sample.shape
    mp, ms = bp * tp, bs * ts
    m = mp + ms
    assert tp == ROW_BLK and ts % ROW_BLK == 0 and mp % ts == 0
    x = jnp.concatenate([x_prompt.reshape(mp, d), x_sample.reshape(ms, d)], 0)

    n_cond = 16
    cond = jnp.zeros((n_cond, d), F32).at[0].set(c_ctx).at[1:1 + bs].set(c)
    mod = _mod_all(cond, mod_w, mod_b).reshape(DEPTH, n_cond, 6, d)
    mod_blk = jnp.concatenate([
        jnp.broadcast_to(mod[:, 0:1], (DEPTH, mp // ROW_BLK, 6, d)),
        jnp.repeat(mod[:, 1:1 + bs], ts // ROW_BLK, axis=1)], 1)[:, :, :, None, :]
    zeros_blk = jnp.zeros((m // ROW_BLK, 1, d), F32)

    zero_r = jnp.zeros((bp, 2, H_A // 2, LANES, LANES), F32)
    zero_d = jnp.zeros((bp, 2, H_B, HEAD_B, HEAD_B), F32)
    new_r, new_d = [], []

    for l in range(DEPTH):
        shift_m, scale_m, gate_m, shift_f, scale_f, gate_f = (mod_blk[l, :, j] for j in range(6))
        i = l // 2
        if l % 2 == 0:
            w_in = ab_w_in[i]
            w_p = jnp.concatenate([
                w_in[:, :C_PA], w_in[:, C_PA + 4 * C_B:], jnp.zeros((d, C_PA_PAD - C_PA - 4 * H_B), F32),
                w_in[:, C_PA:C_PA + 4 * C_B]], 1).astype(BF16)
            p = _norm_matmul(x, norm_mix_w[l], scale_m, shift_m, w_p)
            mu = _pad_cols(rwkv_mu[i][None, :], C_PA_PAD)
            pam = _token_shift(p, mu, ts, mp // ts, tp)
            w2bd = jnp.zeros((2, 2 * LORA, C_A), F32).at[0, :LORA].set(rwkv_w2[i, 0]).at[1, LORA:].set(rwkv_w2[i, 1])
            a2bd = jnp.zeros((2, 2 * LORA, C_A), F32).at[0, :LORA].set(rwkv_a2[i, 0]).at[1, LORA:].set(rwkv_a2[i, 1])
            g2pad = jnp.pad(rwkv_g2[i], ((0, C_LO - 4 * LORA - GATE_LORA), (0, 0)))
            lw, ic, gate = _rwkv_lora(pam, w2bd.astype(BF16), a2bd.astype(BF16), g2pad.astype(BF16),
                                      rwkv_w0[i][:, None, :], rwkv_a0[i][:, None, :])
            vecs = [v_[i].reshape(1, C_A) for v_ in (rwkv_k_k, rwkv_k_a, rwkv_r_k, rwkv_ln_w, rwkv_ln_b)]
            oa_p, sr_p = _rwkv_scan(pam, lw, ic, gate, *vecs, zero_r, bp, tp, 0)
            oa_s, _ = _rwkv_scan(pam, lw, ic, gate, *vecs, _block_diag_pairs(state_rwkv[:, i]), bs, ts, mp // ts)
            gnw = gdn_norm_w[i].reshape(1, HEAD_B)
            on_decay_lanes = lambda p_: jnp.zeros((1, LANES), F32).at[
                0, BA_LANE0 + 2 * H_B:BA_LANE0 + 4 * H_B].set(p_.reshape(2 * H_B))
            alog_vec, dtb_vec = on_decay_lanes(gdn_a_log[i]), on_decay_lanes(gdn_dt_bias[i])
            ob_p, sd_p = _gdn_scan(p, gdn_conv_w[i], alog_vec, dtb_vec, gnw, zero_d, bp, tp, 0)
            ob_s, _ = _gdn_scan(p, gdn_conv_w[i], alog_vec, dtb_vec, gnw, state_delta[:, i],
                                bs, ts, mp // ts)
            new_r.append(_diag_pairs(sr_p))
            new_d.append(sd_p)
            o_a = jnp.concatenate([oa_p, oa_s], 0)
            o_b = jnp.concatenate([ob_p, ob_s], 0)
            x = _matmul_residual([o_a, o_b], ab_w_out, i, x, gate_m)
        else:
            h = _norm_mod(x, norm_mix_w[l], scale_m, shift_m, F32)
            gate_blk = jnp.concatenate([jnp.broadcast_to(mod[l, 0:1, 2], (mp // ts, d)), mod[l, 1:1 + bs, 2]], 0)
            x = _pool_mixer(h, x, pool_w, i, pool_scale[i].reshape(1, d), gate_blk[:, None, :], ts, mp // ts, tp)
        act = _ffn_up(x, norm_ffn_w[l], scale_f, shift_f, ffn_w_gate, ffn_w_up, l)
        x = _matmul_residual([act], ffn_w_down, l, x, gate_f, single_buffer_w=True)

    y = _norm_mod(x, norm_final_w, zeros_blk, zeros_blk, F32)
    y_prompt = y[:mp].reshape(bp, tp, d)
    y_sample = y[mp:].reshape(bs, ts, d)
    return (y_prompt, y_sample, jnp.stack(new_r, 1), jnp.stack(new_d, 1))
```

```python
import functools

import jax
import jax.numpy as jnp
from jax import lax
from jax.experimental import pallas as pl
from jax.experimental.pallas import tpu as pltpu

F32 = jnp.float32
BF16 = jnp.bfloat16

D_MODEL = 2048
DEPTH = 4
GRID_W = 64
N_AB = 2
C_A = 1024
HEAD_A = 64
H_A = 16
LORA = 64
GATE_LORA = 160
C_B = 1024
HEAD_B = 128
H_B = 8
C_PA = 3 * C_A + 4 * LORA + GATE_LORA
C_PA_PAD = 3584
C_LO = C_PA_PAD - 3 * C_A
POOL_GROUPS = 4
C_G = D_MODEL // POOL_GROUPS
D_FF = 5632
RMS_EPS = 1e-6
GN_EPS = 64e-5
CHUNK = 64
CHUNK_B = 128
ROW_BLK = 256
LANES = 128
BA_LANE0 = C_PA % LANES
VMEM_LIMIT = 48 * 1024 * 1024

NT = (((1,), (1,)), ((), ()))
TN = (((0,), (0,)), ((), ()))
NN = (((1,), (0,)), ((), ()))


def _cparams(*sem):
    return pltpu.CompilerParams(dimension_semantics=sem, vmem_limit_bytes=VMEM_LIMIT)


def _bdot(a, b, dims=NN):
    return lax.dot_general(a.astype(BF16), b.astype(BF16), dims, preferred_element_type=F32)


def _fdot(a, b, dims=NN):
    return lax.dot_general(a, b, dims, precision=lax.Precision.HIGHEST, preferred_element_type=F32)


def _sigmoid(x):
    return 0.5 + 0.5 * jnp.tanh(0.5 * x)


def _silu(x):
    return x * _sigmoid(x)


def _softplus(x):
    return jnp.maximum(x, 0.0) + jnp.log(1.0 + jnp.exp(-jnp.abs(x)))


def _mod_kernel(c_ref, w_ref, b_ref, o_ref):
    act = _silu(c_ref[...])
    w = w_ref[...]
    a_hi = act.astype(BF16)
    a_lo = (act - a_hi.astype(F32)).astype(BF16)
    w_hi = w.astype(BF16)
    w_lo = (w - w_hi.astype(F32)).astype(BF16)
    n_c = act.shape[0]
    both = jnp.dot(jnp.concatenate([a_hi, a_lo], 0), w_hi, preferred_element_type=F32)
    o_ref[...] = both[:n_c] + both[n_c:] + jnp.dot(a_hi, w_lo, preferred_element_type=F32) + b_ref[...]


def _mod_all(cond, mod_w, mod_b):
    n_c, d = cond.shape
    depth, _, n = mod_w.shape
    tn = 1024
    return pl.pallas_call(
        _mod_kernel,
        grid=(depth, n // tn),
        in_specs=[
            pl.BlockSpec((n_c, d), lambda l, j: (0, 0)),
            pl.BlockSpec((None, d, tn), lambda l, j: (l, 0, j)),
            pl.BlockSpec((None, 1, tn), lambda l, j: (l, 0, j)),
        ],
        out_specs=pl.BlockSpec((None, n_c, tn), lambda l, j: (l, 0, j)),
        out_shape=jax.ShapeDtypeStruct((depth, n_c, n), F32),
        compiler_params=_cparams("parallel", "parallel"),
        name="mod_all",
    )(cond, mod_w, mod_b.reshape(depth, 1, n))


def _norm_mod_kernel(x_ref, nw_ref, sc_ref, sh_ref, o_ref):
    x = x_ref[...]
    ms = jnp.mean(x * x, -1, keepdims=True)
    y = x * lax.rsqrt(ms + RMS_EPS) * nw_ref[...]
    o_ref[...] = (y * (1.0 + sc_ref[...]) + sh_ref[...]).astype(o_ref.dtype)


def _norm_mod(x, nw, scale, shift, out_dtype, blk0=0, n_blk=None):
    m, d = x.shape
    nb = m // ROW_BLK if n_blk is None else n_blk
    return pl.pallas_call(
        _norm_mod_kernel,
        grid=(nb,),
        in_specs=[
            pl.BlockSpec((ROW_BLK, d), lambda i: (blk0 + i, 0)),
            pl.BlockSpec((1, d), lambda i: (0, 0)),
            pl.BlockSpec((None, 1, d), lambda i: (blk0 + i, 0, 0)),
            pl.BlockSpec((None, 1, d), lambda i: (blk0 + i, 0, 0)),
        ],
        out_specs=pl.BlockSpec((ROW_BLK, d), lambda i: (i, 0)),
        out_shape=jax.ShapeDtypeStruct((nb * ROW_BLK, d), out_dtype),
        compiler_params=_cparams("parallel"),
        name="norm_mod",
    )(x, nw.reshape(1, d), scale, shift)


def _norm_mod_rows(x_ref, nw_ref, sc_ref, sh_ref, s):
    x = x_ref[s * ROW_BLK:(s + 1) * ROW_BLK, :]
    y = x * lax.rsqrt(jnp.mean(x * x, -1, keepdims=True) + RMS_EPS) * nw_ref[...]
    return (y * (1.0 + sc_ref[s]) + sh_ref[s]).astype(BF16)


def _norm_mm_kernel(x_ref, nw_ref, sc_ref, sh_ref, b_ref, o_ref, h_ref, *, tm):
    @pl.when(pl.program_id(1) == 0)
    def _():
        for s in range(tm // ROW_BLK):
            h_ref[s * ROW_BLK:(s + 1) * ROW_BLK, :] = _norm_mod_rows(x_ref, nw_ref, sc_ref, sh_ref, s)

    o_ref[...] = jnp.dot(h_ref[...], b_ref[...], preferred_element_type=F32)


def _norm_matmul(x, nw, scale, shift, b, tm=1024, tn=512):
    m, d = x.shape
    n = b.shape[1]
    sub = tm // ROW_BLK
    return pl.pallas_call(
        functools.partial(_norm_mm_kernel, tm=tm),
        grid=(m // tm, n // tn),
        in_specs=[
            pl.BlockSpec((tm, d), lambda i, j: (i, 0)),
            pl.BlockSpec((1, d), lambda i, j: (0, 0)),
            pl.BlockSpec((sub, 1, d), lambda i, j: (i, 0, 0)),
            pl.BlockSpec((sub, 1, d), lambda i, j: (i, 0, 0)),
            pl.BlockSpec((d, tn), lambda i, j: (0, j)),
        ],
        out_specs=pl.BlockSpec((tm, tn), lambda i, j: (i, j)),
        out_shape=jax.ShapeDtypeStruct((m, n), F32),
        scratch_shapes=[pltpu.VMEM((tm, d), BF16)],
        compiler_params=_cparams("parallel", "arbitrary"),
        name="norm_matmul",
    )(x, nw.reshape(1, d), scale, shift, b)


def _gated_residual(x_ref, g_ref, acc, o_ref, tm):
    for s in range(tm // ROW_BLK):
        rows = slice(s * ROW_BLK, (s + 1) * ROW_BLK)
        o_ref[rows, :] = x_ref[rows, :] + g_ref[s] * acc[rows, :]


def _cast_weight_tiles(w_refs, wb_refs):
    @pl.when(pl.program_id(1) == 0)
    def _():
        for w_ref, wb_ref in zip(w_refs, wb_refs):
            wb_ref[...] = w_ref[...].astype(BF16)


def _mm_res_kernel(*refs, n_in, tm):
    a_refs, w_refs = refs[:n_in], refs[n_in:2 * n_in]
    x_ref, g_ref, o_ref = refs[2 * n_in:2 * n_in + 3]
    wb_refs = refs[2 * n_in + 3:]
    _cast_weight_tiles(w_refs, wb_refs)
    acc = jnp.dot(a_refs[0][...], wb_refs[0][...], preferred_element_type=F32)
    for a_ref, wb_ref in zip(a_refs[1:], wb_refs[1:]):
        acc = acc + jnp.dot(a_ref[...], wb_ref[...], preferred_element_type=F32)
    _gated_residual(x_ref, g_ref, acc, o_ref, tm)


def _matmul_residual(a_list, w, layer, x, gate, tm=512, tn=512, single_buffer_w=False):
    m, n = x.shape
    sub = tm // ROW_BLK
    kb = w.shape[1] // len(a_list)
    w_mode = dict(pipeline_mode=pl.Buffered(1)) if single_buffer_w else {}
    in_specs = [pl.BlockSpec((tm, kb), lambda j, i: (i, 0)) for _ in a_list]
    in_specs += [pl.BlockSpec((None, kb, tn), functools.partial(lambda j, i, r: (layer, r, j), r=r), **w_mode)
                 for r in range(len(a_list))]
    in_specs += [
        pl.BlockSpec((tm, tn), lambda j, i: (i, j)),
        pl.BlockSpec((sub, 1, tn), lambda j, i: (i, 0, j)),
    ]
    return pl.pallas_call(
        functools.partial(_mm_res_kernel, n_in=len(a_list), tm=tm),
        grid=(n // tn, m // tm),
        in_specs=in_specs,
        out_specs=pl.BlockSpec((tm, tn), lambda j, i: (i, j)),
        out_shape=jax.ShapeDtypeStruct((m, n), F32),
        scratch_shapes=[pltpu.VMEM((kb, tn), BF16) for _ in a_list],
        compiler_params=_cparams("parallel", "arbitrary"),
        name="matmul_residual",
    )(*a_list, *([w] * len(a_list)), x, gate)


def _ffn_up_kernel(x_ref, nw_ref, sc_ref, sh_ref, wg_ref, wu_ref, o_ref, h_ref, *, tm):
    @pl.when(pl.program_id(1) == 0)
    def _():
        for s in range(tm // ROW_BLK):
            h_ref[s * ROW_BLK:(s + 1) * ROW_BLK, :] = _norm_mod_rows(x_ref, nw_ref, sc_ref, sh_ref, s)

    h = h_ref[...]
    g = jnp.dot(h, wg_ref[...].astype(BF16), preferred_element_type=F32)
    u = jnp.dot(h, wu_ref[...].astype(BF16), preferred_element_type=F32)
    o_ref[...] = (_silu(g) * u).astype(o_ref.dtype)


def _ffn_up(x, nw, scale, shift, wg, wu, layer, tm=1024, tn=256):
    m, d = x.shape
    n = wg.shape[2]
    sub = tm // ROW_BLK
    w_spec = lambda: pl.BlockSpec((None, d, tn), lambda i, j: (layer, 0, j))
    return pl.pallas_call(
        functools.partial(_ffn_up_kernel, tm=tm),
        grid=(m // tm, n // tn),
        in_specs=[
            pl.BlockSpec((tm, d), lambda i, j: (i, 0)),
            pl.BlockSpec((1, d), lambda i, j: (0, 0)),
            pl.BlockSpec((sub, 1, d), lambda i, j: (i, 0, 0)),
            pl.BlockSpec((sub, 1, d), lambda i, j: (i, 0, 0)),
            w_spec(), w_spec(),
        ],
        out_specs=pl.BlockSpec((tm, tn), lambda i, j: (i, j)),
        out_shape=jax.ShapeDtypeStruct((m, n), BF16),
        scratch_shapes=[pltpu.VMEM((tm, d), BF16)],
        compiler_params=_cparams("parallel", "arbitrary"),
        name="ffn_up",
    )(x, nw.reshape(1, d), scale, shift, wg, wu)


def _shift_rows(x, shift, valid):
    t = x.shape[0]
    return jnp.where(valid, pltpu.roll(x, shift % t, 0), 0.0)


def _token_shift_kernel(p_ref, mu_ref, o_ref, *, ctx_blocks, t_ctx):
    x = p_ref[...]
    t, c = x.shape
    row = lax.broadcasted_iota(jnp.int32, (t, c), 0)
    lane = lax.broadcasted_iota(jnp.int32, (t, c), 1)
    mu = mu_ref[...]

    @pl.when(pl.program_id(0) < ctx_blocks)
    def _():
        pos = row % t_ctx
        s = jnp.where(lane % 2 == 0, _shift_rows(x, 1, pos != 0), _shift_rows(x, -1, pos != t_ctx - 1))
        o_ref[...] = x + (s - x) * mu

    @pl.when(pl.program_id(0) >= ctx_blocks)
    def _():
        col = row % GRID_W
        sel = lane % 4
        s = jnp.where(sel == 0, _shift_rows(x, 1, col != 0),
            jnp.where(sel == 1, _shift_rows(x, -1, col != GRID_W - 1),
            jnp.where(sel == 2, _shift_rows(x, GRID_W, row >= GRID_W),
                      _shift_rows(x, -GRID_W, row < t - GRID_W))))
        o_ref[...] = x + (s - x) * mu


def _token_shift(p, mu, t, ctx_blocks, t_ctx):
    tc = 512
    m = p.shape[0]
    first = 4 * C_B // tc
    return pl.pallas_call(
        functools.partial(_token_shift_kernel, ctx_blocks=ctx_blocks, t_ctx=t_ctx),
        grid=(m // t, C_PA_PAD // tc),
        in_specs=[
            pl.BlockSpec((t, tc), lambda b, j: (b, first + j)),
            pl.BlockSpec((1, tc), lambda b, j: (0, j)),
        ],
        out_specs=pl.BlockSpec((t, tc), lambda b, j: (b, j)),
        out_shape=jax.ShapeDtypeStruct((m, C_PA_PAD), F32),
        compiler_params=_cparams("parallel", "parallel"),
        name="token_shift",
    )(p, mu)


def _rwkv_lora_kernel(lo_ref, w2_ref, a2_ref, g2_ref, w0_ref, a0_ref, lw_ref, ic_ref, gt_ref):
    lo = lo_ref[...]
    w_in = jnp.tanh(lo[:, 0:2 * LORA])
    a_in = lo[:, 2 * LORA:4 * LORA]
    g_in = _sigmoid(lo[:, 4 * LORA:])
    for d in range(2):
        w_log = -_softplus(-(w0_ref[d] + _bdot(w_in, w2_ref[d]))) - 0.5
        lw_ref[d] = -jnp.exp(w_log)
        ic_ref[d] = _sigmoid(a0_ref[d] + _bdot(a_in, a2_ref[d]))
    gt_ref[...] = _bdot(g_in, g2_ref[...])


def _rwkv_lora(pam, w2bd, a2bd, g2pad, w0, a0, tm=512):
    m = pam.shape[0]
    lo_blk = 3 * C_A // C_LO
    return pl.pallas_call(
        _rwkv_lora_kernel,
        grid=(m // tm,),
        in_specs=[
            pl.BlockSpec((tm, C_LO), lambda i: (i, lo_blk)),
            pl.BlockSpec((2, 2 * LORA, C_A), lambda i: (0, 0, 0)),
            pl.BlockSpec((2, 2 * LORA, C_A), lambda i: (0, 0, 0)),
            pl.BlockSpec((C_LO - 4 * LORA, C_A), lambda i: (0, 0)),
            pl.BlockSpec((2, 1, C_A), lambda i: (0, 0, 0)),
            pl.BlockSpec((2, 1, C_A), lambda i: (0, 0, 0)),
        ],
        out_specs=[
            pl.BlockSpec((2, tm, C_A), lambda i: (0, i, 0)),
            pl.BlockSpec((2, tm, C_A), lambda i: (0, i, 0)),
            pl.BlockSpec((tm, C_A), lambda i: (i, 0)),
        ],
        out_shape=[
            jax.ShapeDtypeStruct((2, m, C_A), F32),
            jax.ShapeDtypeStruct((2, m, C_A), F32),
            jax.ShapeDtypeStruct((m, C_A), F32),
        ],
        compiler_params=_cparams("parallel"),
        name="rwkv_lora",
    )(pam, w2bd, a2bd, g2pad, w0, a0)


def _order_masks(n, reverse):
    rr = lax.broadcasted_iota(jnp.int32, (n, n), 0)
    cc = lax.broadcasted_iota(jnp.int32, (n, n), 1)
    if reverse:
        return cc >= rr, cc > rr
    return cc <= rr, cc < rr


def _run_lockstep(chains):
    results = [None] * len(chains)
    pending = {i: next(c) for i, c in enumerate(chains)}
    while pending:
        products = {i: fn(a, b, dims) for i, (fn, a, b, dims) in pending.items()}
        pending = {}
        for i, prod in products.items():
            try:
                pending[i] = chains[i].send(prod)
            except StopIteration as stop:
                results[i] = stop.value
    return results


def _triangular_inverse(l, order):
    n = l.shape[0]
    ri = lax.broadcasted_iota(jnp.int32, (n, n), 0)
    ci = lax.broadcasted_iota(jnp.int32, (n, n), 1)

    def joins(k, g):
        return (ri // (g * k) == ci // (g * k)) & ((ri // k) != (ci // k))

    x = (ri == ci).astype(F32) + jnp.where(joins(1, 2), l, 0.0)
    k = 2
    while k < order:
        g = 4 if 4 * k <= order else 2
        nil = yield _bdot, x, jnp.where(joins(k, g), l, 0.0), NN
        x = x + (yield _bdot, nil, x, NN)
        if g == 4:
            nil2 = yield _bdot, nil, nil, NN
            x = x + (yield _bdot, nil2, x, NN)
        k *= g
    return x


def _split_heads(x, first):
    return jnp.concatenate([jnp.where(first, x, 0.0), jnp.where(first, 0.0, x)], 0)


def _chunk_cumsum(x, reverse, chunk):
    t = x.shape[0]
    pos = lax.broadcasted_iota(jnp.int32, x.shape, 0) % chunk
    s = 1
    while s < chunk:
        if reverse:
            x = x + jnp.where(pos < chunk - s, pltpu.roll(x, t - s, 0), 0.0)
        else:
            x = x + jnp.where(pos >= s, pltpu.roll(x, s, 0), 0.0)
        s *= 2
    return x


def _rwkv_chunk(s_bd, r, lw, cum, k, v, a, b, reverse):
    c = CHUNK
    cum_ex = cum - lw
    tot = cum[0:1] if reverse else cum[c - 1:c]
    e_neg = jnp.exp(-cum)
    e_hat = jnp.exp(tot - cum)
    at = a * jnp.exp(cum_ex)
    rt = r * jnp.exp(cum)
    first = lax.broadcasted_iota(jnp.int32, (c, LANES), 1) < HEAD_A
    lhs = jnp.concatenate([at, rt], 0)
    rbk = jnp.concatenate([_split_heads(b * e_neg, first), _split_heads(k * e_neg, first)], 0)
    g = yield _bdot, lhs, rbk, NT
    ri_g = lax.broadcasted_iota(jnp.int32, (2 * c, 4 * c), 0)
    pos_c = lax.broadcasted_iota(jnp.int32, (2 * c, 4 * c), 1) % c
    pos_r = ri_g % c
    own = jnp.where(ri_g < c, 0, 1)
    keep = (pos_c > pos_r - own) if reverse else (pos_c < pos_r + own)
    g = jnp.where(keep, g, 0.0)
    l_ab = g[0:c, 0:2 * c]
    l_ak = g[0:c, 2 * c:4 * c]
    m_r = g[c:2 * c, :]
    v_bd = _split_heads(v, first)
    sa = yield _bdot, lhs, s_bd, NT
    rhs = sa[0:c] + (yield _bdot, l_ak, v_bd, NN)
    l_bd = _split_heads(l_ab, first)
    t_inv = yield from _triangular_inverse(l_bd, c)
    u_bd = yield _bdot, t_inv, _split_heads(rhs, first), NN
    y = sa[c:2 * c] + (yield _bdot, m_r, jnp.concatenate([u_bd, v_bd], 0), NN)
    u = u_bd[0:c] + u_bd[c:2 * c]
    upd = yield _bdot, jnp.concatenate([u, v], 0), jnp.concatenate([b * e_hat, k * e_hat], 0), TN
    ri = lax.broadcasted_iota(jnp.int32, (LANES, LANES), 0) < HEAD_A
    ci = lax.broadcasted_iota(jnp.int32, (LANES, LANES), 1) < HEAD_A
    s_new = s_bd * jnp.exp(tot) + jnp.where(ri == ci, upd, 0.0)
    return y, s_new


def _head_sum(x, first):
    s0 = jnp.sum(jnp.where(first, x, 0.0), -1, keepdims=True)
    s1 = jnp.sum(jnp.where(first, 0.0, x), -1, keepdims=True)
    return jnp.where(first, s0, s1)


def _rwkv_scan_kernel(r_ref, k_ref, v_ref, lw_ref, ic_ref, gt_ref, kk_ref, ka_ref, rk_ref, lnw_ref, lnb_ref,
                      s0_ref, o_ref, sf_ref, acc_ref, cum_ref, *, t, pairs):
    n = t // CHUNK
    first = lax.broadcasted_iota(jnp.int32, (CHUNK, LANES), 1) < HEAD_A
    for d in range(2):
        cum_ref[d] = _chunk_cumsum(lw_ref[d], d == 1, CHUNK)
    sf_ref[...] = s0_ref[...]

    def body(i, carry):
        where, chains = [], []
        for d in range(2):
            reverse = d == 1
            ci = (n - 1 - i) if reverse else i
            rows = pl.ds(pl.multiple_of(ci * CHUNK, CHUNK), CHUNK)
            for p in range(pairs):
                lanes = slice(p * LANES, (p + 1) * LANES)
                r = r_ref[rows, lanes]
                k = k_ref[rows, lanes]
                v = v_ref[rows, lanes]
                iclr = ic_ref[d, rows, lanes]
                kk = k * kk_ref[:, lanes]
                kk = kk * lax.rsqrt(_head_sum(kk * kk, first) + 1e-12)
                k_d = k * (1.0 + (iclr - 1.0) * ka_ref[:, lanes])
                where.append((d, p, rows, lanes, r, k_d, v))
                chains.append(_rwkv_chunk(sf_ref[d, p], r, lw_ref[d, rows, lanes], cum_ref[d, rows, lanes],
                                          k_d, v, -kk, kk * iclr, reverse))
        for (d, p, rows, lanes, r, k_d, v), (y, s_new) in zip(where, _run_lockstep(chains)):
            sf_ref[d, p] = s_new
            mu = _head_sum(y, first) * (1.0 / HEAD_A)
            yc = y - mu
            var = _head_sum(yc * yc, first) * (1.0 / HEAD_A)
            out = yc * lax.rsqrt(var + GN_EPS) * lnw_ref[:, lanes] + lnb_ref[:, lanes]
            acc_ref[d, rows, lanes] = out + _head_sum(r * k_d * rk_ref[:, lanes], first) * v
        return carry

    lax.fori_loop(0, n, body, 0)
    o_ref[...] = ((acc_ref[0] + acc_ref[1]) * gt_ref[...]).astype(o_ref.dtype)


def _rwkv_scan(pam, lw, ic, gate, k_k, k_a, r_k, ln_w, ln_b, s0_bd, n_seq, t, row_blk0, pairs=4):
    w = pairs * LANES
    nb = C_A // w
    vec = lambda: pl.BlockSpec((1, w), lambda b, h: (0, h))
    rows = lambda off: pl.BlockSpec((t, w), lambda b, h: (row_blk0 + b, off + h))
    rows2 = lambda: pl.BlockSpec((2, t, w), lambda b, h: (0, row_blk0 + b, h))
    state = pl.BlockSpec((None, 2, pairs, LANES, LANES), lambda b, h: (b, 0, h, 0, 0))
    return pl.pallas_call(
        functools.partial(_rwkv_scan_kernel, t=t, pairs=pairs),
        grid=(n_seq, nb),
        in_specs=[
            rows(0), rows(nb), rows(2 * nb), rows2(), rows2(), rows(0),
            vec(), vec(), vec(), vec(), vec(),
            state,
        ],
        out_specs=[pl.BlockSpec((t, w), lambda b, h: (b, h)), state],
        out_shape=[
            jax.ShapeDtypeStruct((n_seq * t, C_A), BF16),
            jax.ShapeDtypeStruct((n_seq, 2, C_A // LANES, LANES, LANES), F32),
        ],
        scratch_shapes=[pltpu.VMEM((2, t, w), F32), pltpu.VMEM((2, t, w), F32)],
        compiler_params=_cparams("parallel", "parallel"),
        name="rwkv_scan",
    )(pam, pam, pam, lw, ic, gate, k_k, k_a, r_k, ln_w, ln_b, s0_bd)


def _gdn_chunk(s, q, k, v, gc, beta, reverse):
    c = q.shape[0]
    incl, strict = _order_masks(c, reverse)
    pick = (lax.broadcasted_iota(jnp.int32, (8, LANES), 1) == 0).astype(F32)
    gr = (yield _fdot, pick, jnp.broadcast_to(gc, (c, LANES)), NT)[0:1, :]
    decay = jnp.where(incl, jnp.exp(jnp.where(incl, gc - gr, 0.0)), 0.0)
    g_last = gc[0:1] if reverse else gc[c - 1:c]
    kb = k * beta
    l = jnp.where(strict, (yield _bdot, kb, k, NT) * decay, 0.0)
    attn = (yield _bdot, q, k, NT) * decay
    a_inv = yield from _triangular_inverse(-l, c)
    eg = jnp.exp(gc)
    sol = yield _bdot, a_inv, jnp.concatenate([v * beta, kb * eg], -1), NN
    u, w = sol[:, 0:HEAD_B], sol[:, HEAD_B:]
    ws = yield _bdot, jnp.concatenate([w, q * eg], 0), s, NN
    v_new = u - ws[0:c]
    o = ws[c:2 * c] + (yield _bdot, attn, v_new, NN)
    s_new = s * jnp.exp(g_last) + (yield _bdot, k * jnp.exp(g_last - gc), v_new, TN)
    return o, s_new


def _gdn_scan_kernel(q_ref, k_ref, v_ref, z_ref, ba_ref, cq_ref, ck_ref, cv_ref, gnw_ref, alog_ref, dtb_ref,
                     s0_ref, o_ref, sf_ref, qs_ref, ks_ref, vs_ref, acc_ref, gc_ref, beta_ref, *, t, heads):
    n = t // CHUNK_B
    hg = pl.program_id(1)
    blk = CHUNK_B
    brow = lax.broadcasted_iota(jnp.int32, (blk, LANES), 0)
    no_row = jnp.zeros((1, LANES), F32)

    def conv_silu(x_ref, w, r0, lanes):
        x = x_ref[r0:r0 + blk, lanes]
        before = x_ref[r0 - 1:r0, lanes] if r0 > 0 else no_row
        after = x_ref[r0 + blk:r0 + blk + 1, lanes] if r0 + blk < t else no_row
        prev = jnp.where(brow == 0, before, pltpu.roll(x, 1, 0))
        nxt = jnp.where(brow == blk - 1, after, pltpu.roll(x, blk - 1, 0))
        return _silu(w[0:1] * prev + w[1:2] * x + w[2:3] * nxt)

    def l2norm(x):
        return x * lax.rsqrt(jnp.sum(x * x, -1, keepdims=True) + 1e-6)

    for g in range(heads):
        lanes = slice(g * LANES, (g + 1) * LANES)
        wq, wk, wv = cq_ref[:, lanes], ck_ref[:, lanes], cv_ref[:, lanes]
        for r0 in range(0, t, blk):
            rows = slice(r0, r0 + blk)
            qs_ref[rows, lanes] = l2norm(conv_silu(q_ref, wq, r0, lanes)) * (HEAD_B ** -0.5)
            ks_ref[rows, lanes] = l2norm(conv_silu(k_ref, wk, r0, lanes))
            vs_ref[rows, lanes] = conv_silu(v_ref, wv, r0, lanes)
    ba = ba_ref[...]
    beta_ref[...] = _sigmoid(ba)
    g_log = -jnp.exp(alog_ref[...]) * _softplus(ba + dtb_ref[...])
    gc_ref[0] = _chunk_cumsum(g_log, False, CHUNK_B)
    gc_ref[1] = _chunk_cumsum(g_log, True, CHUNK_B)
    sf_ref[...] = s0_ref[...]
    lane = lax.broadcasted_iota(jnp.int32, (CHUNK_B, LANES), 1)

    def body(i, carry):
        chains = []
        for d in range(2):
            ci = (n - 1 - i) if d == 1 else i
            rows = pl.ds(pl.multiple_of(ci * CHUNK_B, CHUNK_B), CHUNK_B)
            gcs = gc_ref[d, rows, :]
            bts = beta_ref[rows, :]
            for g in range(heads):
                lanes = slice(g * LANES, (g + 1) * LANES)
                chains.append((d, g, rows, lanes, gcs, bts, sf_ref[d, g],
                               qs_ref[rows, lanes], ks_ref[rows, lanes], vs_ref[rows, lanes]))
        gens = []
        for d, g, rows, lanes, gcs, bts, s, q, k, v in chains:
            col = d * H_B + hg * heads + g
            gc = jnp.sum(jnp.where(lane == BA_LANE0 + 2 * H_B + col, gcs, 0.0), -1, keepdims=True)
            beta = jnp.sum(jnp.where(lane == BA_LANE0 + col, bts, 0.0), -1, keepdims=True)
            gens.append(_gdn_chunk(s, q, k, v, gc, beta, d == 1))
        for (d, g, rows, lanes, *_), (o, s_new) in zip(chains, _run_lockstep(gens)):
            sf_ref[d, g] = s_new
            acc_ref[d, rows, lanes] = o
        return carry

    lax.fori_loop(0, n, body, 0)
    for g in range(heads):
        lanes = slice(g * LANES, (g + 1) * LANES)
        for r0 in range(0, t, blk):
            rows = slice(r0, r0 + blk)
            o = acc_ref[0, rows, lanes] + acc_ref[1, rows, lanes]
            o = o * lax.rsqrt(jnp.mean(o * o, -1, keepdims=True) + RMS_EPS) * gnw_ref[...]
            o_ref[rows, lanes] = (o * _silu(z_ref[rows, lanes])).astype(o_ref.dtype)


def _gdn_scan(p, conv_w, alog_vec, dtb_vec, gn_w, s0, n_seq, t, row_blk0, heads=4):
    w = heads * LANES
    nb = C_B // w
    rows = lambda off: pl.BlockSpec((t, w), lambda b, h: (row_blk0 + b, off + h))
    cw = lambda off: pl.BlockSpec((3, w), lambda b, h: (0, off + h))
    vec = pl.BlockSpec((1, LANES), lambda b, h: (0, 0))
    state = pl.BlockSpec((None, 2, heads, HEAD_B, HEAD_B), lambda b, h: (b, 0, h, 0, 0))
    return pl.pallas_call(
        functools.partial(_gdn_scan_kernel, t=t, heads=heads),
        grid=(n_seq, nb),
        in_specs=[
            rows(0), rows(nb), rows(2 * nb), rows(3 * nb),
            pl.BlockSpec((t, LANES), lambda b, h: (row_blk0 + b, (4 * C_B + C_PA) // LANES)),
            cw(0), cw(nb), cw(2 * nb),
            vec, vec, vec,
            state,
        ],
        out_specs=[pl.BlockSpec((t, w), lambda b, h: (b, h)), state],
        out_shape=[
            jax.ShapeDtypeStruct((n_seq * t, C_B), BF16),
            jax.ShapeDtypeStruct((n_seq, 2, H_B, HEAD_B, HEAD_B), F32),
        ],
        scratch_shapes=[pltpu.VMEM((t, w), F32)] * 3 + [pltpu.VMEM((2, t, w), F32), pltpu.VMEM((2, t, LANES), F32),
                                                         pltpu.VMEM((t, LANES), F32)],
        compiler_params=_cparams("parallel", "parallel"),
        name="gdn_scan",
    )(p, p, p, p, p, conv_w, conv_w, conv_w, gn_w, alog_vec, dtb_vec, s0)


POOL_HALF = 2


def _pool_kernel(x_ref, nw_ref, sc_ref, sh_ref, w_ref, ps_ref, g_ref, o_ref, *, ctx_blocks, t_ctx):
    tb, d = x_ref.shape
    t = jnp.where(pl.program_id(0) < ctx_blocks, t_ctx, tb)
    row = lax.broadcasted_iota(jnp.int32, (tb, C_G), 0) & (t - 1)

    def back(x, n):
        return _shift_rows(x, n, row >= n)

    def fwd(x, n):
        return _shift_rows(x, -n, row < t - n)

    ss = jnp.zeros((tb, 1), F32)
    for g in range(POOL_GROUPS):
        xg = x_ref[:, g * C_G:(g + 1) * C_G]
        ss = ss + jnp.sum(xg * xg, -1, keepdims=True)
    rs = lax.rsqrt(ss * (1.0 / d) + RMS_EPS)

    def group(g, out_cols):
        cols = slice(g * C_G, (g + 1) * C_G)
        xg = x_ref[:, cols]
        h = (xg * rs * nw_ref[:, cols]) * (1.0 + sc_ref[:, cols]) + sh_ref[:, cols]
        half = 1 << g
        ph, qh, n = h, h, 1
        while n < half:
            ph = back(ph, n) + ph
            qh = qh + fwd(qh, n)
            n *= 2
        s = back(ph, 1) + qh
        count = jnp.clip(row + half, 0, t) - jnp.clip(row - half, 0, t)
        p = s / count.astype(F32) - h
        y = _bdot(p, w_ref[g % POOL_HALF]) * ps_ref[:, cols]
        o_ref[:, out_cols] = xg + g_ref[:, cols] * y

    for half_idx in range(POOL_GROUPS // POOL_HALF):
        @pl.when(pl.program_id(1) == half_idx)
        def _(half_idx=half_idx):
            for gg in range(POOL_HALF):
                group(half_idx * POOL_HALF + gg, slice(gg * C_G, (gg + 1) * C_G))


def _pool_mixer(x, nw, scale_blk, shift_blk, w_pool, layer, pool_scale, gate_blk, t, ctx_blocks, t_ctx):
    m, d = x.shape
    wide = POOL_HALF * C_G
    vec = lambda: pl.BlockSpec((1, d), lambda b, j: (0, 0))
    blk_vec = lambda: pl.BlockSpec((None, 1, d), lambda b, j: (b, 0, 0))
    return pl.pallas_call(
        functools.partial(_pool_kernel, ctx_blocks=ctx_blocks, t_ctx=t_ctx),
        grid=(m // t, POOL_GROUPS // POOL_HALF),
        in_specs=[
            pl.BlockSpec((t, d), lambda b, j: (b, 0)),
            vec(), blk_vec(), blk_vec(),
            pl.BlockSpec((None, POOL_HALF, C_G, C_G), lambda b, j: (layer, j, 0, 0)),
            vec(), blk_vec(),
        ],
        out_specs=pl.BlockSpec((t, wide), lambda b, j: (b, j)),
        out_shape=jax.ShapeDtypeStruct((m, d), F32),
        compiler_params=_cparams("parallel", "arbitrary"),
        name="pool_mixer",
    )(x, nw.reshape(1, d), scale_blk, shift_blk, w_pool, pool_scale, gate_blk)


def _pad_cols(w, n):
    return jnp.pad(w, ((0, 0), (0, n - w.shape[1])))


def _block_diag_pairs(s):
    b, two, h, n, _ = s.shape
    sp = s.reshape(b, two, h // 2, 2, n, n)
    z = jnp.zeros_like(sp[:, :, :, 0])
    top = jnp.concatenate([sp[:, :, :, 0], z], -1)
    bot = jnp.concatenate([z, sp[:, :, :, 1]], -1)
    return jnp.concatenate([top, bot], -2)


def _diag_pairs(s_bd):
    b, two, hp, _, _ = s_bd.shape
    n = HEAD_A
    return jnp.stack([s_bd[..., :n, :n], s_bd[..., n:, n:]], 3).reshape(b, two, 2 * hp, n, n)


def kernel(x_prompt, x_sample, state_rwkv, state_delta, c, c_ctx, mod_w, mod_b, norm_mix_w, norm_ffn_w, norm_final_w, ab_w_in, ab_w_out, rwkv_mu, rwkv_w0, rwkv_w2, rwkv_a0, rwkv_a2, rwkv_g2, rwkv_k_k, rwkv_k_a, rwkv_r_k, rwkv_ln_w, rwkv_ln_b, gdn_conv_w, gdn_a_log, gdn_dt_bias, gdn_norm_w, pool_w, pool_scale, ffn_w_gate, ffn_w_up, ffn_w_down):
    bp, tp, d = x_prompt.shape
    bs, ts, _ = x_sample.shape
    mp, ms = bp * tp, bs * ts
    m = mp + ms
    assert tp == ROW_BLK and ts % ROW_BLK == 0 and mp % ts == 0
    x = jnp.concatenate([x_prompt.reshape(mp, d), x_sample.reshape(ms, d)], 0)

    n_cond = 16
    cond = jnp.zeros((n_cond, d), F32).at[0].set(c_ctx).at[1:1 + bs].set(c)
    mod = _mod_all(cond, mod_w, mod_b).reshape(DEPTH, n_cond, 6, d)
    mod_blk = jnp.concatenate([
        jnp.broadcast_to(mod[:, 0:1], (DEPTH, mp // ROW_BLK, 6, d)),
        jnp.repeat(mod[:, 1:1 + bs], ts // ROW_BLK, axis=1)], 1)[:, :, :, None, :]
    zeros_blk = jnp.zeros((m // ROW_BLK, 1, d), F32)

    zero_r = jnp.zeros((bp, 2, H_A // 2, LANES, LANES), F32)
    zero_d = jnp.zeros((bp, 2, H_B, HEAD_B, HEAD_B), F32)
    new_r, new_d = [], []

    for l in range(DEPTH):
        shift_m, scale_m, gate_m, shift_f, scale_f, gate_f = (mod_blk[l, :, j] for j in range(6))
        i = l // 2
        if l % 2 == 0:
            w_in = ab_w_in[i]
            w_p = jnp.concatenate([
                w_in[:, C_PA:C_PA + 4 * C_B], w_in[:, :C_PA], w_in[:, C_PA + 4 * C_B:],
                jnp.zeros((d, C_PA_PAD - C_PA - 4 * H_B), F32)], 1).astype(BF16)
            p = _norm_matmul(x, norm_mix_w[l], scale_m, shift_m, w_p)
            mu = _pad_cols(rwkv_mu[i][None, :], C_PA_PAD)
            pam = _token_shift(p, mu, ts, mp // ts, tp)
            w2bd = jnp.zeros((2, 2 * LORA, C_A), F32).at[0, :LORA].set(rwkv_w2[i, 0]).at[1, LORA:].set(rwkv_w2[i, 1])
            a2bd = jnp.zeros((2, 2 * LORA, C_A), F32).at[0, :LORA].set(rwkv_a2[i, 0]).at[1, LORA:].set(rwkv_a2[i, 1])
            g2pad = jnp.pad(rwkv_g2[i], ((0, C_LO - 4 * LORA - GATE_LORA), (0, 0)))
            lw, ic, gate = _rwkv_lora(pam, w2bd.astype(BF16), a2bd.astype(BF16), g2pad.astype(BF16),
                                      rwkv_w0[i][:, None, :], rwkv_a0[i][:, None, :])
            vecs = [v_[i].reshape(1, C_A) for v_ in (rwkv_k_k, rwkv_k_a, rwkv_r_k, rwkv_ln_w, rwkv_ln_b)]
            oa_p, sr_p = _rwkv_scan(pam, lw, ic, gate, *vecs, zero_r, bp, tp, 0, pairs=H_A // 2)
            oa_s, _ = _rwkv_scan(pam, lw, ic, gate, *vecs, _block_diag_pairs(state_rwkv[:, i]), bs, ts, mp // ts)
            gnw = gdn_norm_w[i].reshape(1, HEAD_B)
            on_decay_lanes = lambda p_: jnp.zeros((1, LANES), F32).at[
                0, BA_LANE0 + 2 * H_B:BA_LANE0 + 4 * H_B].set(p_.reshape(2 * H_B))
            alog_vec, dtb_vec = on_decay_lanes(gdn_a_log[i]), on_decay_lanes(gdn_dt_bias[i])
            ob_p, sd_p = _gdn_scan(p, gdn_conv_w[i], alog_vec, dtb_vec, gnw, zero_d, bp, tp, 0, heads=H_B)
            ob_s, _ = _gdn_scan(p, gdn_conv_w[i], alog_vec, dtb_vec, gnw, state_delta[:, i],
                                bs, ts, mp // ts)
            new_r.append(_diag_pairs(sr_p))
            new_d.append(sd_p)
            o_a = jnp.concatenate([oa_p, oa_s], 0)
            o_b = jnp.concatenate([ob_p, ob_s], 0)
            x = _matmul_residual([o_a, o_b], ab_w_out, i, x, gate_m)
        else:
            per_blk = lambda j: jnp.concatenate(
                [jnp.broadcast_to(mod[l, 0:1, j], (mp // ts, d)), mod[l, 1:1 + bs, j]], 0)[:, None, :]
            x = _pool_mixer(x, norm_mix_w[l], per_blk(1), per_blk(0), pool_w, i, pool_scale[i].reshape(1, d),
                            per_blk(2), ts, mp // ts, tp)
        act = _ffn_up(x, norm_ffn_w[l], scale_f, shift_f, ffn_w_gate, ffn_w_up, l)
        x = _matmul_residual([act], ffn_w_down, l, x, gate_f, single_buffer_w=True)

    y_prompt = _norm_mod(x, norm_final_w, zeros_blk, zeros_blk, F32, 0, mp // ROW_BLK).reshape(bp, tp, d)
    y_sample = _norm_mod(x, norm_final_w, zeros_blk, zeros_blk, F32, mp // ROW_BLK, ms // ROW_BLK).reshape(bs, ts, d)
    return (y_prompt, y_sample, jnp.stack(new_r, 1), jnp.stack(new_d, 1))
```

```python
import functools

import jax
import jax.numpy as jnp
from jax import lax
from jax.experimental import pallas as pl
from jax.experimental.pallas import tpu as pltpu

F32 = jnp.float32
BF16 = jnp.bfloat16

D_MODEL = 2048
DEPTH = 4
GRID_W = 64
N_AB = 2
C_A = 1024
HEAD_A = 64
H_A = 16
LORA = 64
GATE_LORA = 160
C_B = 1024
HEAD_B = 128
H_B = 8
C_PA = 3 * C_A + 4 * LORA + GATE_LORA
C_PA_PAD = 3584
C_LO = C_PA_PAD - 3 * C_A
POOL_GROUPS = 4
C_G = D_MODEL // POOL_GROUPS
D_FF = 5632
RMS_EPS = 1e-6
GN_EPS = 64e-5
CHUNK = 64
CHUNK_B = 128
ROW_BLK = 256
LANES = 128
BA_LANE0 = C_PA % LANES
VMEM_LIMIT = 48 * 1024 * 1024

NT = (((1,), (1,)), ((), ()))
TN = (((0,), (0,)), ((), ()))
NN = (((1,), (0,)), ((), ()))


def _cparams(*sem, vmem=VMEM_LIMIT):
    return pltpu.CompilerParams(dimension_semantics=sem, vmem_limit_bytes=vmem)


def _bdot(a, b, dims=NN):
    return lax.dot_general(a.astype(BF16), b.astype(BF16), dims, preferred_element_type=F32)


def _fdot(a, b, dims=NN):
    return lax.dot_general(a, b, dims, precision=lax.Precision.HIGHEST, preferred_element_type=F32)


def _sigmoid(x):
    return 0.5 + 0.5 * jnp.tanh(0.5 * x)


def _silu(x):
    return x * _sigmoid(x)


def _softplus(x):
    return jnp.maximum(x, 0.0) + jnp.log(1.0 + jnp.exp(-jnp.abs(x)))


def _mod_kernel(c_ref, w_ref, b_ref, o_ref):
    act = _silu(c_ref[...])
    w = w_ref[...]
    a_hi = act.astype(BF16)
    a_lo = (act - a_hi.astype(F32)).astype(BF16)
    w_hi = w.astype(BF16)
    w_lo = (w - w_hi.astype(F32)).astype(BF16)
    n_c = act.shape[0]
    both = jnp.dot(jnp.concatenate([a_hi, a_lo], 0), w_hi, preferred_element_type=F32)
    o_ref[...] = both[:n_c] + both[n_c:] + jnp.dot(a_hi, w_lo, preferred_element_type=F32) + b_ref[...]


def _mod_all(cond, mod_w, mod_b):
    n_c, d = cond.shape
    depth, _, n = mod_w.shape
    tn = 1024
    return pl.pallas_call(
        _mod_kernel,
        grid=(depth, n // tn),
        in_specs=[
            pl.BlockSpec((n_c, d), lambda l, j: (0, 0)),
            pl.BlockSpec((None, d, tn), lambda l, j: (l, 0, j)),
            pl.BlockSpec((None, 1, tn), lambda l, j: (l, 0, j)),
        ],
        out_specs=pl.BlockSpec((None, n_c, tn), lambda l, j: (l, 0, j)),
        out_shape=jax.ShapeDtypeStruct((depth, n_c, n), F32),
        compiler_params=_cparams("parallel", "parallel"),
        name="mod_all",
    )(cond, mod_w, mod_b.reshape(depth, 1, n))


def _norm_mod_kernel(x_ref, nw_ref, sc_ref, sh_ref, o_ref):
    x = x_ref[...]
    ms = jnp.mean(x * x, -1, keepdims=True)
    y = x * lax.rsqrt(ms + RMS_EPS) * nw_ref[...]
    o_ref[...] = (y * (1.0 + sc_ref[...]) + sh_ref[...]).astype(o_ref.dtype)


def _norm_mod(x, nw, scale, shift, out_dtype, blk0=0, n_blk=None):
    m, d = x.shape
    nb = m // ROW_BLK if n_blk is None else n_blk
    return pl.pallas_call(
        _norm_mod_kernel,
        grid=(nb,),
        in_specs=[
            pl.BlockSpec((ROW_BLK, d), lambda i: (blk0 + i, 0)),
            pl.BlockSpec((1, d), lambda i: (0, 0)),
            pl.BlockSpec((None, 1, d), lambda i: (blk0 + i, 0, 0)),
            pl.BlockSpec((None, 1, d), lambda i: (blk0 + i, 0, 0)),
        ],
        out_specs=pl.BlockSpec((ROW_BLK, d), lambda i: (i, 0)),
        out_shape=jax.ShapeDtypeStruct((nb * ROW_BLK, d), out_dtype),
        compiler_params=_cparams("parallel"),
        name="norm_mod",
    )(x, nw.reshape(1, d), scale, shift)


def _norm_mod_rows(x_ref, nw_ref, sc_ref, sh_ref, s):
    x = x_ref[s * ROW_BLK:(s + 1) * ROW_BLK, :]
    y = x * lax.rsqrt(jnp.mean(x * x, -1, keepdims=True) + RMS_EPS) * nw_ref[...]
    return (y * (1.0 + sc_ref[s]) + sh_ref[s]).astype(BF16)


def _norm_mm_kernel(x_ref, nw_ref, sc_ref, sh_ref, b_ref, o_ref, h_ref, *, tm):
    @pl.when(pl.program_id(1) == 0)
    def _():
        for s in range(tm // ROW_BLK):
            h_ref[s * ROW_BLK:(s + 1) * ROW_BLK, :] = _norm_mod_rows(x_ref, nw_ref, sc_ref, sh_ref, s)

    o_ref[...] = jnp.dot(h_ref[...], b_ref[...], preferred_element_type=F32)


def _norm_matmul(x, nw, scale, shift, b, tm=1024, tn=512):
    m, d = x.shape
    n = b.shape[1]
    sub = tm // ROW_BLK
    return pl.pallas_call(
        functools.partial(_norm_mm_kernel, tm=tm),
        grid=(m // tm, n // tn),
        in_specs=[
            pl.BlockSpec((tm, d), lambda i, j: (i, 0)),
            pl.BlockSpec((1, d), lambda i, j: (0, 0)),
            pl.BlockSpec((sub, 1, d), lambda i, j: (i, 0, 0)),
            pl.BlockSpec((sub, 1, d), lambda i, j: (i, 0, 0)),
            pl.BlockSpec((d, tn), lambda i, j: (0, j)),
        ],
        out_specs=pl.BlockSpec((tm, tn), lambda i, j: (i, j)),
        out_shape=jax.ShapeDtypeStruct((m, n), F32),
        scratch_shapes=[pltpu.VMEM((tm, d), BF16)],
        compiler_params=_cparams("parallel", "arbitrary"),
        name="norm_matmul",
    )(x, nw.reshape(1, d), scale, shift, b)


def _gated_residual(x_ref, g_ref, acc, o_ref, tm):
    for s in range(tm // ROW_BLK):
        rows = slice(s * ROW_BLK, (s + 1) * ROW_BLK)
        o_ref[rows, :] = x_ref[rows, :] + g_ref[s] * acc[rows, :]


def _cast_weight_tiles(w_refs, wb_refs):
    @pl.when(pl.program_id(1) == 0)
    def _():
        for w_ref, wb_ref in zip(w_refs, wb_refs):
            wb_ref[...] = w_ref[...].astype(BF16)


def _mm_res_kernel(*refs, n_in, tm):
    a_refs, w_refs = refs[:n_in], refs[n_in:2 * n_in]
    x_ref, g_ref, o_ref = refs[2 * n_in:2 * n_in + 3]
    wb_refs = refs[2 * n_in + 3:]
    _cast_weight_tiles(w_refs, wb_refs)
    acc = jnp.dot(a_refs[0][...], wb_refs[0][...], preferred_element_type=F32)
    for a_ref, wb_ref in zip(a_refs[1:], wb_refs[1:]):
        acc = acc + jnp.dot(a_ref[...], wb_ref[...], preferred_element_type=F32)
    _gated_residual(x_ref, g_ref, acc, o_ref, tm)


def _matmul_residual(a_list, w, layer, x, gate, tm=512, tn=512, single_buffer_w=False):
    m, n = x.shape
    sub = tm // ROW_BLK
    kb = w.shape[1] // len(a_list)
    w_mode = dict(pipeline_mode=pl.Buffered(1)) if single_buffer_w else {}
    in_specs = [pl.BlockSpec((tm, kb), lambda j, i: (i, 0)) for _ in a_list]
    in_specs += [pl.BlockSpec((None, kb, tn), functools.partial(lambda j, i, r: (layer, r, j), r=r), **w_mode)
                 for r in range(len(a_list))]
    in_specs += [
        pl.BlockSpec((tm, tn), lambda j, i: (i, j)),
        pl.BlockSpec((sub, 1, tn), lambda j, i: (i, 0, j)),
    ]
    return pl.pallas_call(
        functools.partial(_mm_res_kernel, n_in=len(a_list), tm=tm),
        grid=(n // tn, m // tm),
        in_specs=in_specs,
        out_specs=pl.BlockSpec((tm, tn), lambda j, i: (i, j)),
        out_shape=jax.ShapeDtypeStruct((m, n), F32),
        scratch_shapes=[pltpu.VMEM((kb, tn), BF16) for _ in a_list],
        compiler_params=_cparams("parallel", "arbitrary"),
        name="matmul_residual",
    )(*a_list, *([w] * len(a_list)), x, gate)


def _ffn_up_kernel(x_ref, nw_ref, sc_ref, sh_ref, wg_ref, wu_ref, o_ref, h_ref, *, tm):
    @pl.when(pl.program_id(1) == 0)
    def _():
        for s in range(tm // ROW_BLK):
            h_ref[s * ROW_BLK:(s + 1) * ROW_BLK, :] = _norm_mod_rows(x_ref, nw_ref, sc_ref, sh_ref, s)

    h = h_ref[...]
    g = jnp.dot(h, wg_ref[...].astype(BF16), preferred_element_type=F32)
    u = jnp.dot(h, wu_ref[...].astype(BF16), preferred_element_type=F32)
    o_ref[...] = (_silu(g) * u).astype(o_ref.dtype)


def _ffn_up(x, nw, scale, shift, wg, wu, layer, tm=1024, tn=256):
    m, d = x.shape
    n = wg.shape[2]
    sub = tm // ROW_BLK
    w_spec = lambda: pl.BlockSpec((None, d, tn), lambda i, j: (layer, 0, j))
    return pl.pallas_call(
        functools.partial(_ffn_up_kernel, tm=tm),
        grid=(m // tm, n // tn),
        in_specs=[
            pl.BlockSpec((tm, d), lambda i, j: (i, 0)),
            pl.BlockSpec((1, d), lambda i, j: (0, 0)),
            pl.BlockSpec((sub, 1, d), lambda i, j: (i, 0, 0)),
            pl.BlockSpec((sub, 1, d), lambda i, j: (i, 0, 0)),
            w_spec(), w_spec(),
        ],
        out_specs=pl.BlockSpec((tm, tn), lambda i, j: (i, j)),
        out_shape=jax.ShapeDtypeStruct((m, n), BF16),
        scratch_shapes=[pltpu.VMEM((tm, d), BF16)],
        compiler_params=_cparams("parallel", "arbitrary"),
        name="ffn_up",
    )(x, nw.reshape(1, d), scale, shift, wg, wu)


def _shift_rows(x, shift, valid):
    t = x.shape[0]
    return jnp.where(valid, pltpu.roll(x, shift % t, 0), 0.0)


def _token_shift_kernel(p_ref, mu_ref, o_ref, *, ctx_blocks, t_ctx):
    x = p_ref[...]
    t, c = x.shape
    row = lax.broadcasted_iota(jnp.int32, (t, c), 0)
    lane = lax.broadcasted_iota(jnp.int32, (t, c), 1)
    mu = mu_ref[...]

    @pl.when(pl.program_id(0) < ctx_blocks)
    def _():
        pos = row % t_ctx
        s = jnp.where(lane % 2 == 0, _shift_rows(x, 1, pos != 0), _shift_rows(x, -1, pos != t_ctx - 1))
        o_ref[...] = x + (s - x) * mu

    @pl.when(pl.program_id(0) >= ctx_blocks)
    def _():
        col = row % GRID_W
        sel = lane % 4
        s = jnp.where(sel == 0, _shift_rows(x, 1, col != 0),
            jnp.where(sel == 1, _shift_rows(x, -1, col != GRID_W - 1),
            jnp.where(sel == 2, _shift_rows(x, GRID_W, row >= GRID_W),
                      _shift_rows(x, -GRID_W, row < t - GRID_W))))
        o_ref[...] = x + (s - x) * mu


def _token_shift(p, mu, t, ctx_blocks, t_ctx):
    tc = 512
    m = p.shape[0]
    first = 4 * C_B // tc
    return pl.pallas_call(
        functools.partial(_token_shift_kernel, ctx_blocks=ctx_blocks, t_ctx=t_ctx),
        grid=(m // t, C_PA_PAD // tc),
        in_specs=[
            pl.BlockSpec((t, tc), lambda b, j: (b, first + j)),
            pl.BlockSpec((1, tc), lambda b, j: (0, j)),
        ],
        out_specs=pl.BlockSpec((t, tc), lambda b, j: (b, j)),
        out_shape=jax.ShapeDtypeStruct((m, C_PA_PAD), F32),
        compiler_params=_cparams("parallel", "parallel"),
        name="token_shift",
    )(p, mu)


def _order_masks(n, reverse):
    rr = lax.broadcasted_iota(jnp.int32, (n, n), 0)
    cc = lax.broadcasted_iota(jnp.int32, (n, n), 1)
    if reverse:
        return cc >= rr, cc > rr
    return cc <= rr, cc < rr


def _run_lockstep(chains):
    results = [None] * len(chains)
    pending = {i: next(c) for i, c in enumerate(chains)}
    while pending:
        products = {i: fn(a, b, dims) for i, (fn, a, b, dims) in pending.items()}
        pending = {}
        for i, prod in products.items():
            try:
                pending[i] = chains[i].send(prod)
            except StopIteration as stop:
                results[i] = stop.value
    return results


def _triangular_inverse(l, order):
    n = l.shape[0]
    ri = lax.broadcasted_iota(jnp.int32, (n, n), 0)
    ci = lax.broadcasted_iota(jnp.int32, (n, n), 1)

    def joins(k, g):
        return (ri // (g * k) == ci // (g * k)) & ((ri // k) != (ci // k))

    x = (ri == ci).astype(F32) + jnp.where(joins(1, 2), l, 0.0)
    k = 2
    while k < order:
        g = 4 if 4 * k <= order else 2
        nil = yield _bdot, x, jnp.where(joins(k, g), l, 0.0), NN
        x = x + (yield _bdot, nil, x, NN)
        if g == 4:
            nil2 = yield _bdot, nil, nil, NN
            x = x + (yield _bdot, nil2, x, NN)
        k *= g
    return x


def _split_heads(x, first):
    return jnp.concatenate([jnp.where(first, x, 0.0), jnp.where(first, 0.0, x)], 0)


def _chunk_cumsum(x, reverse, chunk):
    t = x.shape[0]
    pos = lax.broadcasted_iota(jnp.int32, x.shape, 0) % chunk
    s = 1
    while s < chunk:
        if reverse:
            x = x + jnp.where(pos < chunk - s, pltpu.roll(x, t - s, 0), 0.0)
        else:
            x = x + jnp.where(pos >= s, pltpu.roll(x, s, 0), 0.0)
        s *= 2
    return x


def _rwkv_chunk(s_bd, r, lw, cum, k, v, a, b, reverse):
    c = CHUNK
    cum_ex = cum - lw
    tot = cum[0:1] if reverse else cum[c - 1:c]
    e_neg = jnp.exp(-cum)
    e_hat = jnp.exp(tot - cum)
    at = a * jnp.exp(cum_ex)
    rt = r * jnp.exp(cum)
    first = lax.broadcasted_iota(jnp.int32, (c, LANES), 1) < HEAD_A
    lhs = jnp.concatenate([at, rt], 0)
    rbk = jnp.concatenate([_split_heads(b * e_neg, first), _split_heads(k * e_neg, first)], 0)
    g = yield _bdot, lhs, rbk, NT
    ri_g = lax.broadcasted_iota(jnp.int32, (2 * c, 4 * c), 0)
    pos_c = lax.broadcasted_iota(jnp.int32, (2 * c, 4 * c), 1) % c
    pos_r = ri_g % c
    own = jnp.where(ri_g < c, 0, 1)
    keep = (pos_c > pos_r - own) if reverse else (pos_c < pos_r + own)
    g = jnp.where(keep, g, 0.0)
    l_ab = g[0:c, 0:2 * c]
    l_ak = g[0:c, 2 * c:4 * c]
    m_r = g[c:2 * c, :]
    v_bd = _split_heads(v, first)
    sa = yield _bdot, lhs, s_bd, NT
    rhs = sa[0:c] + (yield _bdot, l_ak, v_bd, NN)
    l_bd = _split_heads(l_ab, first)
    t_inv = yield from _triangular_inverse(l_bd, c)
    u_bd = yield _bdot, t_inv, _split_heads(rhs, first), NN
    y = sa[c:2 * c] + (yield _bdot, m_r, jnp.concatenate([u_bd, v_bd], 0), NN)
    u = u_bd[0:c] + u_bd[c:2 * c]
    upd = yield _bdot, jnp.concatenate([u, v], 0), jnp.concatenate([b * e_hat, k * e_hat], 0), TN
    ri = lax.broadcasted_iota(jnp.int32, (LANES, LANES), 0) < HEAD_A
    ci = lax.broadcasted_iota(jnp.int32, (LANES, LANES), 1) < HEAD_A
    s_new = s_bd * jnp.exp(tot) + jnp.where(ri == ci, upd, 0.0)
    return y, s_new


def _head_sum(x, first):
    s0 = jnp.sum(jnp.where(first, x, 0.0), -1, keepdims=True)
    s1 = jnp.sum(jnp.where(first, 0.0, x), -1, keepdims=True)
    return jnp.where(first, s0, s1)


def _rwkv_scan_kernel(r_ref, k_ref, v_ref, lo_ref, w2_ref, a2_ref, g2_ref, w0_ref, a0_ref,
                      kk_ref, ka_ref, rk_ref, lnw_ref, lnb_ref, s0_ref, o_ref, sf_ref, acc_ref, *, t, pairs):
    n = t // CHUNK
    first = lax.broadcasted_iota(jnp.int32, (CHUNK, LANES), 1) < HEAD_A
    sf_ref[...] = s0_ref[...]

    def body(i, carry):
        where, chains = [], []
        for d in range(2):
            reverse = d == 1
            ci = (n - 1 - i) if reverse else i
            rows = pl.ds(pl.multiple_of(ci * CHUNK, CHUNK), CHUNK)
            lo = lo_ref[rows, :]
            w_log = -_softplus(-(w0_ref[d] + _bdot(jnp.tanh(lo[:, 0:2 * LORA]), w2_ref[d]))) - 0.5
            lw_all = -jnp.exp(w_log)
            ic_all = _sigmoid(a0_ref[d] + _bdot(lo[:, 2 * LORA:4 * LORA], a2_ref[d]))
            cum_all = _chunk_cumsum(lw_all, reverse, CHUNK)
            for p in range(pairs):
                lanes = slice(p * LANES, (p + 1) * LANES)
                r = r_ref[rows, lanes]
                k = k_ref[rows, lanes]
                v = v_ref[rows, lanes]
                iclr = ic_all[:, lanes]
                kk = k * kk_ref[:, lanes]
                kk = kk * lax.rsqrt(_head_sum(kk * kk, first) + 1e-12)
                k_d = k * (1.0 + (iclr - 1.0) * ka_ref[:, lanes])
                where.append((d, p, rows, lanes, r, k_d, v))
                chains.append(_rwkv_chunk(sf_ref[d, p], r, lw_all[:, lanes], cum_all[:, lanes],
                                          k_d, v, -kk, kk * iclr, reverse))
        for (d, p, rows, lanes, r, k_d, v), (y, s_new) in zip(where, _run_lockstep(chains)):
            sf_ref[d, p] = s_new
            mu = _head_sum(y, first) * (1.0 / HEAD_A)
            yc = y - mu
            var = _head_sum(yc * yc, first) * (1.0 / HEAD_A)
            out = yc * lax.rsqrt(var + GN_EPS) * lnw_ref[:, lanes] + lnb_ref[:, lanes]
            acc_ref[d, rows, lanes] = out + _head_sum(r * k_d * rk_ref[:, lanes], first) * v
        return carry

    lax.fori_loop(0, n, body, 0)
    blk = 2 * CHUNK
    for r0 in range(0, t, blk):
        rows = slice(r0, r0 + blk)
        gate = _bdot(_sigmoid(lo_ref[rows, 4 * LORA:]), g2_ref[...])
        o_ref[rows, :] = ((acc_ref[0, rows, :] + acc_ref[1, rows, :]) * gate).astype(o_ref.dtype)


def _rwkv_scan(pam, w2bd, a2bd, g2pad, w0, a0, k_k, k_a, r_k, ln_w, ln_b, s0_bd, n_seq, t, row_blk0, pairs):
    w = pairs * LANES
    nb = C_A // w
    vec = lambda: pl.BlockSpec((1, w), lambda b, h: (0, h))
    vec2 = lambda: pl.BlockSpec((2, 1, w), lambda b, h: (0, 0, h))
    rows = lambda off: pl.BlockSpec((t, w), lambda b, h: (row_blk0 + b, off + h))
    state = pl.BlockSpec((None, 2, pairs, LANES, LANES), lambda b, h: (b, 0, h, 0, 0))
    return pl.pallas_call(
        functools.partial(_rwkv_scan_kernel, t=t, pairs=pairs),
        grid=(n_seq, nb),
        in_specs=[
            rows(0), rows(nb), rows(2 * nb),
            pl.BlockSpec((t, C_LO), lambda b, h: (row_blk0 + b, 3 * C_A // C_LO)),
            pl.BlockSpec((2, 2 * LORA, w), lambda b, h: (0, 0, h)),
            pl.BlockSpec((2, 2 * LORA, w), lambda b, h: (0, 0, h)),
            pl.BlockSpec((C_LO - 4 * LORA, w), lambda b, h: (0, h)),
            vec2(), vec2(),
            vec(), vec(), vec(), vec(), vec(),
            state,
        ],
        out_specs=[pl.BlockSpec((t, w), lambda b, h: (b, h)), state],
        out_shape=[
            jax.ShapeDtypeStruct((n_seq * t, C_A), BF16),
            jax.ShapeDtypeStruct((n_seq, 2, C_A // LANES, LANES, LANES), F32),
        ],
        scratch_shapes=[pltpu.VMEM((2, t, w), F32)],
        compiler_params=_cparams("parallel", "parallel", vmem=max(VMEM_LIMIT, (
            2 * (t * (3 * w * 4 + C_LO * 4 + w * 2) + 2 * 2 * pairs * LANES * LANES * 4)
            + 2 * t * w * 4 + 10 * 1024 * 1024))),
        name="rwkv_scan",
    )(pam, pam, pam, pam, w2bd, a2bd, g2pad, w0, a0, k_k, k_a, r_k, ln_w, ln_b, s0_bd)


def _gdn_chunk(s, q, k, v, gc, beta, reverse):
    c = q.shape[0]
    incl, strict = _order_masks(c, reverse)
    pick = (lax.broadcasted_iota(jnp.int32, (8, LANES), 1) == 0).astype(F32)
    gr = (yield _fdot, pick, jnp.broadcast_to(gc, (c, LANES)), NT)[0:1, :]
    decay = jnp.where(incl, jnp.exp(jnp.where(incl, gc - gr, 0.0)), 0.0)
    g_last = gc[0:1] if reverse else gc[c - 1:c]
    kb = k * beta
    l = jnp.where(strict, (yield _bdot, kb, k, NT) * decay, 0.0)
    attn = (yield _bdot, q, k, NT) * decay
    a_inv = yield from _triangular_inverse(-l, c)
    eg = jnp.exp(gc)
    sol = yield _bdot, a_inv, jnp.concatenate([v * beta, kb * eg], -1), NN
    u, w = sol[:, 0:HEAD_B], sol[:, HEAD_B:]
    ws = yield _bdot, jnp.concatenate([w, q * eg], 0), s, NN
    v_new = u - ws[0:c]
    o = ws[c:2 * c] + (yield _bdot, attn, v_new, NN)
    s_new = s * jnp.exp(g_last) + (yield _bdot, k * jnp.exp(g_last - gc), v_new, TN)
    return o, s_new


def _gdn_scan_kernel(q_ref, k_ref, v_ref, z_ref, ba_ref, cq_ref, ck_ref, cv_ref, gnw_ref, alog_ref, dtb_ref,
                     s0_ref, o_ref, sf_ref, qs_ref, ks_ref, vs_ref, acc_ref, gc_ref, beta_ref, *, t, heads):
    n = t // CHUNK_B
    hg = pl.program_id(1)
    blk = CHUNK_B
    brow = lax.broadcasted_iota(jnp.int32, (blk, LANES), 0)
    no_row = jnp.zeros((1, LANES), F32)

    def conv_silu(x_ref, w, r0, lanes):
        x = x_ref[r0:r0 + blk, lanes]
        before = x_ref[r0 - 1:r0, lanes] if r0 > 0 else no_row
        after = x_ref[r0 + blk:r0 + blk + 1, lanes] if r0 + blk < t else no_row
        prev = jnp.where(brow == 0, before, pltpu.roll(x, 1, 0))
        nxt = jnp.where(brow == blk - 1, after, pltpu.roll(x, blk - 1, 0))
        return _silu(w[0:1] * prev + w[1:2] * x + w[2:3] * nxt)

    def l2norm(x):
        return x * lax.rsqrt(jnp.sum(x * x, -1, keepdims=True) + 1e-6)

    for g in range(heads):
        lanes = slice(g * LANES, (g + 1) * LANES)
        wq, wk, wv = cq_ref[:, lanes], ck_ref[:, lanes], cv_ref[:, lanes]
        for r0 in range(0, t, blk):
            rows = slice(r0, r0 + blk)
            qs_ref[rows, lanes] = l2norm(conv_silu(q_ref, wq, r0, lanes)) * (HEAD_B ** -0.5)
            ks_ref[rows, lanes] = l2norm(conv_silu(k_ref, wk, r0, lanes))
            vs_ref[rows, lanes] = conv_silu(v_ref, wv, r0, lanes)
    ba = ba_ref[...]
    beta_ref[...] = _sigmoid(ba)
    g_log = -jnp.exp(alog_ref[...]) * _softplus(ba + dtb_ref[...])
    gc_ref[0] = _chunk_cumsum(g_log, False, CHUNK_B)
    gc_ref[1] = _chunk_cumsum(g_log, True, CHUNK_B)
    sf_ref[...] = s0_ref[...]
    lane = lax.broadcasted_iota(jnp.int32, (CHUNK_B, LANES), 1)

    def body(i, carry):
        chains = []
        for d in range(2):
            ci = (n - 1 - i) if d == 1 else i
            rows = pl.ds(pl.multiple_of(ci * CHUNK_B, CHUNK_B), CHUNK_B)
            gcs = gc_ref[d, rows, :]
            bts = beta_ref[rows, :]
            for g in range(heads):
                lanes = slice(g * LANES, (g + 1) * LANES)
                chains.append((d, g, rows, lanes, gcs, bts, sf_ref[d, g],
                               qs_ref[rows, lanes], ks_ref[rows, lanes], vs_ref[rows, lanes]))
        gens = []
        for d, g, rows, lanes, gcs, bts, s, q, k, v in chains:
            col = d * H_B + hg * heads + g
            gc = jnp.sum(jnp.where(lane == BA_LANE0 + 2 * H_B + col, gcs, 0.0), -1, keepdims=True)
            beta = jnp.sum(jnp.where(lane == BA_LANE0 + col, bts, 0.0), -1, keepdims=True)
            gens.append(_gdn_chunk(s, q, k, v, gc, beta, d == 1))
        for (d, g, rows, lanes, *_), (o, s_new) in zip(chains, _run_lockstep(gens)):
            sf_ref[d, g] = s_new
            acc_ref[d, rows, lanes] = o
        return carry

    lax.fori_loop(0, n, body, 0)
    for g in range(heads):
        lanes = slice(g * LANES, (g + 1) * LANES)
        for r0 in range(0, t, blk):
            rows = slice(r0, r0 + blk)
            o = acc_ref[0, rows, lanes] + acc_ref[1, rows, lanes]
            o = o * lax.rsqrt(jnp.mean(o * o, -1, keepdims=True) + RMS_EPS) * gnw_ref[...]
            o_ref[rows, lanes] = (o * _silu(z_ref[rows, lanes])).astype(o_ref.dtype)


def _gdn_scan(p, conv_w, alog_vec, dtb_vec, gn_w, s0, n_seq, t, row_blk0, heads=4):
    w = heads * LANES
    nb = C_B // w
    rows = lambda off: pl.BlockSpec((t, w), lambda b, h: (row_blk0 + b, off + h))
    cw = lambda off: pl.BlockSpec((3, w), lambda b, h: (0, off + h))
    vec = pl.BlockSpec((1, LANES), lambda b, h: (0, 0))
    state = pl.BlockSpec((None, 2, heads, HEAD_B, HEAD_B), lambda b, h: (b, 0, h, 0, 0))
    return pl.pallas_call(
        functools.partial(_gdn_scan_kernel, t=t, heads=heads),
        grid=(n_seq, nb),
        in_specs=[
            rows(0), rows(nb), rows(2 * nb), rows(3 * nb),
            pl.BlockSpec((t, LANES), lambda b, h: (row_blk0 + b, (4 * C_B + C_PA) // LANES)),
            cw(0), cw(nb), cw(2 * nb),
            vec, vec, vec,
            state,
        ],
        out_specs=[pl.BlockSpec((t, w), lambda b, h: (b, h)), state],
        out_shape=[
            jax.ShapeDtypeStruct((n_seq * t, C_B), BF16),
            jax.ShapeDtypeStruct((n_seq, 2, H_B, HEAD_B, HEAD_B), F32),
        ],
        scratch_shapes=[pltpu.VMEM((t, w), F32)] * 3 + [pltpu.VMEM((2, t, w), F32), pltpu.VMEM((2, t, LANES), F32),
                                                         pltpu.VMEM((t, LANES), F32)],
        compiler_params=_cparams("parallel", "parallel"),
        name="gdn_scan",
    )(p, p, p, p, p, conv_w, conv_w, conv_w, gn_w, alog_vec, dtb_vec, s0)


POOL_HALF = 2


def _pool_kernel(x_ref, nw_ref, sc_ref, sh_ref, w_ref, ps_ref, g_ref, o_ref, *, ctx_blocks, t_ctx):
    tb, d = x_ref.shape
    t = jnp.where(pl.program_id(0) < ctx_blocks, t_ctx, tb)
    row = lax.broadcasted_iota(jnp.int32, (tb, C_G), 0) & (t - 1)

    def back(x, n):
        return _shift_rows(x, n, row >= n)

    def fwd(x, n):
        return _shift_rows(x, -n, row < t - n)

    ss = jnp.zeros((tb, 1), F32)
    for g in range(POOL_GROUPS):
        xg = x_ref[:, g * C_G:(g + 1) * C_G]
        ss = ss + jnp.sum(xg * xg, -1, keepdims=True)
    rs = lax.rsqrt(ss * (1.0 / d) + RMS_EPS)

    def group(g, out_cols):
        cols = slice(g * C_G, (g + 1) * C_G)
        xg = x_ref[:, cols]
        h = (xg * rs * nw_ref[:, cols]) * (1.0 + sc_ref[:, cols]) + sh_ref[:, cols]
        half = 1 << g
        ph, qh, n = h, h, 1
        while n < half:
            ph = back(ph, n) + ph
            qh = qh + fwd(qh, n)
            n *= 2
        s = back(ph, 1) + qh
        count = jnp.clip(row + half, 0, t) - jnp.clip(row - half, 0, t)
        p = s / count.astype(F32) - h
        y = _bdot(p, w_ref[g % POOL_HALF]) * ps_ref[:, cols]
        o_ref[:, out_cols] = xg + g_ref[:, cols] * y

    for half_idx in range(POOL_GROUPS // POOL_HALF):
        @pl.when(pl.program_id(1) == half_idx)
        def _(half_idx=half_idx):
            for gg in range(POOL_HALF):
                group(half_idx * POOL_HALF + gg, slice(gg * C_G, (gg + 1) * C_G))


def _pool_mixer(x, nw, scale_blk, shift_blk, w_pool, layer, pool_scale, gate_blk, t, ctx_blocks, t_ctx):
    m, d = x.shape
    wide = POOL_HALF * C_G
    vec = lambda: pl.BlockSpec((1, d), lambda b, j: (0, 0))
    blk_vec = lambda: pl.BlockSpec((None, 1, d), lambda b, j: (b, 0, 0))
    return pl.pallas_call(
        functools.partial(_pool_kernel, ctx_blocks=ctx_blocks, t_ctx=t_ctx),
        grid=(m // t, POOL_GROUPS // POOL_HALF),
        in_specs=[
            pl.BlockSpec((t, d), lambda b, j: (b, 0)),
            vec(), blk_vec(), blk_vec(),
            pl.BlockSpec((None, POOL_HALF, C_G, C_G), lambda b, j: (layer, j, 0, 0)),
            vec(), blk_vec(),
        ],
        out_specs=pl.BlockSpec((t, wide), lambda b, j: (b, j)),
        out_shape=jax.ShapeDtypeStruct((m, d), F32),
        compiler_params=_cparams("parallel", "arbitrary"),
        name="pool_mixer",
    )(x, nw.reshape(1, d), scale_blk, shift_blk, w_pool, pool_scale, gate_blk)


def _pad_cols(w, n):
    return jnp.pad(w, ((0, 0), (0, n - w.shape[1])))


def _block_diag_pairs(s):
    b, two, h, n, _ = s.shape
    sp = s.reshape(b, two, h // 2, 2, n, n)
    z = jnp.zeros_like(sp[:, :, :, 0])
    top = jnp.concatenate([sp[:, :, :, 0], z], -1)
    bot = jnp.concatenate([z, sp[:, :, :, 1]], -1)
    return jnp.concatenate([top, bot], -2)


def _diag_pairs(s_bd):
    b, two, hp, _, _ = s_bd.shape
    n = HEAD_A
    return jnp.stack([s_bd[..., :n, :n], s_bd[..., n:, n:]], 3).reshape(b, two, 2 * hp, n, n)


def kernel(x_prompt, x_sample, state_rwkv, state_delta, c, c_ctx, mod_w, mod_b, norm_mix_w, norm_ffn_w, norm_final_w, ab_w_in, ab_w_out, rwkv_mu, rwkv_w0, rwkv_w2, rwkv_a0, rwkv_a2, rwkv_g2, rwkv_k_k, rwkv_k_a, rwkv_r_k, rwkv_ln_w, rwkv_ln_b, gdn_conv_w, gdn_a_log, gdn_dt_bias, gdn_norm_w, pool_w, pool_scale, ffn_w_gate, ffn_w_up, ffn_w_down):
    bp, tp, d = x_prompt.shape
    bs, ts, _ = x_sample.shape
    mp, ms = bp * tp, bs * ts
    m = mp + ms
    assert tp == ROW_BLK and ts % ROW_BLK == 0 and mp % ts == 0
    x = jnp.concatenate([x_prompt.reshape(mp, d), x_sample.reshape(ms, d)], 0)

    n_cond = 16
    cond = jnp.zeros((n_cond, d), F32).at[0].set(c_ctx).at[1:1 + bs].set(c)
    mod = _mod_all(cond, mod_w, mod_b).reshape(DEPTH, n_cond, 6, d)
    mod_blk = jnp.concatenate([
        jnp.broadcast_to(mod[:, 0:1], (DEPTH, mp // ROW_BLK, 6, d)),
        jnp.repeat(mod[:, 1:1 + bs], ts // ROW_BLK, axis=1)], 1)[:, :, :, None, :]
    zeros_blk = jnp.zeros((m // ROW_BLK, 1, d), F32)

    zero_r = jnp.zeros((bp, 2, H_A // 2, LANES, LANES), F32)
    zero_d = jnp.zeros((bp, 2, H_B, HEAD_B, HEAD_B), F32)
    new_r, new_d = [], []

    for l in range(DEPTH):
        shift_m, scale_m, gate_m, shift_f, scale_f, gate_f = (mod_blk[l, :, j] for j in range(6))
        i = l // 2
        if l % 2 == 0:
            w_in = ab_w_in[i]
            w_p = jnp.concatenate([
                w_in[:, C_PA:C_PA + 4 * C_B], w_in[:, :C_PA], w_in[:, C_PA + 4 * C_B:],
                jnp.zeros((d, C_PA_PAD - C_PA - 4 * H_B), F32)], 1).astype(BF16)
            p = _norm_matmul(x, norm_mix_w[l], scale_m, shift_m, w_p)
            mu = _pad_cols(rwkv_mu[i][None, :], C_PA_PAD)
            pam = _token_shift(p, mu, ts, mp // ts, tp)
            w2bd = jnp.zeros((2, 2 * LORA, C_A), F32).at[0, :LORA].set(rwkv_w2[i, 0]).at[1, LORA:].set(rwkv_w2[i, 1])
            a2bd = jnp.zeros((2, 2 * LORA, C_A), F32).at[0, :LORA].set(rwkv_a2[i, 0]).at[1, LORA:].set(rwkv_a2[i, 1])
            g2pad = jnp.pad(rwkv_g2[i], ((0, C_LO - 4 * LORA - GATE_LORA), (0, 0)))
            lora = (w2bd.astype(BF16), a2bd.astype(BF16), g2pad.astype(BF16),
                    rwkv_w0[i][:, None, :], rwkv_a0[i][:, None, :])
            vecs = [v_[i].reshape(1, C_A) for v_ in (rwkv_k_k, rwkv_k_a, rwkv_r_k, rwkv_ln_w, rwkv_ln_b)]
            oa_p, sr_p = _rwkv_scan(pam, *lora, *vecs, zero_r, bp, tp, 0, H_A // 2)
            oa_s, _ = _rwkv_scan(pam, *lora, *vecs, _block_diag_pairs(state_rwkv[:, i]), bs, ts, mp // ts, H_A // 2)
            gnw = gdn_norm_w[i].reshape(1, HEAD_B)
            on_decay_lanes = lambda p_: jnp.zeros((1, LANES), F32).at[
                0, BA_LANE0 + 2 * H_B:BA_LANE0 + 4 * H_B].set(p_.reshape(2 * H_B))
            alog_vec, dtb_vec = on_decay_lanes(gdn_a_log[i]), on_decay_lanes(gdn_dt_bias[i])
            ob_p, sd_p = _gdn_scan(p, gdn_conv_w[i], alog_vec, dtb_vec, gnw, zero_d, bp, tp, 0, heads=H_B)
            ob_s, _ = _gdn_scan(p, gdn_conv_w[i], alog_vec, dtb_vec, gnw, state_delta[:, i],
                                bs, ts, mp // ts)
            new_r.append(_diag_pairs(sr_p))
            new_d.append(sd_p)
            o_a = jnp.concatenate([oa_p, oa_s], 0)
            o_b = jnp.concatenate([ob_p, ob_s], 0)
            x = _matmul_residual([o_a, o_b], ab_w_out, i, x, gate_m)
        else:
            per_blk = lambda j: jnp.concatenate(
                [jnp.broadcast_to(mod[l, 0:1, j], (mp // ts, d)), mod[l, 1:1 + bs, j]], 0)[:, None, :]
            x = _pool_mixer(x, norm_mix_w[l], per_blk(1), per_blk(0), pool_w, i, pool_scale[i].reshape(1, d),
                            per_blk(2), ts, mp // ts, tp)
        act = _ffn_up(x, norm_ffn_w[l], scale_f, shift_f, ffn_w_gate, ffn_w_up, l)
        x = _matmul_residual([act], ffn_w_down, l, x, gate_f, single_buffer_w=True)

    y_prompt = _norm_mod(x, norm_final_w, zeros_blk, zeros_blk, F32, 0, mp // ROW_BLK).reshape(bp, tp, d)
    y_sample = _norm_mod(x, norm_final_w, zeros_blk, zeros_blk, F32, mp // ROW_BLK, ms // ROW_BLK).reshape(bs, ts, d)
    return (y_prompt, y_sample, jnp.stack(new_r, 1), jnp.stack(new_d, 1))
```

```python
import functools

import jax
import jax.numpy as jnp
from jax import lax
from jax.experimental import pallas as pl
from jax.experimental.pallas import tpu as pltpu

F32 = jnp.float32
BF16 = jnp.bfloat16

D_MODEL = 2048
DEPTH = 4
GRID_W = 64
N_AB = 2
C_A = 1024
HEAD_A = 64
H_A = 16
LORA = 64
GATE_LORA = 160
C_B = 1024
HEAD_B = 128
H_B = 8
C_PA = 3 * C_A + 4 * LORA + GATE_LORA
C_PA_PAD = 3584
C_LO = C_PA_PAD - 3 * C_A
POOL_GROUPS = 4
C_G = D_MODEL // POOL_GROUPS
D_FF = 5632
RMS_EPS = 1e-6
GN_EPS = 64e-5
CHUNK = 64
CHUNK_B = 128
ROW_BLK = 256
LANES = 128
BA_LANE0 = C_PA % LANES
VMEM_LIMIT = 48 * 1024 * 1024

NT = (((1,), (1,)), ((), ()))
TN = (((0,), (0,)), ((), ()))
NN = (((1,), (0,)), ((), ()))


def _cparams(*sem, vmem=VMEM_LIMIT):
    return pltpu.CompilerParams(dimension_semantics=sem, vmem_limit_bytes=vmem)


def _bdot(a, b, dims=NN):
    return lax.dot_general(a.astype(BF16), b.astype(BF16), dims, preferred_element_type=F32)


def _fdot(a, b, dims=NN):
    return lax.dot_general(a, b, dims, precision=lax.Precision.HIGHEST, preferred_element_type=F32)


def _sigmoid(x):
    return 0.5 + 0.5 * jnp.tanh(0.5 * x)


def _silu(x):
    h = 0.5 * x
    return h + h * jnp.tanh(h)


def _softplus(x):
    return jnp.maximum(x, 0.0) + jnp.log(1.0 + jnp.exp(-jnp.abs(x)))


def _mod_kernel(c_ref, w_ref, b_ref, o_ref):
    act = _silu(c_ref[...])
    w = w_ref[...]
    a_hi = act.astype(BF16)
    a_lo = (act - a_hi.astype(F32)).astype(BF16)
    w_hi = w.astype(BF16)
    w_lo = (w - w_hi.astype(F32)).astype(BF16)
    n_c = act.shape[0]
    both = jnp.dot(jnp.concatenate([a_hi, a_lo], 0), w_hi, preferred_element_type=F32)
    o_ref[...] = both[:n_c] + both[n_c:] + jnp.dot(a_hi, w_lo, preferred_element_type=F32) + b_ref[...]


def _mod_all(cond, mod_w, mod_b):
    n_c, d = cond.shape
    depth, _, n = mod_w.shape
    tn = 1024
    return pl.pallas_call(
        _mod_kernel,
        grid=(depth, n // tn),
        in_specs=[
            pl.BlockSpec((n_c, d), lambda l, j: (0, 0)),
            pl.BlockSpec((None, d, tn), lambda l, j: (l, 0, j)),
            pl.BlockSpec((None, 1, tn), lambda l, j: (l, 0, j)),
        ],
        out_specs=pl.BlockSpec((None, n_c, tn), lambda l, j: (l, 0, j)),
        out_shape=jax.ShapeDtypeStruct((depth, n_c, n), F32),
        compiler_params=_cparams("parallel", "parallel"),
        name="mod_all",
    )(cond, mod_w, mod_b.reshape(depth, 1, n))


def _norm_mod_kernel(x_ref, nw_ref, sc_ref, sh_ref, o_ref):
    x = x_ref[...]
    ms = jnp.mean(x * x, -1, keepdims=True)
    y = x * lax.rsqrt(ms + RMS_EPS) * nw_ref[...]
    o_ref[...] = (y * (1.0 + sc_ref[...]) + sh_ref[...]).astype(o_ref.dtype)


def _norm_mod(x, nw, scale, shift, out_dtype, blk0=0, n_blk=None):
    m, d = x.shape
    nb = m // ROW_BLK if n_blk is None else n_blk
    return pl.pallas_call(
        _norm_mod_kernel,
        grid=(nb,),
        in_specs=[
            pl.BlockSpec((ROW_BLK, d), lambda i: (blk0 + i, 0)),
            pl.BlockSpec((1, d), lambda i: (0, 0)),
            pl.BlockSpec((None, 1, d), lambda i: (blk0 + i, 0, 0)),
            pl.BlockSpec((None, 1, d), lambda i: (blk0 + i, 0, 0)),
        ],
        out_specs=pl.BlockSpec((ROW_BLK, d), lambda i: (i, 0)),
        out_shape=jax.ShapeDtypeStruct((nb * ROW_BLK, d), out_dtype),
        compiler_params=_cparams("parallel"),
        name="norm_mod",
    )(x, nw.reshape(1, d), scale, shift)


def _norm_mod_rows(x_ref, nw_ref, sc_ref, sh_ref, s):
    x = x_ref[s * ROW_BLK:(s + 1) * ROW_BLK, :]
    y = x * lax.rsqrt(jnp.mean(x * x, -1, keepdims=True) + RMS_EPS) * nw_ref[...]
    return (y * (1.0 + sc_ref[s]) + sh_ref[s]).astype(BF16)


def _norm_mm_kernel(x_ref, nw_ref, sc_ref, sh_ref, b_ref, o_ref, h_ref, *, tm):
    @pl.when(pl.program_id(1) == 0)
    def _():
        for s in range(tm // ROW_BLK):
            h_ref[s * ROW_BLK:(s + 1) * ROW_BLK, :] = _norm_mod_rows(x_ref, nw_ref, sc_ref, sh_ref, s)

    o_ref[...] = jnp.dot(h_ref[...], b_ref[...], preferred_element_type=F32)


def _norm_matmul(x, nw, scale, shift, b, tm=1024, tn=512):
    m, d = x.shape
    n = b.shape[1]
    sub = tm // ROW_BLK
    return pl.pallas_call(
        functools.partial(_norm_mm_kernel, tm=tm),
        grid=(m // tm, n // tn),
        in_specs=[
            pl.BlockSpec((tm, d), lambda i, j: (i, 0)),
            pl.BlockSpec((1, d), lambda i, j: (0, 0)),
            pl.BlockSpec((sub, 1, d), lambda i, j: (i, 0, 0)),
            pl.BlockSpec((sub, 1, d), lambda i, j: (i, 0, 0)),
            pl.BlockSpec((d, tn), lambda i, j: (0, j)),
        ],
        out_specs=pl.BlockSpec((tm, tn), lambda i, j: (i, j)),
        out_shape=jax.ShapeDtypeStruct((m, n), F32),
        scratch_shapes=[pltpu.VMEM((tm, d), BF16)],
        compiler_params=_cparams("parallel", "arbitrary"),
        name="norm_matmul",
    )(x, nw.reshape(1, d), scale, shift, b)


def _gated_residual(x_ref, g_ref, acc, o_ref, tm):
    for s in range(tm // ROW_BLK):
        rows = slice(s * ROW_BLK, (s + 1) * ROW_BLK)
        o_ref[rows, :] = x_ref[rows, :] + g_ref[s] * acc[rows, :]


def _cast_weight_tiles(w_refs, wb_refs):
    @pl.when(pl.program_id(1) == 0)
    def _():
        for w_ref, wb_ref in zip(w_refs, wb_refs):
            wb_ref[...] = w_ref[...].astype(BF16)


def _mm_res_kernel(*refs, n_in, tm):
    a_refs, w_refs = refs[:n_in], refs[n_in:2 * n_in]
    x_ref, g_ref, o_ref = refs[2 * n_in:2 * n_in + 3]
    wb_refs = refs[2 * n_in + 3:]
    _cast_weight_tiles(w_refs, wb_refs)
    acc = jnp.dot(a_refs[0][...], wb_refs[0][...], preferred_element_type=F32)
    for a_ref, wb_ref in zip(a_refs[1:], wb_refs[1:]):
        acc = acc + jnp.dot(a_ref[...], wb_ref[...], preferred_element_type=F32)
    _gated_residual(x_ref, g_ref, acc, o_ref, tm)


def _matmul_residual(a_list, w, layer, x, gate, tm=512, tn=512, single_buffer_w=False):
    m, n = x.shape
    sub = tm // ROW_BLK
    kb = w.shape[1] // len(a_list)
    w_mode = dict(pipeline_mode=pl.Buffered(1)) if single_buffer_w else {}
    in_specs = [pl.BlockSpec((tm, kb), lambda j, i: (i, 0)) for _ in a_list]
    in_specs += [pl.BlockSpec((None, kb, tn), functools.partial(lambda j, i, r: (layer, r, j), r=r), **w_mode)
                 for r in range(len(a_list))]
    in_specs += [
        pl.BlockSpec((tm, tn), lambda j, i: (i, j)),
        pl.BlockSpec((sub, 1, tn), lambda j, i: (i, 0, j)),
    ]
    return pl.pallas_call(
        functools.partial(_mm_res_kernel, n_in=len(a_list), tm=tm),
        grid=(n // tn, m // tm),
        in_specs=in_specs,
        out_specs=pl.BlockSpec((tm, tn), lambda j, i: (i, j)),
        out_shape=jax.ShapeDtypeStruct((m, n), F32),
        scratch_shapes=[pltpu.VMEM((kb, tn), BF16) for _ in a_list],
        compiler_params=_cparams("parallel", "arbitrary"),
        name="matmul_residual",
    )(*a_list, *([w] * len(a_list)), x, gate)


def _ffn_up_kernel(x_ref, nw_ref, sc_ref, sh_ref, wg_ref, wu_ref, o_ref, h_ref, *, tm):
    @pl.when(pl.program_id(1) == 0)
    def _():
        for s in range(tm // ROW_BLK):
            h_ref[s * ROW_BLK:(s + 1) * ROW_BLK, :] = _norm_mod_rows(x_ref, nw_ref, sc_ref, sh_ref, s)

    h = h_ref[...]
    g = jnp.dot(h, wg_ref[...].astype(BF16), preferred_element_type=F32)
    u = jnp.dot(h, wu_ref[...].astype(BF16), preferred_element_type=F32)
    o_ref[...] = (_silu(g) * u).astype(o_ref.dtype)


def _ffn_up(x, nw, scale, shift, wg, wu, layer, tm=1024, tn=512):
    m, d = x.shape
    n = wg.shape[2]
    sub = tm // ROW_BLK
    w_spec = lambda: pl.BlockSpec((None, d, tn), lambda i, j: (layer, 0, j))
    return pl.pallas_call(
        functools.partial(_ffn_up_kernel, tm=tm),
        grid=(m // tm, n // tn),
        in_specs=[
            pl.BlockSpec((tm, d), lambda i, j: (i, 0)),
            pl.BlockSpec((1, d), lambda i, j: (0, 0)),
            pl.BlockSpec((sub, 1, d), lambda i, j: (i, 0, 0)),
            pl.BlockSpec((sub, 1, d), lambda i, j: (i, 0, 0)),
            w_spec(), w_spec(),
        ],
        out_specs=pl.BlockSpec((tm, tn), lambda i, j: (i, j)),
        out_shape=jax.ShapeDtypeStruct((m, n), BF16),
        scratch_shapes=[pltpu.VMEM((tm, d), BF16)],
        compiler_params=_cparams("parallel", "arbitrary", vmem=(
            2 * (tm * d * 4 + 2 * d * tn * 4 + tm * tn * 2) + tm * d * 2
            + 2 * d * tn * 2 + 2 * tm * tn * 4 + 4 * 1024 * 1024)),
        name="ffn_up",
    )(x, nw.reshape(1, d), scale, shift, wg, wu)


def _shift_rows(x, shift, valid):
    t = x.shape[0]
    return jnp.where(valid, pltpu.roll(x, shift % t, 0), 0.0)


def _token_shift_kernel(p_ref, mu_ref, o_ref, *, ctx_blocks, t_ctx):
    x = p_ref[...]
    t, c = x.shape
    row = lax.broadcasted_iota(jnp.int32, (t, c), 0)
    lane = lax.broadcasted_iota(jnp.int32, (t, c), 1)
    mu = mu_ref[...]

    @pl.when(pl.program_id(0) < ctx_blocks)
    def _():
        pos = row % t_ctx
        s = jnp.where(lane % 2 == 0, _shift_rows(x, 1, pos != 0), _shift_rows(x, -1, pos != t_ctx - 1))
        o_ref[...] = x + (s - x) * mu

    @pl.when(pl.program_id(0) >= ctx_blocks)
    def _():
        col = row % GRID_W
        sel = lane % 4
        s = jnp.where(sel == 0, _shift_rows(x, 1, col != 0),
            jnp.where(sel == 1, _shift_rows(x, -1, col != GRID_W - 1),
            jnp.where(sel == 2, _shift_rows(x, GRID_W, row >= GRID_W),
                      _shift_rows(x, -GRID_W, row < t - GRID_W))))
        o_ref[...] = x + (s - x) * mu


def _token_shift(p, mu, t, ctx_blocks, t_ctx):
    tc = 512
    m = p.shape[0]
    first = 4 * C_B // tc
    return pl.pallas_call(
        functools.partial(_token_shift_kernel, ctx_blocks=ctx_blocks, t_ctx=t_ctx),
        grid=(m // t, C_PA_PAD // tc),
        in_specs=[
            pl.BlockSpec((t, tc), lambda b, j: (b, first + j)),
            pl.BlockSpec((1, tc), lambda b, j: (0, j)),
        ],
        out_specs=pl.BlockSpec((t, tc), lambda b, j: (b, j)),
        out_shape=jax.ShapeDtypeStruct((m, C_PA_PAD), F32),
        compiler_params=_cparams("parallel", "parallel"),
        name="token_shift",
    )(p, mu)


def _order_masks(n, reverse):
    rr = lax.broadcasted_iota(jnp.int32, (n, n), 0)
    cc = lax.broadcasted_iota(jnp.int32, (n, n), 1)
    if reverse:
        return cc >= rr, cc > rr
    return cc <= rr, cc < rr


def _run_lockstep(chains):
    results = [None] * len(chains)
    pending = {i: next(c) for i, c in enumerate(chains)}
    while pending:
        products = {i: fn(a, b, dims) for i, (fn, a, b, dims) in pending.items()}
        pending = {}
        for i, prod in products.items():
            try:
                pending[i] = chains[i].send(prod)
            except StopIteration as stop:
                results[i] = stop.value
    return results


def _triangular_inverse(l, order):
    n = l.shape[0]
    ri = lax.broadcasted_iota(jnp.int32, (n, n), 0)
    ci = lax.broadcasted_iota(jnp.int32, (n, n), 1)

    def joins(k, g):
        return (ri // (g * k) == ci // (g * k)) & ((ri // k) != (ci // k))

    x = (ri == ci).astype(F32) + jnp.where(joins(1, 2), l, 0.0)
    k = 2
    while k < order:
        g = 4 if 4 * k <= order else 2
        nil = yield _bdot, x, jnp.where(joins(k, g), l, 0.0), NN
        x = x + (yield _bdot, nil, x, NN)
        if g == 4:
            nil2 = yield _bdot, nil, nil, NN
            x = x + (yield _bdot, nil2, x, NN)
        k *= g
    return x


def _split_heads(x, first):
    return jnp.concatenate([jnp.where(first, x, 0.0), jnp.where(first, 0.0, x)], 0)


def _chunk_cumsum(x, reverse, chunk):
    t = x.shape[0]
    pos = lax.broadcasted_iota(jnp.int32, x.shape, 0) % chunk
    s = 1
    while s < chunk:
        if reverse:
            x = x + jnp.where(pos < chunk - s, pltpu.roll(x, t - s, 0), 0.0)
        else:
            x = x + jnp.where(pos >= s, pltpu.roll(x, s, 0), 0.0)
        s *= 2
    return x


def _rwkv_chunk(s_bd, r, lw, cum, k, v, a, b, reverse):
    c = CHUNK
    cum_ex = cum - lw
    tot = cum[0:1] if reverse else cum[c - 1:c]
    e_neg = jnp.exp(-cum)
    e_hat = jnp.exp(tot - cum)
    at = a * jnp.exp(cum_ex)
    rt = r * jnp.exp(cum)
    first = lax.broadcasted_iota(jnp.int32, (c, LANES), 1) < HEAD_A
    lhs = jnp.concatenate([at, rt], 0)
    rbk = jnp.concatenate([_split_heads(b * e_neg, first), _split_heads(k * e_neg, first)], 0)
    g = yield _bdot, lhs, rbk, NT
    ri_g = lax.broadcasted_iota(jnp.int32, (2 * c, 4 * c), 0)
    pos_c = lax.broadcasted_iota(jnp.int32, (2 * c, 4 * c), 1) % c
    pos_r = ri_g % c
    own = jnp.where(ri_g < c, 0, 1)
    keep = (pos_c > pos_r - own) if reverse else (pos_c < pos_r + own)
    g = jnp.where(keep, g, 0.0)
    l_ab = g[0:c, 0:2 * c]
    l_ak = g[0:c, 2 * c:4 * c]
    m_r = g[c:2 * c, :]
    v_bd = _split_heads(v, first)
    sa = yield _bdot, lhs, s_bd, NT
    rhs = sa[0:c] + (yield _bdot, l_ak, v_bd, NN)
    l_bd = _split_heads(l_ab, first)
    t_inv = yield from _triangular_inverse(l_bd, c)
    u_bd = yield _bdot, t_inv, _split_heads(rhs, first), NN
    y = sa[c:2 * c] + (yield _bdot, m_r, jnp.concatenate([u_bd, v_bd], 0), NN)
    u = u_bd[0:c] + u_bd[c:2 * c]
    upd = yield _bdot, jnp.concatenate([u, v], 0), jnp.concatenate([b * e_hat, k * e_hat], 0), TN
    ri = lax.broadcasted_iota(jnp.int32, (LANES, LANES), 0) < HEAD_A
    ci = lax.broadcasted_iota(jnp.int32, (LANES, LANES), 1) < HEAD_A
    s_new = s_bd * jnp.exp(tot) + jnp.where(ri == ci, upd, 0.0)
    return y, s_new


def _head_sum(x, first):
    s0 = jnp.sum(jnp.where(first, x, 0.0), -1, keepdims=True)
    s1 = jnp.sum(jnp.where(first, 0.0, x), -1, keepdims=True)
    return jnp.where(first, s0, s1)


def _rwkv_scan_kernel(r_ref, k_ref, v_ref, lo_ref, w2_ref, a2_ref, g2_ref, w0_ref, a0_ref,
                      kk_ref, ka_ref, rk_ref, lnw_ref, lnb_ref, s0_ref, o_ref, sf_ref, acc_ref, *, t, pairs):
    n = t // CHUNK
    first = lax.broadcasted_iota(jnp.int32, (CHUNK, LANES), 1) < HEAD_A
    sf_ref[...] = s0_ref[...]

    def body(i, carry):
        where, chains = [], []
        for d in range(2):
            reverse = d == 1
            ci = (n - 1 - i) if reverse else i
            rows = pl.ds(pl.multiple_of(ci * CHUNK, CHUNK), CHUNK)
            lo = lo_ref[rows, :]
            w_log = -_softplus(-(w0_ref[d] + _bdot(jnp.tanh(lo[:, 0:2 * LORA]), w2_ref[d]))) - 0.5
            lw_all = -jnp.exp(w_log)
            ic_all = _sigmoid(a0_ref[d] + _bdot(lo[:, 2 * LORA:4 * LORA], a2_ref[d]))
            cum_all = _chunk_cumsum(lw_all, reverse, CHUNK)
            for p in range(pairs):
                lanes = slice(p * LANES, (p + 1) * LANES)
                r = r_ref[rows, lanes]
                k = k_ref[rows, lanes]
                v = v_ref[rows, lanes]
                iclr = ic_all[:, lanes]
                kk = k * kk_ref[:, lanes]
                kk = kk * lax.rsqrt(_head_sum(kk * kk, first) + 1e-12)
                k_d = k * (1.0 + (iclr - 1.0) * ka_ref[:, lanes])
                where.append((d, p, rows, lanes, r, k_d, v))
                chains.append(_rwkv_chunk(sf_ref[d, p], r, lw_all[:, lanes], cum_all[:, lanes],
                                          k_d, v, -kk, kk * iclr, reverse))
        for (d, p, rows, lanes, r, k_d, v), (y, s_new) in zip(where, _run_lockstep(chains)):
            sf_ref[d, p] = s_new
            mu = _head_sum(y, first) * (1.0 / HEAD_A)
            yc = y - mu
            var = _head_sum(yc * yc, first) * (1.0 / HEAD_A)
            out = yc * lax.rsqrt(var + GN_EPS) * lnw_ref[:, lanes] + lnb_ref[:, lanes]
            acc_ref[d, rows, lanes] = out + _head_sum(r * k_d * rk_ref[:, lanes], first) * v
        return carry

    lax.fori_loop(0, n, body, 0)
    blk = 2 * CHUNK
    for r0 in range(0, t, blk):
        rows = slice(r0, r0 + blk)
        gate = _bdot(_sigmoid(lo_ref[rows, 4 * LORA:]), g2_ref[...])
        o_ref[rows, :] = ((acc_ref[0, rows, :] + acc_ref[1, rows, :]) * gate).astype(o_ref.dtype)


def _rwkv_scan(pam, w2bd, a2bd, g2pad, w0, a0, k_k, k_a, r_k, ln_w, ln_b, s0_bd, n_seq, t, row_blk0, pairs):
    w = pairs * LANES
    nb = C_A // w
    vec = lambda: pl.BlockSpec((1, w), lambda b, h: (0, h))
    vec2 = lambda: pl.BlockSpec((2, 1, w), lambda b, h: (0, 0, h))
    rows = lambda off: pl.BlockSpec((t, w), lambda b, h: (row_blk0 + b, off + h))
    state = pl.BlockSpec((None, 2, pairs, LANES, LANES), lambda b, h: (b, 0, h, 0, 0))
    return pl.pallas_call(
        functools.partial(_rwkv_scan_kernel, t=t, pairs=pairs),
        grid=(n_seq, nb),
        in_specs=[
            rows(0), rows(nb), rows(2 * nb),
            pl.BlockSpec((t, C_LO), lambda b, h: (row_blk0 + b, 3 * C_A // C_LO)),
            pl.BlockSpec((2, 2 * LORA, w), lambda b, h: (0, 0, h)),
            pl.BlockSpec((2, 2 * LORA, w), lambda b, h: (0, 0, h)),
            pl.BlockSpec((C_LO - 4 * LORA, w), lambda b, h: (0, h)),
            vec2(), vec2(),
            vec(), vec(), vec(), vec(), vec(),
            state,
        ],
        out_specs=[pl.BlockSpec((t, w), lambda b, h: (b, h)), state],
        out_shape=[
            jax.ShapeDtypeStruct((n_seq * t, C_A), BF16),
            jax.ShapeDtypeStruct((n_seq, 2, C_A // LANES, LANES, LANES), F32),
        ],
        scratch_shapes=[pltpu.VMEM((2, t, w), F32)],
        compiler_params=_cparams("parallel", "parallel", vmem=max(VMEM_LIMIT, (
            2 * (t * (3 * w * 4 + C_LO * 4 + w * 2) + 2 * 2 * pairs * LANES * LANES * 4)
            + 2 * t * w * 4 + 10 * 1024 * 1024))),
        name="rwkv_scan",
    )(pam, pam, pam, pam, w2bd, a2bd, g2pad, w0, a0, k_k, k_a, r_k, ln_w, ln_b, s0_bd)


def _gdn_chunk(s, q, k, v, gc, beta, reverse):
    c = q.shape[0]
    incl, strict = _order_masks(c, reverse)
    pick = (lax.broadcasted_iota(jnp.int32, (8, LANES), 1) == 0).astype(F32)
    gr = (yield _fdot, pick, jnp.broadcast_to(gc, (c, LANES)), NT)[0:1, :]
    decay = jnp.where(incl, jnp.exp(jnp.where(incl, gc - gr, 0.0)), 0.0)
    g_last = gc[0:1] if reverse else gc[c - 1:c]
    kb = k * beta
    l = jnp.where(strict, (yield _bdot, kb, k, NT) * decay, 0.0)
    attn = (yield _bdot, q, k, NT) * decay
    a_inv = yield from _triangular_inverse(-l, c)
    eg = jnp.exp(gc)
    sol = yield _bdot, a_inv, jnp.concatenate([v * beta, kb * eg], -1), NN
    u, w = sol[:, 0:HEAD_B], sol[:, HEAD_B:]
    ws = yield _bdot, jnp.concatenate([w, q * eg], 0), s, NN
    v_new = u - ws[0:c]
    o = ws[c:2 * c] + (yield _bdot, attn, v_new, NN)
    s_new = s * jnp.exp(g_last) + (yield _bdot, k * jnp.exp(g_last - gc), v_new, TN)
    return o, s_new


def _gdn_scan_kernel(q_ref, k_ref, v_ref, z_ref, ba_ref, cq_ref, ck_ref, cv_ref, gnw_ref, alog_ref, dtb_ref,
                     s0_ref, o_ref, sf_ref, qs_ref, ks_ref, vs_ref, acc_ref, gc_ref, beta_ref, *, t, heads):
    n = t // CHUNK_B
    hg = pl.program_id(1)
    blk = CHUNK_B
    brow = lax.broadcasted_iota(jnp.int32, (blk, LANES), 0)
    no_row = jnp.zeros((1, LANES), F32)

    def conv_silu(x_ref, w, r0, lanes):
        x = x_ref[r0:r0 + blk, lanes]
        before = x_ref[r0 - 1:r0, lanes] if r0 > 0 else no_row
        after = x_ref[r0 + blk:r0 + blk + 1, lanes] if r0 + blk < t else no_row
        prev = jnp.where(brow == 0, before, pltpu.roll(x, 1, 0))
        nxt = jnp.where(brow == blk - 1, after, pltpu.roll(x, blk - 1, 0))
        return _silu(w[0:1] * prev + w[1:2] * x + w[2:3] * nxt)

    def l2norm(x, scale=1.0):
        return x * (lax.rsqrt(jnp.sum(x * x, -1, keepdims=True) + 1e-6) * scale)

    for g in range(heads):
        lanes = slice(g * LANES, (g + 1) * LANES)
        wq, wk, wv = cq_ref[:, lanes], ck_ref[:, lanes], cv_ref[:, lanes]
        for r0 in range(0, t, blk):
            rows = slice(r0, r0 + blk)
            qs_ref[rows, lanes] = l2norm(conv_silu(q_ref, wq, r0, lanes), HEAD_B ** -0.5)
            ks_ref[rows, lanes] = l2norm(conv_silu(k_ref, wk, r0, lanes))
            vs_ref[rows, lanes] = conv_silu(v_ref, wv, r0, lanes)
    ba = ba_ref[...]
    beta_ref[...] = _sigmoid(ba)
    g_log = -jnp.exp(alog_ref[...]) * _softplus(ba + dtb_ref[...])
    gc_ref[0] = _chunk_cumsum(g_log, False, CHUNK_B)
    gc_ref[1] = _chunk_cumsum(g_log, True, CHUNK_B)
    sf_ref[...] = s0_ref[...]
    lane = lax.broadcasted_iota(jnp.int32, (CHUNK_B, LANES), 1)

    def body(i, carry):
        chains = []
        for d in range(2):
            ci = (n - 1 - i) if d == 1 else i
            rows = pl.ds(pl.multiple_of(ci * CHUNK_B, CHUNK_B), CHUNK_B)
            gcs = gc_ref[d, rows, :]
            bts = beta_ref[rows, :]
            for g in range(heads):
                lanes = slice(g * LANES, (g + 1) * LANES)
                chains.append((d, g, rows, lanes, gcs, bts, sf_ref[d, g],
                               qs_ref[rows, lanes], ks_ref[rows, lanes], vs_ref[rows, lanes]))
        gens = []
        for d, g, rows, lanes, gcs, bts, s, q, k, v in chains:
            col = d * H_B + hg * heads + g
            gc = jnp.sum(jnp.where(lane == BA_LANE0 + 2 * H_B + col, gcs, 0.0), -1, keepdims=True)
            beta = jnp.sum(jnp.where(lane == BA_LANE0 + col, bts, 0.0), -1, keepdims=True)
            gens.append(_gdn_chunk(s, q, k, v, gc, beta, d == 1))
        for (d, g, rows, lanes, *_), (o, s_new) in zip(chains, _run_lockstep(gens)):
            sf_ref[d, g] = s_new
            acc_ref[d, rows, lanes] = o
        return carry

    lax.fori_loop(0, n, body, 0)
    for g in range(heads):
        lanes = slice(g * LANES, (g + 1) * LANES)
        for r0 in range(0, t, blk):
            rows = slice(r0, r0 + blk)
            o = acc_ref[0, rows, lanes] + acc_ref[1, rows, lanes]
            o = o * lax.rsqrt(jnp.mean(o * o, -1, keepdims=True) + RMS_EPS) * gnw_ref[...]
            o_ref[rows, lanes] = (o * _silu(z_ref[rows, lanes])).astype(o_ref.dtype)


def _gdn_scan(p, conv_w, alog_vec, dtb_vec, gn_w, s0, n_seq, t, row_blk0, heads=4):
    w = heads * LANES
    nb = C_B // w
    rows = lambda off: pl.BlockSpec((t, w), lambda b, h: (row_blk0 + b, off + h))
    cw = lambda off: pl.BlockSpec((3, w), lambda b, h: (0, off + h))
    vec = pl.BlockSpec((1, LANES), lambda b, h: (0, 0))
    state = pl.BlockSpec((None, 2, heads, HEAD_B, HEAD_B), lambda b, h: (b, 0, h, 0, 0))
    return pl.pallas_call(
        functools.partial(_gdn_scan_kernel, t=t, heads=heads),
        grid=(n_seq, nb),
        in_specs=[
            rows(0), rows(nb), rows(2 * nb), rows(3 * nb),
            pl.BlockSpec((t, LANES), lambda b, h: (row_blk0 + b, (4 * C_B + C_PA) // LANES)),
            cw(0), cw(nb), cw(2 * nb),
            vec, vec, vec,
            state,
        ],
        out_specs=[pl.BlockSpec((t, w), lambda b, h: (b, h)), state],
        out_shape=[
            jax.ShapeDtypeStruct((n_seq * t, C_B), BF16),
            jax.ShapeDtypeStruct((n_seq, 2, H_B, HEAD_B, HEAD_B), F32),
        ],
        scratch_shapes=[pltpu.VMEM((t, w), F32)] * 3 + [pltpu.VMEM((2, t, w), F32), pltpu.VMEM((2, t, LANES), F32),
                                                         pltpu.VMEM((t, LANES), F32)],
        compiler_params=_cparams("parallel", "parallel"),
        name="gdn_scan",
    )(p, p, p, p, p, conv_w, conv_w, conv_w, gn_w, alog_vec, dtb_vec, s0)


POOL_HALF = 2


def _pool_kernel(x_ref, nw_ref, sc_ref, sh_ref, w_ref, ps_ref, g_ref, o_ref, *, ctx_blocks, t_ctx):
    tb, d = x_ref.shape
    t = jnp.where(pl.program_id(0) < ctx_blocks, t_ctx, tb)
    row = lax.broadcasted_iota(jnp.int32, (tb, C_G), 0) & (t - 1)

    def back(x, n):
        return _shift_rows(x, n, row >= n)

    def fwd(x, n):
        return _shift_rows(x, -n, row < t - n)

    ss = jnp.zeros((tb, 1), F32)
    for g in range(POOL_GROUPS):
        xg = x_ref[:, g * C_G:(g + 1) * C_G]
        ss = ss + jnp.sum(xg * xg, -1, keepdims=True)
    rs = lax.rsqrt(ss * (1.0 / d) + RMS_EPS)

    def group(g, out_cols):
        cols = slice(g * C_G, (g + 1) * C_G)
        xg = x_ref[:, cols]
        h = (xg * rs * nw_ref[:, cols]) * (1.0 + sc_ref[:, cols]) + sh_ref[:, cols]
        half = 1 << g
        ph, qh, n = h, h, 1
        while n < half:
            ph = back(ph, n) + ph
            qh = qh + fwd(qh, n)
            n *= 2
        s = back(ph, 1) + qh
        count = jnp.clip(row + half, 0, t) - jnp.clip(row - half, 0, t)
        p = s / count.astype(F32) - h
        y = _bdot(p, w_ref[g % POOL_HALF]) * ps_ref[:, cols]
        o_ref[:, out_cols] = xg + g_ref[:, cols] * y

    for half_idx in range(POOL_GROUPS // POOL_HALF):
        @pl.when(pl.program_id(1) == half_idx)
        def _(half_idx=half_idx):
            for gg in range(POOL_HALF):
                group(half_idx * POOL_HALF + gg, slice(gg * C_G, (gg + 1) * C_G))


def _pool_mixer(x, nw, scale_blk, shift_blk, w_pool, layer, pool_scale, gate_blk, t, ctx_blocks, t_ctx):
    m, d = x.shape
    wide = POOL_HALF * C_G
    vec = lambda: pl.BlockSpec((1, d), lambda b, j: (0, 0))
    blk_vec = lambda: pl.BlockSpec((None, 1, d), lambda b, j: (b, 0, 0))
    return pl.pallas_call(
        functools.partial(_pool_kernel, ctx_blocks=ctx_blocks, t_ctx=t_ctx),
        grid=(m // t, POOL_GROUPS // POOL_HALF),
        in_specs=[
            pl.BlockSpec((t, d), lambda b, j: (b, 0)),
            vec(), blk_vec(), blk_vec(),
            pl.BlockSpec((None, POOL_HALF, C_G, C_G), lambda b, j: (layer, j, 0, 0)),
            vec(), blk_vec(),
        ],
        out_specs=pl.BlockSpec((t, wide), lambda b, j: (b, j)),
        out_shape=jax.ShapeDtypeStruct((m, d), F32),
        compiler_params=_cparams("parallel", "arbitrary"),
        name="pool_mixer",
    )(x, nw.reshape(1, d), scale_blk, shift_blk, w_pool, pool_scale, gate_blk)


def _pad_cols(w, n):
    return jnp.pad(w, ((0, 0), (0, n - w.shape[1])))


def _block_diag_pairs(s):
    b, two, h, n, _ = s.shape
    sp = s.reshape(b, two, h // 2, 2, n, n)
    z = jnp.zeros_like(sp[:, :, :, 0])
    top = jnp.concatenate([sp[:, :, :, 0], z], -1)
    bot = jnp.concatenate([z, sp[:, :, :, 1]], -1)
    return jnp.concatenate([top, bot], -2)


def _diag_pairs(s_bd):
    b, two, hp, _, _ = s_bd.shape
    n = HEAD_A
    return jnp.stack([s_bd[..., :n, :n], s_bd[..., n:, n:]], 3).reshape(b, two, 2 * hp, n, n)


def kernel(x_prompt, x_sample, state_rwkv, state_delta, c, c_ctx, mod_w, mod_b, norm_mix_w, norm_ffn_w, norm_final_w, ab_w_in, ab_w_out, rwkv_mu, rwkv_w0, rwkv_w2, rwkv_a0, rwkv_a2, rwkv_g2, rwkv_k_k, rwkv_k_a, rwkv_r_k, rwkv_ln_w, rwkv_ln_b, gdn_conv_w, gdn_a_log, gdn_dt_bias, gdn_norm_w, pool_w, pool_scale, ffn_w_gate, ffn_w_up, ffn_w_down):
    bp, tp, d = x_prompt.shape
    bs, ts, _ = x_sample.shape
    mp, ms = bp * tp, bs * ts
    m = mp + ms
    assert tp == ROW_BLK and ts % ROW_BLK == 0 and mp % ts == 0
    x = jnp.concatenate([x_prompt.reshape(mp, d), x_sample.reshape(ms, d)], 0)

    n_cond = 16
    cond = jnp.zeros((n_cond, d), F32).at[0].set(c_ctx).at[1:1 + bs].set(c)
    mod = _mod_all(cond, mod_w, mod_b).reshape(DEPTH, n_cond, 6, d)
    mod_blk = jnp.concatenate([
        jnp.broadcast_to(mod[:, 0:1], (DEPTH, mp // ROW_BLK, 6, d)),
        jnp.repeat(mod[:, 1:1 + bs], ts // ROW_BLK, axis=1)], 1)[:, :, :, None, :]
    zeros_blk = jnp.zeros((m // ROW_BLK, 1, d), F32)

    zero_r = jnp.zeros((bp, 2, H_A // 2, LANES, LANES), F32)
    zero_d = jnp.zeros((bp, 2, H_B, HEAD_B, HEAD_B), F32)
    new_r, new_d = [], []

    for l in range(DEPTH):
        shift_m, scale_m, gate_m, shift_f, scale_f, gate_f = (mod_blk[l, :, j] for j in range(6))
        i = l // 2
        if l % 2 == 0:
            w_in = ab_w_in[i]
            w_p = jnp.concatenate([
                w_in[:, C_PA:C_PA + 4 * C_B], w_in[:, :C_PA], w_in[:, C_PA + 4 * C_B:],
                jnp.zeros((d, C_PA_PAD - C_PA - 4 * H_B), F32)], 1).astype(BF16)
            p = _norm_matmul(x, norm_mix_w[l], scale_m, shift_m, w_p)
            mu = _pad_cols(rwkv_mu[i][None, :], C_PA_PAD)
            pam = _token_shift(p, mu, ts, mp // ts, tp)
            w2bd = jnp.zeros((2, 2 * LORA, C_A), F32).at[0, :LORA].set(rwkv_w2[i, 0]).at[1, LORA:].set(rwkv_w2[i, 1])
            a2bd = jnp.zeros((2, 2 * LORA, C_A), F32).at[0, :LORA].set(rwkv_a2[i, 0]).at[1, LORA:].set(rwkv_a2[i, 1])
            g2pad = jnp.pad(rwkv_g2[i], ((0, C_LO - 4 * LORA - GATE_LORA), (0, 0)))
            lora = (w2bd.astype(BF16), a2bd.astype(BF16), g2pad.astype(BF16),
                    rwkv_w0[i][:, None, :], rwkv_a0[i][:, None, :])
            vecs = [v_[i].reshape(1, C_A) for v_ in (rwkv_k_k, rwkv_k_a, rwkv_r_k, rwkv_ln_w, rwkv_ln_b)]
            oa_p, sr_p = _rwkv_scan(pam, *lora, *vecs, zero_r, bp, tp, 0, H_A // 2)
            oa_s, _ = _rwkv_scan(pam, *lora, *vecs, _block_diag_pairs(state_rwkv[:, i]), bs, ts, mp // ts, H_A // 2)
            gnw = gdn_norm_w[i].reshape(1, HEAD_B)
            on_decay_lanes = lambda p_: jnp.zeros((1, LANES), F32).at[
                0, BA_LANE0 + 2 * H_B:BA_LANE0 + 4 * H_B].set(p_.reshape(2 * H_B))
            alog_vec, dtb_vec = on_decay_lanes(gdn_a_log[i]), on_decay_lanes(gdn_dt_bias[i])
            ob_p, sd_p = _gdn_scan(p, gdn_conv_w[i], alog_vec, dtb_vec, gnw, zero_d, bp, tp, 0, heads=H_B)
            ob_s, _ = _gdn_scan(p, gdn_conv_w[i], alog_vec, dtb_vec, gnw, state_delta[:, i],
                                bs, ts, mp // ts)
            new_r.append(_diag_pairs(sr_p))
            new_d.append(sd_p)
            o_a = jnp.concatenate([oa_p, oa_s], 0)
            o_b = jnp.concatenate([ob_p, ob_s], 0)
            x = _matmul_residual([o_a, o_b], ab_w_out, i, x, gate_m, tm=1024)
        else:
            per_blk = lambda j: jnp.concatenate(
                [jnp.broadcast_to(mod[l, 0:1, j], (mp // ts, d)), mod[l, 1:1 + bs, j]], 0)[:, None, :]
            x = _pool_mixer(x, norm_mix_w[l], per_blk(1), per_blk(0), pool_w, i, pool_scale[i].reshape(1, d),
                            per_blk(2), ts, mp // ts, tp)
        act = _ffn_up(x, norm_ffn_w[l], scale_f, shift_f, ffn_w_gate, ffn_w_up, l)
        x = _matmul_residual([act], ffn_w_down, l, x, gate_f, single_buffer_w=True)

    y_prompt = _norm_mod(x, norm_final_w, zeros_blk, zeros_blk, F32, 0, mp // ROW_BLK).reshape(bp, tp, d)
    y_sample = _norm_mod(x, norm_final_w, zeros_blk, zeros_blk, F32, mp // ROW_BLK, ms // ROW_BLK).reshape(bs, ts, d)
    return (y_prompt, y_sample, jnp.stack(new_r, 1), jnp.stack(new_d, 1))
```

```python
import functools

import jax
import jax.numpy as jnp
from jax import lax
from jax.experimental import pallas as pl
from jax.experimental.pallas import tpu as pltpu

F32 = jnp.float32
BF16 = jnp.bfloat16

D_MODEL = 2048
DEPTH = 4
GRID_W = 64
N_AB = 2
C_A = 1024
HEAD_A = 64
H_A = 16
LORA = 64
GATE_LORA = 160
C_B = 1024
HEAD_B = 128
H_B = 8
C_PA = 3 * C_A + 4 * LORA + GATE_LORA
C_PA_PAD = 3584
C_LO = C_PA_PAD - 3 * C_A
POOL_GROUPS = 4
C_G = D_MODEL // POOL_GROUPS
D_FF = 5632
RMS_EPS = 1e-6
GN_EPS = 64e-5
CHUNK = 64
CHUNK_B = 128
ROW_BLK = 256
LANES = 128
BA_LANE0 = C_PA % LANES
VMEM_LIMIT = 48 * 1024 * 1024

NT = (((1,), (1,)), ((), ()))
TN = (((0,), (0,)), ((), ()))
NN = (((1,), (0,)), ((), ()))


def _cparams(*sem, vmem=VMEM_LIMIT):
    return pltpu.CompilerParams(dimension_semantics=sem, vmem_limit_bytes=vmem)


def _bdot(a, b, dims=NN):
    return lax.dot_general(a.astype(BF16), b.astype(BF16), dims, preferred_element_type=F32)


def _fdot(a, b, dims=NN):
    return lax.dot_general(a, b, dims, precision=lax.Precision.HIGHEST, preferred_element_type=F32)


def _sigmoid(x):
    return 0.5 + 0.5 * jnp.tanh(0.5 * x)


def _silu(x):
    h = 0.5 * x
    return h + h * jnp.tanh(h)


def _softplus(x):
    return jnp.maximum(x, 0.0) + jnp.log(1.0 + jnp.exp(-jnp.abs(x)))


def _mod_kernel(c_ref, w_ref, b_ref, o_ref):
    act = _silu(c_ref[...])
    w = w_ref[...]
    a_hi = act.astype(BF16)
    a_lo = (act - a_hi.astype(F32)).astype(BF16)
    w_hi = w.astype(BF16)
    w_lo = (w - w_hi.astype(F32)).astype(BF16)
    n_c = act.shape[0]
    both = jnp.dot(jnp.concatenate([a_hi, a_lo], 0), w_hi, preferred_element_type=F32)
    o_ref[...] = both[:n_c] + both[n_c:] + jnp.dot(a_hi, w_lo, preferred_element_type=F32) + b_ref[...]


def _mod_all(cond, mod_w, mod_b):
    n_c, d = cond.shape
    depth, _, n = mod_w.shape
    tn = 1024
    return pl.pallas_call(
        _mod_kernel,
        grid=(depth, n // tn),
        in_specs=[
            pl.BlockSpec((n_c, d), lambda l, j: (0, 0)),
            pl.BlockSpec((None, d, tn), lambda l, j: (l, 0, j)),
            pl.BlockSpec((None, 1, tn), lambda l, j: (l, 0, j)),
        ],
        out_specs=pl.BlockSpec((None, n_c, tn), lambda l, j: (l, 0, j)),
        out_shape=jax.ShapeDtypeStruct((depth, n_c, n), F32),
        compiler_params=_cparams("parallel", "parallel"),
        name="mod_all",
    )(cond, mod_w, mod_b.reshape(depth, 1, n))


def _norm_mod_kernel(x_ref, nw_ref, sc_ref, sh_ref, o_ref):
    x = x_ref[...]
    ms = jnp.mean(x * x, -1, keepdims=True)
    y = x * lax.rsqrt(ms + RMS_EPS) * nw_ref[...]
    o_ref[...] = (y * (1.0 + sc_ref[...]) + sh_ref[...]).astype(o_ref.dtype)


def _norm_mod(x, nw, scale, shift, out_dtype, blk0=0, n_blk=None):
    m, d = x.shape
    nb = m // ROW_BLK if n_blk is None else n_blk
    return pl.pallas_call(
        _norm_mod_kernel,
        grid=(nb,),
        in_specs=[
            pl.BlockSpec((ROW_BLK, d), lambda i: (blk0 + i, 0)),
            pl.BlockSpec((1, d), lambda i: (0, 0)),
            pl.BlockSpec((None, 1, d), lambda i: (blk0 + i, 0, 0)),
            pl.BlockSpec((None, 1, d), lambda i: (blk0 + i, 0, 0)),
        ],
        out_specs=pl.BlockSpec((ROW_BLK, d), lambda i: (i, 0)),
        out_shape=jax.ShapeDtypeStruct((nb * ROW_BLK, d), out_dtype),
        compiler_params=_cparams("parallel"),
        name="norm_mod",
    )(x, nw.reshape(1, d), scale, shift)


def _norm_mod_rows(x_ref, nw_ref, sc_ref, sh_ref, s):
    x = x_ref[s * ROW_BLK:(s + 1) * ROW_BLK, :]
    y = x * lax.rsqrt(jnp.mean(x * x, -1, keepdims=True) + RMS_EPS) * nw_ref[...]
    return (y * (1.0 + sc_ref[s]) + sh_ref[s]).astype(BF16)


def _norm_mm_kernel(x_ref, nw_ref, sc_ref, sh_ref, b_ref, o_ref, h_ref, *, tm):
    @pl.when(pl.program_id(1) == 0)
    def _():
        for s in range(tm // ROW_BLK):
            h_ref[s * ROW_BLK:(s + 1) * ROW_BLK, :] = _norm_mod_rows(x_ref, nw_ref, sc_ref, sh_ref, s)

    o_ref[...] = jnp.dot(h_ref[...], b_ref[...], preferred_element_type=F32)


def _norm_matmul(x, nw, scale, shift, b, tm=1024, tn=1536):
    m, d = x.shape
    n = b.shape[1]
    sub = tm // ROW_BLK
    return pl.pallas_call(
        functools.partial(_norm_mm_kernel, tm=tm),
        grid=(m // tm, n // tn),
        in_specs=[
            pl.BlockSpec((tm, d), lambda i, j: (i, 0)),
            pl.BlockSpec((1, d), lambda i, j: (0, 0)),
            pl.BlockSpec((sub, 1, d), lambda i, j: (i, 0, 0)),
            pl.BlockSpec((sub, 1, d), lambda i, j: (i, 0, 0)),
            pl.BlockSpec((d, tn), lambda i, j: (0, j)),
        ],
        out_specs=pl.BlockSpec((tm, tn), lambda i, j: (i, j)),
        out_shape=jax.ShapeDtypeStruct((m, n), F32),
        scratch_shapes=[pltpu.VMEM((tm, d), BF16)],
        compiler_params=_cparams("parallel", "arbitrary", vmem=(
            2 * (tm * d * 4 + d * tn * 2 + tm * tn * 4) + tm * d * 2 + tm * tn * 4 + 4 * 1024 * 1024)),
        name="norm_matmul",
    )(x, nw.reshape(1, d), scale, shift, b)


def _gated_residual(x_ref, g_ref, acc, o_ref, tm):
    for s in range(tm // ROW_BLK):
        rows = slice(s * ROW_BLK, (s + 1) * ROW_BLK)
        o_ref[rows, :] = x_ref[rows, :] + g_ref[s] * acc[rows, :]


def _cast_weight_tiles(w_refs, wb_refs):
    @pl.when(pl.program_id(1) == 0)
    def _():
        for w_ref, wb_ref in zip(w_refs, wb_refs):
            wb_ref[...] = w_ref[...].astype(BF16)


def _mm_res_kernel(*refs, n_in, tm):
    a_refs, w_refs = refs[:n_in], refs[n_in:2 * n_in]
    x_ref, g_ref, o_ref = refs[2 * n_in:2 * n_in + 3]
    wb_refs = refs[2 * n_in + 3:]
    _cast_weight_tiles(w_refs, wb_refs)
    acc = jnp.dot(a_refs[0][...], wb_refs[0][...], preferred_element_type=F32)
    for a_ref, wb_ref in zip(a_refs[1:], wb_refs[1:]):
        acc = acc + jnp.dot(a_ref[...], wb_ref[...], preferred_element_type=F32)
    _gated_residual(x_ref, g_ref, acc, o_ref, tm)


def _matmul_residual(a_list, w, layer, x, gate, tm=512, tn=512, single_buffer_w=False):
    m, n = x.shape
    sub = tm // ROW_BLK
    kb = w.shape[1] // len(a_list)
    w_mode = dict(pipeline_mode=pl.Buffered(1)) if single_buffer_w else {}
    in_specs = [pl.BlockSpec((tm, kb), lambda j, i: (i, 0)) for _ in a_list]
    in_specs += [pl.BlockSpec((None, kb, tn), functools.partial(lambda j, i, r: (layer, r, j), r=r), **w_mode)
                 for r in range(len(a_list))]
    in_specs += [
        pl.BlockSpec((tm, tn), lambda j, i: (i, j)),
        pl.BlockSpec((sub, 1, tn), lambda j, i: (i, 0, j)),
    ]
    return pl.pallas_call(
        functools.partial(_mm_res_kernel, n_in=len(a_list), tm=tm),
        grid=(n // tn, m // tm),
        in_specs=in_specs,
        out_specs=pl.BlockSpec((tm, tn), lambda j, i: (i, j)),
        out_shape=jax.ShapeDtypeStruct((m, n), F32),
        scratch_shapes=[pltpu.VMEM((kb, tn), BF16) for _ in a_list],
        compiler_params=_cparams("parallel", "arbitrary"),
        name="matmul_residual",
    )(*a_list, *([w] * len(a_list)), x, gate)


def _ffn_up_kernel(x_ref, nw_ref, sc_ref, sh_ref, wg_ref, wu_ref, o_ref, h_ref, *, tm):
    @pl.when(pl.program_id(1) == 0)
    def _():
        for s in range(tm // ROW_BLK):
            h_ref[s * ROW_BLK:(s + 1) * ROW_BLK, :] = _norm_mod_rows(x_ref, nw_ref, sc_ref, sh_ref, s)

    h = h_ref[...]
    g = jnp.dot(h, wg_ref[...].astype(BF16), preferred_element_type=F32)
    u = jnp.dot(h, wu_ref[...].astype(BF16), preferred_element_type=F32)
    o_ref[...] = (_silu(g) * u).astype(o_ref.dtype)


def _ffn_up(x, nw, scale, shift, wg, wu, layer, tm=1024, tn=512):
    m, d = x.shape
    n = wg.shape[2]
    sub = tm // ROW_BLK
    w_spec = lambda: pl.BlockSpec((None, d, tn), lambda i, j: (layer, 0, j))
    return pl.pallas_call(
        functools.partial(_ffn_up_kernel, tm=tm),
        grid=(m // tm, n // tn),
        in_specs=[
            pl.BlockSpec((tm, d), lambda i, j: (i, 0)),
            pl.BlockSpec((1, d), lambda i, j: (0, 0)),
            pl.BlockSpec((sub, 1, d), lambda i, j: (i, 0, 0)),
            pl.BlockSpec((sub, 1, d), lambda i, j: (i, 0, 0)),
            w_spec(), w_spec(),
        ],
        out_specs=pl.BlockSpec((tm, tn), lambda i, j: (i, j)),
        out_shape=jax.ShapeDtypeStruct((m, n), BF16),
        scratch_shapes=[pltpu.VMEM((tm, d), BF16)],
        compiler_params=_cparams("parallel", "arbitrary", vmem=(
            2 * (tm * d * 4 + 2 * d * tn * 4 + tm * tn * 2) + tm * d * 2
            + 2 * d * tn * 2 + 2 * tm * tn * 4 + 4 * 1024 * 1024)),
        name="ffn_up",
    )(x, nw.reshape(1, d), scale, shift, wg, wu)


def _shift_rows(x, shift, valid):
    t = x.shape[0]
    return jnp.where(valid, pltpu.roll(x, shift % t, 0), 0.0)


def _token_shift_kernel(p_ref, mu_ref, o_ref, *, ctx_blocks, t_ctx):
    x = p_ref[...]
    t, c = x.shape
    row = lax.broadcasted_iota(jnp.int32, (t, c), 0)
    lane = lax.broadcasted_iota(jnp.int32, (t, c), 1)
    mu = mu_ref[...]

    @pl.when(pl.program_id(0) < ctx_blocks)
    def _():
        pos = row % t_ctx
        s = jnp.where(lane % 2 == 0, _shift_rows(x, 1, pos != 0), _shift_rows(x, -1, pos != t_ctx - 1))
        o_ref[...] = x + (s - x) * mu

    @pl.when(pl.program_id(0) >= ctx_blocks)
    def _():
        col = row % GRID_W
        sel = lane % 4
        s = jnp.where(sel == 0, _shift_rows(x, 1, col != 0),
            jnp.where(sel == 1, _shift_rows(x, -1, col != GRID_W - 1),
            jnp.where(sel == 2, _shift_rows(x, GRID_W, row >= GRID_W),
                      _shift_rows(x, -GRID_W, row < t - GRID_W))))
        o_ref[...] = x + (s - x) * mu


def _token_shift(p, mu, t, ctx_blocks, t_ctx):
    tc = 512
    m = p.shape[0]
    first = 4 * C_B // tc
    return pl.pallas_call(
        functools.partial(_token_shift_kernel, ctx_blocks=ctx_blocks, t_ctx=t_ctx),
        grid=(m // t, C_PA_PAD // tc),
        in_specs=[
            pl.BlockSpec((t, tc), lambda b, j: (b, first + j)),
            pl.BlockSpec((1, tc), lambda b, j: (0, j)),
        ],
        out_specs=pl.BlockSpec((t, tc), lambda b, j: (b, j)),
        out_shape=jax.ShapeDtypeStruct((m, C_PA_PAD), F32),
        compiler_params=_cparams("parallel", "parallel"),
        name="token_shift",
    )(p, mu)


def _order_masks(n, reverse):
    rr = lax.broadcasted_iota(jnp.int32, (n, n), 0)
    cc = lax.broadcasted_iota(jnp.int32, (n, n), 1)
    if reverse:
        return cc >= rr, cc > rr
    return cc <= rr, cc < rr


def _run_lockstep(chains):
    results = [None] * len(chains)
    pending = {i: next(c) for i, c in enumerate(chains)}
    while pending:
        products = {i: fn(a, b, dims) for i, (fn, a, b, dims) in pending.items()}
        pending = {}
        for i, prod in products.items():
            try:
                pending[i] = chains[i].send(prod)
            except StopIteration as stop:
                results[i] = stop.value
    return results


def _triangular_inverse(l, order):
    n = l.shape[0]
    ri = lax.broadcasted_iota(jnp.int32, (n, n), 0)
    ci = lax.broadcasted_iota(jnp.int32, (n, n), 1)

    def joins(k, g):
        return (ri // (g * k) == ci // (g * k)) & ((ri // k) != (ci // k))

    x = (ri == ci).astype(F32) + jnp.where(joins(1, 2), l, 0.0)
    k = 2
    while k < order:
        g = 4 if 4 * k <= order else 2
        nil = yield _bdot, x, jnp.where(joins(k, g), l, 0.0), NN
        x = x + (yield _bdot, nil, x, NN)
        if g == 4:
            nil2 = yield _bdot, nil, nil, NN
            x = x + (yield _bdot, nil2, x, NN)
        k *= g
    return x


def _split_heads(x, first):
    return jnp.concatenate([jnp.where(first, x, 0.0), jnp.where(first, 0.0, x)], 0)


def _chunk_cumsum(x, reverse, chunk):
    t = x.shape[0]
    pos = lax.broadcasted_iota(jnp.int32, x.shape, 0) % chunk
    s = 1
    while s < chunk:
        if reverse:
            x = x + jnp.where(pos < chunk - s, pltpu.roll(x, t - s, 0), 0.0)
        else:
            x = x + jnp.where(pos >= s, pltpu.roll(x, s, 0), 0.0)
        s *= 2
    return x


def _rwkv_chunk(s_bd, r, lw, cum, k, v, a, b, reverse):
    c = CHUNK
    cum_ex = cum - lw
    tot = cum[0:1] if reverse else cum[c - 1:c]
    e_neg = jnp.exp(-cum)
    e_hat = jnp.exp(tot - cum)
    at = a * jnp.exp(cum_ex)
    rt = r * jnp.exp(cum)
    first = lax.broadcasted_iota(jnp.int32, (c, LANES), 1) < HEAD_A
    lhs = jnp.concatenate([at, rt], 0)
    rbk = jnp.concatenate([_split_heads(b * e_neg, first), _split_heads(k * e_neg, first)], 0)
    g = yield _bdot, lhs, rbk, NT
    ri_g = lax.broadcasted_iota(jnp.int32, (2 * c, 4 * c), 0)
    pos_c = lax.broadcasted_iota(jnp.int32, (2 * c, 4 * c), 1) % c
    pos_r = ri_g % c
    own = jnp.where(ri_g < c, 0, 1)
    keep = (pos_c > pos_r - own) if reverse else (pos_c < pos_r + own)
    g = jnp.where(keep, g, 0.0)
    l_ab = g[0:c, 0:2 * c]
    l_ak = g[0:c, 2 * c:4 * c]
    m_r = g[c:2 * c, :]
    v_bd = _split_heads(v, first)
    sa = yield _bdot, lhs, s_bd, NT
    rhs = sa[0:c] + (yield _bdot, l_ak, v_bd, NN)
    l_bd = _split_heads(l_ab, first)
    t_inv = yield from _triangular_inverse(l_bd, c)
    u_bd = yield _bdot, t_inv, _split_heads(rhs, first), NN
    y = sa[c:2 * c] + (yield _bdot, m_r, jnp.concatenate([u_bd, v_bd], 0), NN)
    u = u_bd[0:c] + u_bd[c:2 * c]
    upd = yield _bdot, jnp.concatenate([u, v], 0), jnp.concatenate([b * e_hat, k * e_hat], 0), TN
    ri = lax.broadcasted_iota(jnp.int32, (LANES, LANES), 0) < HEAD_A
    ci = lax.broadcasted_iota(jnp.int32, (LANES, LANES), 1) < HEAD_A
    s_new = s_bd * jnp.exp(tot) + jnp.where(ri == ci, upd, 0.0)
    return y, s_new


def _head_sum(x, first):
    s0 = jnp.sum(jnp.where(first, x, 0.0), -1, keepdims=True)
    s1 = jnp.sum(jnp.where(first, 0.0, x), -1, keepdims=True)
    return jnp.where(first, s0, s1)


def _rwkv_scan_kernel(r_ref, k_ref, v_ref, lo_ref, w2_ref, a2_ref, g2_ref, w0_ref, a0_ref,
                      kk_ref, ka_ref, rk_ref, lnw_ref, lnb_ref, s0_ref, o_ref, sf_ref, acc_ref, *, t, pairs):
    n = t // CHUNK
    first = lax.broadcasted_iota(jnp.int32, (CHUNK, LANES), 1) < HEAD_A
    sf_ref[...] = s0_ref[...]

    def body(i, carry):
        where, chains = [], []
        for d in range(2):
            reverse = d == 1
            ci = (n - 1 - i) if reverse else i
            rows = pl.ds(pl.multiple_of(ci * CHUNK, CHUNK), CHUNK)
            lo = lo_ref[rows, :]
            w_log = -_softplus(-(w0_ref[d] + _bdot(jnp.tanh(lo[:, 0:2 * LORA]), w2_ref[d]))) - 0.5
            lw_all = -jnp.exp(w_log)
            ic_all = _sigmoid(a0_ref[d] + _bdot(lo[:, 2 * LORA:4 * LORA], a2_ref[d]))
            cum_all = _chunk_cumsum(lw_all, reverse, CHUNK)
            for p in range(pairs):
                lanes = slice(p * LANES, (p + 1) * LANES)
                r = r_ref[rows, lanes]
                k = k_ref[rows, lanes]
                v = v_ref[rows, lanes]
                iclr = ic_all[:, lanes]
                kk = k * kk_ref[:, lanes]
                kk = kk * lax.rsqrt(_head_sum(kk * kk, first) + 1e-12)
                k_d = k * (1.0 + (iclr - 1.0) * ka_ref[:, lanes])
                where.append((d, p, rows, lanes, r, k_d, v))
                chains.append(_rwkv_chunk(sf_ref[d, p], r, lw_all[:, lanes], cum_all[:, lanes],
                                          k_d, v, -kk, kk * iclr, reverse))
        for (d, p, rows, lanes, r, k_d, v), (y, s_new) in zip(where, _run_lockstep(chains)):
            sf_ref[d, p] = s_new
            mu = _head_sum(y, first) * (1.0 / HEAD_A)
            yc = y - mu
            var = _head_sum(yc * yc, first) * (1.0 / HEAD_A)
            out = yc * lax.rsqrt(var + GN_EPS) * lnw_ref[:, lanes] + lnb_ref[:, lanes]
            acc_ref[d, rows, lanes] = out + _head_sum(r * k_d * rk_ref[:, lanes], first) * v
        return carry

    lax.fori_loop(0, n, body, 0)
    blk = 2 * CHUNK
    for r0 in range(0, t, blk):
        rows = slice(r0, r0 + blk)
        gate = _bdot(_sigmoid(lo_ref[rows, 4 * LORA:]), g2_ref[...])
        o_ref[rows, :] = ((acc_ref[0, rows, :] + acc_ref[1, rows, :]) * gate).astype(o_ref.dtype)


def _rwkv_scan(pam, w2bd, a2bd, g2pad, w0, a0, k_k, k_a, r_k, ln_w, ln_b, s0_bd, n_seq, t, row_blk0, pairs):
    w = pairs * LANES
    nb = C_A // w
    vec = lambda: pl.BlockSpec((1, w), lambda b, h: (0, h))
    vec2 = lambda: pl.BlockSpec((2, 1, w), lambda b, h: (0, 0, h))
    rows = lambda off: pl.BlockSpec((t, w), lambda b, h: (row_blk0 + b, off + h))
    state = pl.BlockSpec((None, 2, pairs, LANES, LANES), lambda b, h: (b, 0, h, 0, 0))
    return pl.pallas_call(
        functools.partial(_rwkv_scan_kernel, t=t, pairs=pairs),
        grid=(n_seq, nb),
        in_specs=[
            rows(0), rows(nb), rows(2 * nb),
            pl.BlockSpec((t, C_LO), lambda b, h: (row_blk0 + b, 3 * C_A // C_LO)),
            pl.BlockSpec((2, 2 * LORA, w), lambda b, h: (0, 0, h)),
            pl.BlockSpec((2, 2 * LORA, w), lambda b, h: (0, 0, h)),
            pl.BlockSpec((C_LO - 4 * LORA, w), lambda b, h: (0, h)),
            vec2(), vec2(),
            vec(), vec(), vec(), vec(), vec(),
            state,
        ],
        out_specs=[pl.BlockSpec((t, w), lambda b, h: (b, h)), state],
        out_shape=[
            jax.ShapeDtypeStruct((n_seq * t, C_A), BF16),
            jax.ShapeDtypeStruct((n_seq, 2, C_A // LANES, LANES, LANES), F32),
        ],
        scratch_shapes=[pltpu.VMEM((2, t, w), F32)],
        compiler_params=_cparams("parallel", "parallel", vmem=max(VMEM_LIMIT, (
            2 * (t * (3 * w * 4 + C_LO * 4 + w * 2) + 2 * 2 * pairs * LANES * LANES * 4)
            + 2 * t * w * 4 + 10 * 1024 * 1024))),
        name="rwkv_scan",
    )(pam, pam, pam, pam, w2bd, a2bd, g2pad, w0, a0, k_k, k_a, r_k, ln_w, ln_b, s0_bd)


def _gdn_chunk(s, q, k, v, gc, beta, reverse):
    c = q.shape[0]
    incl, strict = _order_masks(c, reverse)
    pick = (lax.broadcasted_iota(jnp.int32, (8, LANES), 1) == 0).astype(F32)
    gr = (yield _fdot, pick, jnp.broadcast_to(gc, (c, LANES)), NT)[0:1, :]
    decay = jnp.where(incl, jnp.exp(jnp.where(incl, gc - gr, 0.0)), 0.0)
    g_last = gc[0:1] if reverse else gc[c - 1:c]
    kb = k * beta
    l = jnp.where(strict, (yield _bdot, kb, k, NT) * decay, 0.0)
    attn = (yield _bdot, q, k, NT) * decay
    a_inv = yield from _triangular_inverse(-l, c)
    eg = jnp.exp(gc)
    sol = yield _bdot, a_inv, jnp.concatenate([v * beta, kb * eg], -1), NN
    u, w = sol[:, 0:HEAD_B], sol[:, HEAD_B:]
    ws = yield _bdot, jnp.concatenate([w, q * eg], 0), s, NN
    v_new = u - ws[0:c]
    o = ws[c:2 * c] + (yield _bdot, attn, v_new, NN)
    s_new = s * jnp.exp(g_last) + (yield _bdot, k * jnp.exp(g_last - gc), v_new, TN)
    return o, s_new


def _gdn_scan_kernel(q_ref, k_ref, v_ref, z_ref, ba_ref, cq_ref, ck_ref, cv_ref, gnw_ref, alog_ref, dtb_ref,
                     s0_ref, o_ref, sf_ref, qs_ref, ks_ref, vs_ref, acc_ref, gc_ref, beta_ref, *, t, heads):
    n = t // CHUNK_B
    hg = pl.program_id(1)
    blk = CHUNK_B
    brow = lax.broadcasted_iota(jnp.int32, (blk, LANES), 0)
    no_row = jnp.zeros((1, LANES), F32)

    def conv_silu(x_ref, w, r0, lanes):
        x = x_ref[r0:r0 + blk, lanes]
        before = x_ref[r0 - 1:r0, lanes] if r0 > 0 else no_row
        after = x_ref[r0 + blk:r0 + blk + 1, lanes] if r0 + blk < t else no_row
        prev = jnp.where(brow == 0, before, pltpu.roll(x, 1, 0))
        nxt = jnp.where(brow == blk - 1, after, pltpu.roll(x, blk - 1, 0))
        return _silu(w[0:1] * prev + w[1:2] * x + w[2:3] * nxt)

    def l2norm(x, scale=1.0):
        return x * (lax.rsqrt(jnp.sum(x * x, -1, keepdims=True) + 1e-6) * scale)

    for g in range(heads):
        lanes = slice(g * LANES, (g + 1) * LANES)
        wq, wk, wv = cq_ref[:, lanes], ck_ref[:, lanes], cv_ref[:, lanes]
        for r0 in range(0, t, blk):
            rows = slice(r0, r0 + blk)
            qs_ref[rows, lanes] = l2norm(conv_silu(q_ref, wq, r0, lanes), HEAD_B ** -0.5)
            ks_ref[rows, lanes] = l2norm(conv_silu(k_ref, wk, r0, lanes))
            vs_ref[rows, lanes] = conv_silu(v_ref, wv, r0, lanes)
    ba = ba_ref[...]
    beta_ref[...] = _sigmoid(ba)
    g_log = -jnp.exp(alog_ref[...]) * _softplus(ba + dtb_ref[...])
    gc_ref[0] = _chunk_cumsum(g_log, False, CHUNK_B)
    gc_ref[1] = _chunk_cumsum(g_log, True, CHUNK_B)
    sf_ref[...] = s0_ref[...]
    lane = lax.broadcasted_iota(jnp.int32, (CHUNK_B, LANES), 1)

    def body(i, carry):
        chains = []
        for d in range(2):
            ci = (n - 1 - i) if d == 1 else i
            rows = pl.ds(pl.multiple_of(ci * CHUNK_B, CHUNK_B), CHUNK_B)
            gcs = gc_ref[d, rows, :]
            bts = beta_ref[rows, :]
            for g in range(heads):
                lanes = slice(g * LANES, (g + 1) * LANES)
                chains.append((d, g, rows, lanes, gcs, bts, sf_ref[d, g],
                               qs_ref[rows, lanes], ks_ref[rows, lanes], vs_ref[rows, lanes]))
        gens = []
        for d, g, rows, lanes, gcs, bts, s, q, k, v in chains:
            col = d * H_B + hg * heads + g
            gc = jnp.sum(jnp.where(lane == BA_LANE0 + 2 * H_B + col, gcs, 0.0), -1, keepdims=True)
            beta = jnp.sum(jnp.where(lane == BA_LANE0 + col, bts, 0.0), -1, keepdims=True)
            gens.append(_gdn_chunk(s, q, k, v, gc, beta, d == 1))
        for (d, g, rows, lanes, *_), (o, s_new) in zip(chains, _run_lockstep(gens)):
            sf_ref[d, g] = s_new
            acc_ref[d, rows, lanes] = o
        return carry

    lax.fori_loop(0, n, body, 0)
    for g in range(heads):
        lanes = slice(g * LANES, (g + 1) * LANES)
        for r0 in range(0, t, blk):
            rows = slice(r0, r0 + blk)
            o = acc_ref[0, rows, lanes] + acc_ref[1, rows, lanes]
            o = o * lax.rsqrt(jnp.mean(o * o, -1, keepdims=True) + RMS_EPS) * gnw_ref[...]
            o_ref[rows, lanes] = (o * _silu(z_ref[rows, lanes])).astype(o_ref.dtype)


def _gdn_scan(p, conv_w, alog_vec, dtb_vec, gn_w, s0, n_seq, t, row_blk0, heads=4):
    w = heads * LANES
    nb = C_B // w
    rows = lambda off: pl.BlockSpec((t, w), lambda b, h: (row_blk0 + b, off + h))
    cw = lambda off: pl.BlockSpec((3, w), lambda b, h: (0, off + h))
    vec = pl.BlockSpec((1, LANES), lambda b, h: (0, 0))
    state = pl.BlockSpec((None, 2, heads, HEAD_B, HEAD_B), lambda b, h: (b, 0, h, 0, 0))
    return pl.pallas_call(
        functools.partial(_gdn_scan_kernel, t=t, heads=heads),
        grid=(n_seq, nb),
        in_specs=[
            rows(0), rows(nb), rows(2 * nb), rows(3 * nb),
            pl.BlockSpec((t, LANES), lambda b, h: (row_blk0 + b, (4 * C_B + C_PA) // LANES)),
            cw(0), cw(nb), cw(2 * nb),
            vec, vec, vec,
            state,
        ],
        out_specs=[pl.BlockSpec((t, w), lambda b, h: (b, h)), state],
        out_shape=[
            jax.ShapeDtypeStruct((n_seq * t, C_B), BF16),
            jax.ShapeDtypeStruct((n_seq, 2, H_B, HEAD_B, HEAD_B), F32),
        ],
        scratch_shapes=[pltpu.VMEM((t, w), F32)] * 3 + [pltpu.VMEM((2, t, w), F32), pltpu.VMEM((2, t, LANES), F32),
                                                         pltpu.VMEM((t, LANES), F32)],
        compiler_params=_cparams("parallel", "parallel"),
        name="gdn_scan",
    )(p, p, p, p, p, conv_w, conv_w, conv_w, gn_w, alog_vec, dtb_vec, s0)


POOL_HALF = 2


def _pool_kernel(x_ref, nw_ref, sc_ref, sh_ref, w_ref, ps_ref, g_ref, o_ref, *, ctx_blocks, t_ctx):
    tb, d = x_ref.shape
    t = jnp.where(pl.program_id(0) < ctx_blocks, t_ctx, tb)
    row = lax.broadcasted_iota(jnp.int32, (tb, C_G), 0) & (t - 1)

    def back(x, n):
        return _shift_rows(x, n, row >= n)

    def fwd(x, n):
        return _shift_rows(x, -n, row < t - n)

    ss = jnp.zeros((tb, 1), F32)
    for g in range(POOL_GROUPS):
        xg = x_ref[:, g * C_G:(g + 1) * C_G]
        ss = ss + jnp.sum(xg * xg, -1, keepdims=True)
    rs = lax.rsqrt(ss * (1.0 / d) + RMS_EPS)

    def group(g, out_cols):
        cols = slice(g * C_G, (g + 1) * C_G)
        xg = x_ref[:, cols]
        h = (xg * rs * nw_ref[:, cols]) * (1.0 + sc_ref[:, cols]) + sh_ref[:, cols]
        half = 1 << g
        ph, qh, n = h, h, 1
        while n < half:
            ph = back(ph, n) + ph
            qh = qh + fwd(qh, n)
            n *= 2
        s = back(ph, 1) + qh
        count = jnp.clip(row + half, 0, t) - jnp.clip(row - half, 0, t)
        p = s / count.astype(F32) - h
        y = _bdot(p, w_ref[g % POOL_HALF]) * ps_ref[:, cols]
        o_ref[:, out_cols] = xg + g_ref[:, cols] * y

    for half_idx in range(POOL_GROUPS // POOL_HALF):
        @pl.when(pl.program_id(1) == half_idx)
        def _(half_idx=half_idx):
            for gg in range(POOL_HALF):
                group(half_idx * POOL_HALF + gg, slice(gg * C_G, (gg + 1) * C_G))


def _pool_mixer(x, nw, scale_blk, shift_blk, w_pool, layer, pool_scale, gate_blk, t, ctx_blocks, t_ctx):
    m, d = x.shape
    wide = POOL_HALF * C_G
    vec = lambda: pl.BlockSpec((1, d), lambda b, j: (0, 0))
    blk_vec = lambda: pl.BlockSpec((None, 1, d), lambda b, j: (b, 0, 0))
    return pl.pallas_call(
        functools.partial(_pool_kernel, ctx_blocks=ctx_blocks, t_ctx=t_ctx),
        grid=(m // t, POOL_GROUPS // POOL_HALF),
        in_specs=[
            pl.BlockSpec((t, d), lambda b, j: (b, 0)),
            vec(), blk_vec(), blk_vec(),
            pl.BlockSpec((None, POOL_HALF, C_G, C_G), lambda b, j: (layer, j, 0, 0)),
            vec(), blk_vec(),
        ],
        out_specs=pl.BlockSpec((t, wide), lambda b, j: (b, j)),
        out_shape=jax.ShapeDtypeStruct((m, d), F32),
        compiler_params=_cparams("parallel", "arbitrary"),
        name="pool_mixer",
    )(x, nw.reshape(1, d), scale_blk, shift_blk, w_pool, pool_scale, gate_blk)


def _pad_cols(w, n):
    return jnp.pad(w, ((0, 0), (0, n - w.shape[1])))


def _block_diag_pairs(s):
    b, two, h, n, _ = s.shape
    sp = s.reshape(b, two, h // 2, 2, n, n)
    z = jnp.zeros_like(sp[:, :, :, 0])
    top = jnp.concatenate([sp[:, :, :, 0], z], -1)
    bot = jnp.concatenate([z, sp[:, :, :, 1]], -1)
    return jnp.concatenate([top, bot], -2)


def _diag_pairs(s_bd):
    b, two, hp, _, _ = s_bd.shape
    n = HEAD_A
    return jnp.stack([s_bd[..., :n, :n], s_bd[..., n:, n:]], 3).reshape(b, two, 2 * hp, n, n)


def kernel(x_prompt, x_sample, state_rwkv, state_delta, c, c_ctx, mod_w, mod_b, norm_mix_w, norm_ffn_w, norm_final_w, ab_w_in, ab_w_out, rwkv_mu, rwkv_w0, rwkv_w2, rwkv_a0, rwkv_a2, rwkv_g2, rwkv_k_k, rwkv_k_a, rwkv_r_k, rwkv_ln_w, rwkv_ln_b, gdn_conv_w, gdn_a_log, gdn_dt_bias, gdn_norm_w, pool_w, pool_scale, ffn_w_gate, ffn_w_up, ffn_w_down):
    bp, tp, d = x_prompt.shape
    bs, ts, _ = x_sample.shape
    mp, ms = bp * tp, bs * ts
    m = mp + ms
    assert tp == ROW_BLK and ts % ROW_BLK == 0 and mp % ts == 0
    x = jnp.concatenate([x_prompt.reshape(mp, d), x_sample.reshape(ms, d)], 0)

    n_cond = 16
    cond = jnp.zeros((n_cond, d), F32).at[0].set(c_ctx).at[1:1 + bs].set(c)
    mod = _mod_all(cond, mod_w, mod_b).reshape(DEPTH, n_cond, 6, d)
    mod_blk = jnp.concatenate([
        jnp.broadcast_to(mod[:, 0:1], (DEPTH, mp // ROW_BLK, 6, d)),
        jnp.repeat(mod[:, 1:1 + bs], ts // ROW_BLK, axis=1)], 1)[:, :, :, None, :]
    zeros_blk = jnp.zeros((m // ROW_BLK, 1, d), F32)

    zero_r = jnp.zeros((bp, 2, H_A // 2, LANES, LANES), F32)
    zero_d = jnp.zeros((bp, 2, H_B, HEAD_B, HEAD_B), F32)
    new_r, new_d = [], []

    for l in range(DEPTH):
        shift_m, scale_m, gate_m, shift_f, scale_f, gate_f = (mod_blk[l, :, j] for j in range(6))
        i = l // 2
        if l % 2 == 0:
            w_in = ab_w_in[i]
            w_p = jnp.concatenate([
                w_in[:, C_PA:C_PA + 4 * C_B], w_in[:, :C_PA], w_in[:, C_PA + 4 * C_B:],
                jnp.zeros((d, C_PA_PAD - C_PA - 4 * H_B), F32)], 1).astype(BF16)
            p = _norm_matmul(x, norm_mix_w[l], scale_m, shift_m, w_p)
            mu = _pad_cols(rwkv_mu[i][None, :], C_PA_PAD)
            pam = _token_shift(p, mu, ts, mp // ts, tp)
            w2bd = jnp.zeros((2, 2 * LORA, C_A), F32).at[0, :LORA].set(rwkv_w2[i, 0]).at[1, LORA:].set(rwkv_w2[i, 1])
            a2bd = jnp.zeros((2, 2 * LORA, C_A), F32).at[0, :LORA].set(rwkv_a2[i, 0]).at[1, LORA:].set(rwkv_a2[i, 1])
            g2pad = jnp.pad(rwkv_g2[i], ((0, C_LO - 4 * LORA - GATE_LORA), (0, 0)))
            lora = (w2bd.astype(BF16), a2bd.astype(BF16), g2pad.astype(BF16),
                    rwkv_w0[i][:, None, :], rwkv_a0[i][:, None, :])
            vecs = [v_[i].reshape(1, C_A) for v_ in (rwkv_k_k, rwkv_k_a, rwkv_r_k, rwkv_ln_w, rwkv_ln_b)]
            oa_p, sr_p = _rwkv_scan(pam, *lora, *vecs, zero_r, bp, tp, 0, H_A // 2)
            oa_s, _ = _rwkv_scan(pam, *lora, *vecs, _block_diag_pairs(state_rwkv[:, i]), bs, ts, mp // ts, H_A // 2)
            gnw = gdn_norm_w[i].reshape(1, HEAD_B)
            on_decay_lanes = lambda p_: jnp.zeros((1, LANES), F32).at[
                0, BA_LANE0 + 2 * H_B:BA_LANE0 + 4 * H_B].set(p_.reshape(2 * H_B))
            alog_vec, dtb_vec = on_decay_lanes(gdn_a_log[i]), on_decay_lanes(gdn_dt_bias[i])
            ob_p, sd_p = _gdn_scan(p, gdn_conv_w[i], alog_vec, dtb_vec, gnw, zero_d, bp, tp, 0, heads=H_B)
            ob_s, _ = _gdn_scan(p, gdn_conv_w[i], alog_vec, dtb_vec, gnw, state_delta[:, i],
                                bs, ts, mp // ts)
            new_r.append(_diag_pairs(sr_p))
            new_d.append(sd_p)
            o_a = jnp.concatenate([oa_p, oa_s], 0)
            o_b = jnp.concatenate([ob_p, ob_s], 0)
            x = _matmul_residual([o_a, o_b], ab_w_out, i, x, gate_m, tm=1024)
        else:
            per_blk = lambda j: jnp.concatenate(
                [jnp.broadcast_to(mod[l, 0:1, j], (mp // ts, d)), mod[l, 1:1 + bs, j]], 0)[:, None, :]
            x = _pool_mixer(x, norm_mix_w[l], per_blk(1), per_blk(0), pool_w, i, pool_scale[i].reshape(1, d),
                            per_blk(2), ts, mp // ts, tp)
        act = _ffn_up(x, norm_ffn_w[l], scale_f, shift_f, ffn_w_gate, ffn_w_up, l)
        x = _matmul_residual([act], ffn_w_down, l, x, gate_f, single_buffer_w=True)

    y_prompt = _norm_mod(x, norm_final_w, zeros_blk, zeros_blk, F32, 0, mp // ROW_BLK).reshape(bp, tp, d)
    y_sample = _norm_mod(x, norm_final_w, zeros_blk, zeros_blk, F32, mp // ROW_BLK, ms // ROW_BLK).reshape(bs, ts, d)
    return (y_prompt, y_sample, jnp.stack(new_r, 1), jnp.stack(new_d, 1))
```
